```python
import math
import jax
import jax.numpy as jnp
from jax import lax
import numpy as np

D_MODEL = 1024
BATCH = 8
SEQ = 2048
DEPTH = 2

GLA_HEADS = 4
GLA_DK = 64
GLA_DV = 128
GLA_GATE_RANK = 16
GLA_GATE_TEMP = 16.0
GLA_CHUNK = 64
DIFF_HEADS = 4
DIFF_DK = 64
DIFF_DV = 128
N_BUCKETS = 32
MAX_DISTANCE = 128
MLA_HEADS = 16
MLA_Q_LORA = 256
MLA_KV_LORA = 128
MLA_NOPE = 64
MLA_ROPE = 32
MLA_DV = 64
ROPE_THETA = 10000.0
D_FF = 2816
N_EXPERTS = 8
TOP_K = 2
Q_BLOCK = 128
DEEPNORM_ALPHA = (2 * DEPTH) ** 0.25
DEEPNORM_BETA = (8 * DEPTH) ** -0.25
N_EVEN = (DEPTH + 1) // 2
N_ODD = DEPTH // 2
EVEN_IN_WIDTHS = (GLA_HEADS * GLA_DK, GLA_HEADS * GLA_DK, GLA_HEADS * GLA_DV, GLA_HEADS * GLA_DV,
                  2 * GLA_GATE_RANK,
                  DIFF_HEADS * 2 * DIFF_DK, DIFF_HEADS * 2 * DIFF_DK, DIFF_HEADS * DIFF_DV)
EVEN_IN = sum(EVEN_IN_WIDTHS)
EVEN_OUT = GLA_HEADS * GLA_DV + DIFF_HEADS * DIFF_DV
ODD_IN = MLA_Q_LORA + MLA_KV_LORA + MLA_ROPE
ODD_OUT = MLA_HEADS * MLA_DV

kernel_name = 'hybrid_gla_diff_mla_moe_encoder'


def _split_widths(t, widths):
    idx, acc = [], 0
    for w in widths[:-1]:
        acc += w
        idx.append(acc)
    return jnp.split(t, idx, axis=-1)


def _layer_norm(x, g, b, eps=1e-5):
    xf = x.astype(jnp.float32)
    mu = jnp.mean(xf, axis=-1, keepdims=True)
    var = jnp.mean(jnp.square(xf - mu), axis=-1, keepdims=True)
    y = (xf - mu) * lax.rsqrt(var + eps) * g.astype(jnp.float32) + b.astype(jnp.float32)
    return y.astype(x.dtype)


def _rms_norm(x, g, eps=1e-6):
    xf = x.astype(jnp.float32)
    y = xf * lax.rsqrt(jnp.mean(jnp.square(xf), axis=-1, keepdims=True) + eps) * g.astype(jnp.float32)
    return y.astype(x.dtype)


def _rope(t, positions):
    half = t.shape[-1] // 2
    inv = ROPE_THETA ** (-jnp.arange(half, dtype=jnp.float32) / half)
    ang = positions.astype(jnp.float32)[:, None] * inv[None, :]
    cos, sin = jnp.cos(ang), jnp.sin(ang)
    tf = t.astype(jnp.float32)
    t1, t2 = tf[..., :half], tf[..., half:]
    return jnp.concatenate([t1 * cos - t2 * sin, t1 * sin + t2 * cos], axis=-1).astype(t.dtype)


def _relative_bucket(rel):
    half = N_BUCKETS // 2
    max_exact = half // 2
    bucket = jnp.where(rel > 0, half, 0).astype(jnp.int32)
    n = jnp.abs(rel)
    n_large = max_exact + (jnp.log(jnp.maximum(n, max_exact).astype(jnp.float32) / max_exact)
                           / math.log(MAX_DISTANCE / max_exact) * (half - max_exact)).astype(jnp.int32)
    n_large = jnp.minimum(n_large, half - 1)
    return bucket + jnp.where(n < max_exact, n, n_large)


def _block_softmax_attention(q, k, v, map_weights, bias_table):
    B, H, M, S, dq = q.shape
    dv = v.shape[-1]
    nb = S // Q_BLOCK
    scale = dq ** -0.5
    q_blocks = jnp.moveaxis(q.reshape(B, H, M, nb, Q_BLOCK, dq), 3, 0)
    starts = jnp.arange(nb, dtype=jnp.int32) * Q_BLOCK
    k_pos = jnp.arange(S, dtype=jnp.int32)
    mw = map_weights.astype(jnp.float32)

    def one_block(args):
        qb, start = args
        s = jnp.einsum('bhmqd,bhmkd->bhmqk', qb, k).astype(jnp.float32) * scale
        if bias_table is not None:
            q_pos = start + jnp.arange(Q_BLOCK, dtype=jnp.int32)
            bucket = _relative_bucket(k_pos[None, :] - q_pos[:, None])
            bias = jnp.moveaxis(bias_table[bucket], -1, 0).astype(jnp.float32)
            s = s + bias[None, :, None]
        p = jax.nn.softmax(s, axis=-1)
        a = jnp.einsum('m,bhmqk->bhqk', mw, p)
        return jnp.einsum('bhqk,bhkd->bhqd', a.astype(v.dtype), v)

    out = lax.map(one_block, (q_blocks, starts))
    return jnp.moveaxis(out, 0, 2).reshape(B, H, S, dv)


def _gla_chunked(q, k, v, log_a):
    out_dtype = v.dtype
    B, H, S, dk = q.shape
    dv = v.shape[-1]
    n = S // GLA_CHUNK
    C = GLA_CHUNK
    q = q.astype(jnp.float32).reshape(B, H, n, C, dk)
    k = k.astype(jnp.float32).reshape(B, H, n, C, dk)
    v = v.astype(jnp.float32).reshape(B, H, n, C, dv)
    b = jnp.cumsum(log_a.astype(jnp.float32).reshape(B, H, n, C, dk), axis=3)
    b_last = b[:, :, :, -1:, :]
    q_dec = q * jnp.exp(b)
    mask = jnp.tril(jnp.ones((C, C), jnp.float32))
    attn = jnp.einsum('bhnid,bhnjd->bhnij', q_dec, k * jnp.exp(-b)) * mask
    o_intra = jnp.einsum('bhnij,bhnjv->bhniv', attn, v)
    chunk_kv = jnp.einsum('bhnjd,bhnjv->bhndv', k * jnp.exp(b_last - b), v)
    decay = jnp.exp(b_last[:, :, :, 0, :])

    def step(state, inp):
        dec, kv = inp
        return dec[..., None] * state + kv, state

    init = jnp.zeros((B, H, dk, dv), jnp.float32)
    _, s_before = lax.scan(step, init, (jnp.moveaxis(decay, 2, 0), jnp.moveaxis(chunk_kv, 2, 0)))
    s_before = jnp.moveaxis(s_before, 0, 2)
    o_inter = jnp.einsum('bhnid,bhndv->bhniv', q_dec, s_before)
    return (o_intra + o_inter).reshape(B, H, S, dv).astype(out_dtype)


def _even_mixer(x, w_in, gate_up, gate_bias, gla_gain, diff_lambda, diff_gain, w_out, rel_bias_table, lam_init):
    B, S, _ = x.shape
    q_g, k_g, v_g, g_g, a_lr, q_d, k_d, v_d = _split_widths(x @ w_in, EVEN_IN_WIDTHS)

    def heads(t, h, d):
        return t.reshape(B, S, h, d).transpose(0, 2, 1, 3)

    qg = heads(q_g, GLA_HEADS, GLA_DK) * (GLA_DK ** -0.5)
    kg = heads(k_g, GLA_HEADS, GLA_DK)
    vg = heads(v_g, GLA_HEADS, GLA_DV)
    a_lr = a_lr.reshape(B, S, 2, GLA_GATE_RANK)
    gate_logits = jnp.einsum('bsdr,drk->dbsk', a_lr, gate_up) + gate_bias[:, None, None, :]
    log_a = jax.nn.log_sigmoid(gate_logits.astype(jnp.float32)) / GLA_GATE_TEMP
    la_f = heads(log_a[0], GLA_HEADS, GLA_DK)
    la_b = heads(log_a[1], GLA_HEADS, GLA_DK)
    o_f = _gla_chunked(qg, kg, vg, la_f)
    flip = lambda t: jnp.flip(t, axis=2)
    o_b = flip(_gla_chunked(flip(qg), flip(kg), flip(vg), flip(la_b)))
    o_gla = _rms_norm(o_f + o_b, gla_gain)
    o_gla = o_gla.transpose(0, 2, 1, 3).reshape(B, S, GLA_HEADS * GLA_DV) * jax.nn.silu(g_g)

    qd = q_d.reshape(B, S, DIFF_HEADS, 2, DIFF_DK).transpose(0, 2, 3, 1, 4)
    kd = k_d.reshape(B, S, DIFF_HEADS, 2, DIFF_DK).transpose(0, 2, 3, 1, 4)
    vd = heads(v_d, DIFF_HEADS, DIFF_DV)
    lf = diff_lambda.astype(jnp.float32)
    lam = jnp.exp(jnp.sum(lf[0] * lf[1])) - jnp.exp(jnp.sum(lf[2] * lf[3])) + lam_init
    map_w = jnp.stack([jnp.ones((), jnp.float32), -lam])
    o_d = _block_softmax_attention(qd, kd, vd, map_w, rel_bias_table)
    o_d = _rms_norm(o_d, diff_gain) * (1.0 - lam_init)
    o_diff = o_d.transpose(0, 2, 1, 3).reshape(B, S, DIFF_HEADS * DIFF_DV)

    return jnp.concatenate([o_gla, o_diff], axis=-1) @ w_out


def _mla_mixer(x, w_in, q_gain, kv_gain, w_uq, w_ukv, w_out):
    B, S, _ = x.shape
    c_q, c_kv, k_r = _split_widths(x @ w_in, (MLA_Q_LORA, MLA_KV_LORA, MLA_ROPE))
    q = (_rms_norm(c_q, q_gain) @ w_uq).reshape(B, S, MLA_HEADS, MLA_NOPE + MLA_ROPE).transpose(0, 2, 1, 3)
    kv = (_rms_norm(c_kv, kv_gain) @ w_ukv).reshape(B, S, MLA_HEADS, MLA_NOPE + MLA_DV).transpose(0, 2, 1, 3)
    pos = jnp.arange(S, dtype=jnp.int32)
    q = jnp.concatenate([q[..., :MLA_NOPE], _rope(q[..., MLA_NOPE:], pos)], axis=-1)
    k_rope = jnp.broadcast_to(_rope(k_r, pos)[:, None], (B, MLA_HEADS, S, MLA_ROPE))
    k = jnp.concatenate([kv[..., :MLA_NOPE], k_rope], axis=-1)
    v = kv[..., MLA_NOPE:]
    o = _block_softmax_attention(q[:, :, None], k[:, :, None], v, jnp.ones((1,), jnp.float32), None)
    return o.transpose(0, 2, 1, 3).reshape(B, S, ODD_OUT) @ w_out


def _swiglu(x, w_gate, w_up, w_down):
    return (jax.nn.silu(x @ w_gate) * (x @ w_up)) @ w_down


def _moe_swiglu(x, router_w, w_gate, w_up, w_down):
    B, S, D = x.shape
    t = x.reshape(B * S, D)
    logits = (t @ router_w).astype(jnp.float32)
    top_vals, top_idx = lax.top_k(logits, TOP_K)
    w = jax.nn.softmax(top_vals, axis=-1)
    combine = jnp.einsum('nk,nke->ne', w, jax.nn.one_hot(top_idx, N_EXPERTS, dtype=jnp.float32))
    out = jnp.zeros_like(t)
    for e in range(N_EXPERTS):
        out = out + combine[:, e:e + 1].astype(t.dtype) * _swiglu(t, w_gate[e], w_up[e], w_down[e])
    return out.reshape(B, S, D)


def setup_inputs(seed: int = 0) -> dict:
    key = jax.random.key(seed)
    ks = iter(jax.random.split(key, 32))

    def dense(shape, fan_in, scale=1.0):
        return jax.random.normal(next(ks), shape, jnp.float32) * (scale * fan_in ** -0.5)

    def gain(shape):
        return 1.0 + 0.02 * jax.random.normal(next(ks), shape, jnp.float32)

    def small(shape, s):
        return s * jax.random.normal(next(ks), shape, jnp.float32)

    beta = DEEPNORM_BETA
    return {
        'x': jax.random.normal(next(ks), (BATCH, SEQ, D_MODEL), jnp.float32),
        'rel_bias_table': small((N_BUCKETS, DIFF_HEADS), 0.5),
        'even_w_in': dense((N_EVEN, D_MODEL, EVEN_IN), D_MODEL),
        'gla_gate_up': dense((N_EVEN, 2, GLA_GATE_RANK, GLA_HEADS * GLA_DK), GLA_GATE_RANK),
        'gla_gate_bias': small((N_EVEN, 2, GLA_HEADS * GLA_DK), 0.1),
        'gla_norm_gain': gain((N_EVEN, GLA_DV)),
        'diff_lambda': small((N_EVEN, 4, DIFF_DK), 0.1),
        'diff_norm_gain': gain((N_EVEN, DIFF_DV)),
        'even_w_out': dense((N_EVEN, EVEN_OUT, D_MODEL), EVEN_OUT, beta),
        'ffn_w_gate': dense((N_EVEN, D_MODEL, D_FF), D_MODEL),
        'ffn_w_up': dense((N_EVEN, D_MODEL, D_FF), D_MODEL),
        'ffn_w_down': dense((N_EVEN, D_FF, D_MODEL), D_FF, beta),
        'odd_w_in': dense((N_ODD, D_MODEL, ODD_IN), D_MODEL),
        'mla_q_norm_gain': gain((N_ODD, MLA_Q_LORA)),
        'mla_kv_norm_gain': gain((N_ODD, MLA_KV_LORA)),
        'mla_w_uq': dense((N_ODD, MLA_Q_LORA, MLA_HEADS * (MLA_NOPE + MLA_ROPE)), MLA_Q_LORA),
        'mla_w_ukv': dense((N_ODD, MLA_KV_LORA, MLA_HEADS * (MLA_NOPE + MLA_DV)), MLA_KV_LORA),
        'odd_w_out': dense((N_ODD, ODD_OUT, D_MODEL), ODD_OUT, beta),
        'router_w': dense((N_ODD, D_MODEL, N_EXPERTS), D_MODEL),
        'moe_w_gate': dense((N_ODD, N_EXPERTS, D_MODEL, D_FF), D_MODEL),
        'moe_w_up': dense((N_ODD, N_EXPERTS, D_MODEL, D_FF), D_MODEL),
        'moe_w_down': dense((N_ODD, N_EXPERTS, D_FF, D_MODEL), D_FF, beta),
        'ln_gain': gain((DEPTH, 2, D_MODEL)),
        'ln_bias': small((DEPTH, 2, D_MODEL), 0.02),
    }


def reference(x, rel_bias_table, even_w_in, gla_gate_up, gla_gate_bias, gla_norm_gain, diff_lambda,
              diff_norm_gain, even_w_out, ffn_w_gate, ffn_w_up, ffn_w_down, odd_w_in, mla_q_norm_gain,
              mla_kv_norm_gain, mla_w_uq, mla_w_ukv, odd_w_out, router_w, moe_w_gate, moe_w_up, moe_w_down,
              ln_gain, ln_bias):
    for layer in range(DEPTH):
        i = layer // 2
        if layer % 2 == 0:
            lam_init = 0.8 - 0.6 * math.exp(-0.3 * layer)
            h = _even_mixer(x, even_w_in[i], gla_gate_up[i], gla_gate_bias[i], gla_norm_gain[i],
                            diff_lambda[i], diff_norm_gain[i], even_w_out[i], rel_bias_table, lam_init)
            x = _layer_norm(DEEPNORM_ALPHA * x + h, ln_gain[layer, 0], ln_bias[layer, 0])
            f = _swiglu(x, ffn_w_gate[i], ffn_w_up[i], ffn_w_down[i])
        else:
            h = _mla_mixer(x, odd_w_in[i], mla_q_norm_gain[i], mla_kv_norm_gain[i], mla_w_uq[i],
                           mla_w_ukv[i], odd_w_out[i])
            x = _layer_norm(DEEPNORM_ALPHA * x + h, ln_gain[layer, 0], ln_bias[layer, 0])
            f = _moe_swiglu(x, router_w[i], moe_w_gate[i], moe_w_up[i], moe_w_down[i])
        x = _layer_norm(DEEPNORM_ALPHA * x + f, ln_gain[layer, 1], ln_bias[layer, 1])
    return x
```

```python
import functools
import math

import jax
import jax.numpy as jnp
from jax import lax
from jax.experimental import pallas as pl
from jax.experimental.pallas import tpu as pltpu

F32 = jnp.float32
BF16 = jnp.bfloat16

LANES = 128
V7X_VMEM_BYTES = 64 * 1024 * 1024
VMEM_LIMIT = V7X_VMEM_BYTES - 8 * 1024 * 1024

D_MODEL = 1024
GLA_HEADS, GLA_DK, GLA_DV = 4, 64, 128
GLA_RANK, GLA_TEMP, GLA_CHUNK = 16, 16.0, 64
DIFF_HEADS, DIFF_DK, DIFF_DV = 4, 64, 128
N_BUCKETS, MAX_DISTANCE = 32, 128
MLA_HEADS, MLA_Q_LORA, MLA_KV_LORA = 16, 256, 128
MLA_NOPE, MLA_ROPE, MLA_DV = 64, 32, 64
ROPE_THETA = 10000.0
D_FF, N_EXPERTS, TOP_K = 2816, 8, 2
DEPTH = 2
ALPHA = (2 * DEPTH) ** 0.25
LN_EPS, RMS_EPS = 1e-5, 1e-6

COL_QK_G, COL_V_G, COL_G_G = 0, 512, 1024
COL_Q_D, COL_K_D, COL_V_D, COL_ALR = 1536, 2048, 2560, 3072
EVEN_COLS = 3200

GLA_GROUP = 256
FF_CHUNK = 256
MOE_TILE = 512


def _cparams(sem):
    return pltpu.CompilerParams(dimension_semantics=sem, vmem_limit_bytes=VMEM_LIMIT)


def _layer_norm(y, g, b):
    mu = jnp.mean(y, axis=-1, keepdims=True)
    d = y - mu
    var = jnp.mean(d * d, axis=-1, keepdims=True)
    return d * lax.rsqrt(var + LN_EPS) * g + b


def _split3(x):
    h1 = x.astype(BF16)
    r1 = x - h1.astype(F32)
    h2 = r1.astype(BF16)
    h3 = (r1 - h2.astype(F32)).astype(BF16)
    return h1, h2, h3


def _dot(a, b):
    return jnp.dot(a, b, preferred_element_type=F32)


def _dot_nt(a, b):
    return lax.dot_general(a, b, (((1,), (1,)), ((), ())), preferred_element_type=F32)


def _dot_tn(a, b):
    return lax.dot_general(a, b, (((0,), (0,)), ((), ())), preferred_element_type=F32)


def _matmul_kernel(x_ref, w_ref, o_ref):
    o_ref[...] = _dot(x_ref[...].astype(BF16), w_ref[...]).astype(o_ref.dtype)


def _matmul(x, w, tm, out_dtype):
    n, k = x.shape
    m = w.shape[1]
    return pl.pallas_call(
        _matmul_kernel,
        grid=(n // tm,),
        in_specs=[pl.BlockSpec((tm, k), lambda i: (i, 0)),
                  pl.BlockSpec((k, m), lambda i: (0, 0))],
        out_specs=pl.BlockSpec((tm, m), lambda i: (i, 0)),
        out_shape=jax.ShapeDtypeStruct((n, m), out_dtype),
        compiler_params=_cparams(("parallel",)),
        name="in_proj",
    )(x, w)


def _gla_kernel(qk_ref, v_ref, g_ref, alr_ref, gup_ref, gbias_ref, gain_ref, o_ref,
                la_ref, qdec_ref, kdec_ref, decay_ref, acc_ref, st_ref):
    seq = qk_ref.shape[0]
    hk = GLA_HEADS * GLA_DK
    hv = GLA_HEADS * GLA_DV
    grp = GLA_GROUP
    c = GLA_CHUNK

    alr = alr_ref[...]
    for d in range(2):
        logits = _dot(alr, gup_ref[d]) + gbias_ref[d]
        la_ref[d] = jax.nn.log_sigmoid(logits) / GLA_TEMP

    row = lax.broadcasted_iota(jnp.int32, (grp, grp), 0)
    col = lax.broadcasted_iota(jnp.int32, (grp, grp), 1)
    same = (row // c) == (col // c)
    tri = [same & (row >= col), same & (row <= col)]
    cum_inc = [jnp.where(t, 1.0, 0.0).astype(BF16) for t in tri]
    cum_rem = [jnp.where(same & (row < col), 1.0, 0.0).astype(BF16),
               jnp.where(same & (row > col), 1.0, 0.0).astype(BF16)]
    lane_k = lax.broadcasted_iota(jnp.int32, (grp, hk), 1) // GLA_DK
    scale = GLA_DK ** -0.5

    def group_body(r, carry):
        rows = pl.ds(pl.multiple_of(r * grp, grp), grp)
        q = qk_ref[rows, 0:hk].astype(F32)
        k = qk_ref[rows, hk:2 * hk].astype(F32)
        v = v_ref[rows, :]
        o_heads = [jnp.zeros((grp, GLA_DV), F32) for _ in range(GLA_HEADS)]
        for d in range(2):
            parts = _split3(la_ref[d, rows, :])
            bcum = sum(_dot(cum_inc[d], p) for p in parts)
            brem = sum(_dot(cum_rem[d], p) for p in parts)
            q_dec = q * jnp.exp(bcum) * scale
            k_inv = (k * jnp.exp(-bcum)).astype(BF16)
            qdec_ref[d, rows, :] = q_dec.astype(BF16)
            kdec_ref[d, rows, :] = (k * jnp.exp(brem)).astype(BF16)
            decay_ref[d, rows, :] = jnp.exp(bcum + brem)
            for h in range(GLA_HEADS):
                q_h = jnp.where(lane_k == h, q_dec, 0.0).astype(BF16)
                attn = jnp.where(tri[d], _dot_nt(q_h, k_inv), 0.0)
                o_heads[h] = o_heads[h] + _dot(attn.astype(BF16), v[:, h * GLA_DV:(h + 1) * GLA_DV])
        for h in range(GLA_HEADS):
            acc_ref[rows, h * GLA_DV:(h + 1) * GLA_DV] = o_heads[h]
        return carry

    lax.fori_loop(0, seq // grp, group_body, 0)

    srow = lax.broadcasted_iota(jnp.int32, (hv, hk), 0) // GLA_DV
    scol = lax.broadcasted_iota(jnp.int32, (hv, hk), 1) // GLA_DK
    head_mask = srow == scol
    n_chunks = seq // c
    st_ref[...] = jnp.zeros_like(st_ref)

    def chunk_body(i, carry):
        for d in range(2):
            ci = i if d == 0 else n_chunks - 1 - i
            rows = pl.ds(pl.multiple_of(ci * c, c), c)
            state = st_ref[d]
            acc_ref[rows, :] += _dot_nt(qdec_ref[d, rows, :], state.astype(BF16))
            kv = _dot_tn(v_ref[rows, :], kdec_ref[d, rows, :])
            decay = decay_ref[d, pl.ds(pl.multiple_of(ci * c, c), 1), :]
            st_ref[d] = state * decay + jnp.where(head_mask, kv, 0.0)
        return carry

    lax.fori_loop(0, n_chunks, chunk_body, 0)

    gain = gain_ref[...]
    for h in range(GLA_HEADS):
        sl = slice(h * GLA_DV, (h + 1) * GLA_DV)
        o = acc_ref[:, sl]
        y = o * lax.rsqrt(jnp.mean(o * o, axis=-1, keepdims=True) + RMS_EPS) * gain
        gate = g_ref[:, sl].astype(F32)
        o_ref[:, sl] = (y * (gate * jax.nn.sigmoid(gate))).astype(o_ref.dtype)


def _gla(proj, gup, gbias, gain):
    b, s, _ = proj.shape
    hk, hv = GLA_HEADS * GLA_DK, GLA_HEADS * GLA_DV
    blk = lambda width, col: pl.BlockSpec((None, s, width), lambda i: (i, 0, col // width))
    return pl.pallas_call(
        _gla_kernel,
        grid=(b,),
        in_specs=[blk(2 * hk, COL_QK_G), blk(hv, COL_V_G), blk(hv, COL_G_G), blk(LANES, COL_ALR),
                  pl.BlockSpec((2, LANES, hk), lambda i: (0, 0, 0)),
                  pl.BlockSpec((2, 1, hk), lambda i: (0, 0, 0)),
                  pl.BlockSpec((1, GLA_DV), lambda i: (0, 0))],
        out_specs=pl.BlockSpec((None, s, hv), lambda i: (i, 0, 0)),
        out_shape=jax.ShapeDtypeStruct((b, s, hv), BF16),
        scratch_shapes=[pltpu.VMEM((2, s, hk), F32),
                        pltpu.VMEM((2, s, hk), BF16),
                        pltpu.VMEM((2, s, hk), BF16),
                        pltpu.VMEM((2, s, hk), F32),
                        pltpu.VMEM((s, hv), F32),
                        pltpu.VMEM((2, hv, hk), F32)],
        compiler_params=_cparams(("parallel",)),
        name="gla",
    )(proj, proj, proj, proj, gup, gbias, gain)


def _bias_kernel(tab_ref, o_ref, *, tile, n_heads):
    h = pl.program_id(0)
    i = pl.program_id(1)
    half = N_BUCKETS // 2
    max_exact = half // 2
    far_neg = tab_ref[(half - 1) * n_heads + h]
    far_pos = tab_ref[(N_BUCKETS - 1) * n_heads + h]
    n_blocks = o_ref.shape[1] // tile
    for j in range(n_blocks):
        d = j - i
        cols = slice(j * tile, (j + 1) * tile)

        @pl.when(d <= -2)
        def _():
            o_ref[:, cols] = jnp.full((tile, tile), far_neg, F32)

        @pl.when(d >= 2)
        def _():
            o_ref[:, cols] = jnp.full((tile, tile), far_pos, F32)

        @pl.when((d >= -1) & (d <= 1))
        def _():
            rel = (d * tile + lax.broadcasted_iota(jnp.int32, (tile, tile), 1)
                   - lax.broadcasted_iota(jnp.int32, (tile, tile), 0))
            bucket = jnp.where(rel > 0, half, 0).astype(jnp.int32)
            n = jnp.abs(rel)
            n_large = max_exact + (jnp.log(jnp.maximum(n, max_exact).astype(F32) / max_exact)
                                   / math.log(MAX_DISTANCE / max_exact) * (half - max_exact)).astype(jnp.int32)
            n_large = jnp.minimum(n_large, half - 1)
            bucket = bucket + jnp.where(n < max_exact, n, n_large)
            val = jnp.zeros((tile, tile), F32)
            for bkt in range(N_BUCKETS):
                val = jnp.where(bucket == bkt, tab_ref[bkt * n_heads + h], val)
            o_ref[:, cols] = val


def _rel_bias(table, seq, tile):
    assert tile >= MAX_DISTANCE and seq % tile == 0
    n_heads = table.shape[1]
    return pl.pallas_call(
        functools.partial(_bias_kernel, tile=tile, n_heads=n_heads),
        grid=(n_heads, seq // tile),
        in_specs=[pl.BlockSpec(memory_space=pltpu.SMEM)],
        out_specs=pl.BlockSpec((None, tile, seq), lambda h, i: (h, i, 0)),
        out_shape=jax.ShapeDtypeStruct((n_heads, seq, seq), F32),
        compiler_params=_cparams(("parallel", "parallel")),
        name="rel_bias",
    )(table.reshape(-1))


def _softmax_pv(s, v):
    m = jnp.max(s, axis=-1, keepdims=True)
    p = jnp.exp(s - m)
    l = jnp.sum(p, axis=-1, keepdims=True)
    return _dot(p.astype(BF16), v) / l


def _diff_kernel(lam_ref, q_ref, k_ref, v_ref, bias_ref, gain_ref, o_ref, *, out_scale):
    q = q_ref[...]
    k = k_ref[...]
    v = v_ref[...]
    bias = bias_ref[...]
    lane = lax.broadcasted_iota(jnp.int32, q.shape, 1)
    scale = DIFF_DK ** -0.5
    qs = (q.astype(F32) * scale).astype(BF16)
    zero = jnp.zeros_like(qs)
    o0 = _softmax_pv(_dot_nt(jnp.where(lane < DIFF_DK, qs, zero), k) + bias, v)
    o1 = _softmax_pv(_dot_nt(jnp.where(lane >= DIFF_DK, qs, zero), k) + bias, v)
    o = o0 - lam_ref[0] * o1
    y = o * lax.rsqrt(jnp.mean(o * o, axis=-1, keepdims=True) + RMS_EPS) * gain_ref[...]
    o_ref[...] = (y * out_scale).astype(o_ref.dtype)


def _diff_attention(proj, bias, lam, gain, lam_init, tq):
    b, s, _ = proj.shape
    h = DIFF_HEADS
    w = 2 * DIFF_DK
    assert w == LANES and DIFF_DV == LANES
    return pl.pallas_call(
        functools.partial(_diff_kernel, out_scale=1.0 - lam_init),
        grid=(h, s // tq, b),
        in_specs=[pl.BlockSpec(memory_space=pltpu.SMEM),
                  pl.BlockSpec((None, tq, w), lambda hh, i, bb: (bb, i, COL_Q_D // w + hh)),
                  pl.BlockSpec((None, s, w), lambda hh, i, bb: (bb, 0, COL_K_D // w + hh)),
                  pl.BlockSpec((None, s, DIFF_DV), lambda hh, i, bb: (bb, 0, COL_V_D // DIFF_DV + hh)),
                  pl.BlockSpec((None, tq, s), lambda hh, i, bb: (hh, i, 0)),
                  pl.BlockSpec((1, DIFF_DV), lambda hh, i, bb: (0, 0))],
        out_specs=pl.BlockSpec((None, tq, DIFF_DV), lambda hh, i, bb: (bb, i, hh)),
        out_shape=jax.ShapeDtypeStruct((b, s, h * DIFF_DV), BF16),
        compiler_params=_cparams(("parallel", "parallel", "parallel")),
        name="diff_attn",
    )(lam, proj, proj, proj, bias, gain)


def _outproj_kernel(*refs, n_in):
    a_refs = refs[:n_in]
    w_refs = refs[n_in:2 * n_in]
    x_ref, g_ref, b_ref, o_ref = refs[2 * n_in:]
    h = sum(_dot(a[...], w[...]) for a, w in zip(a_refs, w_refs))
    o_ref[...] = _layer_norm(ALPHA * x_ref[...] + h, g_ref[...], b_ref[...])


def _outproj_ln(acts, weights, x, g, b, tm):
    n, d = x.shape
    n_in = len(acts)
    in_specs = ([pl.BlockSpec((tm, a.shape[1]), lambda i: (i, 0)) for a in acts]
                + [pl.BlockSpec(w.shape, lambda i: (0, 0)) for w in weights]
                + [pl.BlockSpec((tm, d), lambda i: (i, 0)),
                   pl.BlockSpec((1, d), lambda i: (0, 0)),
                   pl.BlockSpec((1, d), lambda i: (0, 0))])
    return pl.pallas_call(
        functools.partial(_outproj_kernel, n_in=n_in),
        grid=(n // tm,),
        in_specs=in_specs,
        out_specs=pl.BlockSpec((tm, d), lambda i: (i, 0)),
        out_shape=jax.ShapeDtypeStruct((n, d), F32),
        compiler_params=_cparams(("parallel",)),
        name="outproj_ln",
    )(*acts, *weights, x, g, b)


def _outproj_route_kernel(a_ref, w_ref, x_ref, g_ref, b_ref, rw_ref, o_ref, info_ref, cnt_ref, carry_ref):
    i = pl.program_id(0)
    tm = x_ref.shape[0]

    @pl.when(i == 0)
    def _():
        carry_ref[...] = jnp.zeros_like(carry_ref)

    xn = _layer_norm(ALPHA * x_ref[...] + _dot(a_ref[...], w_ref[...]), g_ref[...], b_ref[...])
    o_ref[...] = xn

    xs = _split3(xn)
    ws = _split3(rw_ref[...])
    logits = sum(_dot(xs[p], ws[q]) for p in range(3) for q in range(3) if p + q <= 2)
    lane = lax.broadcasted_iota(jnp.int32, logits.shape, 1).astype(F32)
    neg = jnp.float32(-jnp.inf)
    logits = jnp.where(lane < N_EXPERTS, logits, neg)
    v1 = jnp.max(logits, axis=-1, keepdims=True)
    e1 = jnp.min(jnp.where(logits == v1, lane, float(LANES)), axis=-1, keepdims=True)
    rest = jnp.where(lane == e1, neg, logits)
    v2 = jnp.max(rest, axis=-1, keepdims=True)
    e2 = jnp.min(jnp.where(rest == v2, lane, float(LANES)), axis=-1, keepdims=True)
    t = jnp.exp(v2 - v1)
    w1 = 1.0 / (1.0 + t)
    w2 = t / (1.0 + t)

    onehot = jnp.where((lane == e1) | (lane == e2), 1.0, 0.0)
    row = lax.broadcasted_iota(jnp.int32, (tm, tm), 0)
    col = lax.broadcasted_iota(jnp.int32, (tm, tm), 1)
    before = jnp.where(row > col, 1.0, 0.0).astype(BF16)
    prior = carry_ref[...] + _dot(before, onehot.astype(BF16))
    r1 = jnp.sum(jnp.where(lane == e1, prior, 0.0), axis=-1, keepdims=True)
    r2 = jnp.sum(jnp.where(lane == e2, prior, 0.0), axis=-1, keepdims=True)
    carry_ref[...] += jnp.sum(onehot, axis=0, keepdims=True)
    cnt_ref[...] = jnp.broadcast_to(carry_ref[...], cnt_ref.shape)

    info = jnp.zeros(logits.shape, F32)
    for idx, val in enumerate((e1, e2, w1, w2, r1, r2)):
        info = jnp.where(lane == idx, val, info)
    info_ref[...] = info


def _outproj_route(a, w, x, g, b, router_w, tm):
    n, d = x.shape
    return pl.pallas_call(
        _outproj_route_kernel,
        grid=(n // tm,),
        in_specs=[pl.BlockSpec((tm, a.shape[1]), lambda i: (i, 0)),
                  pl.BlockSpec(w.shape, lambda i: (0, 0)),
                  pl.BlockSpec((tm, d), lambda i: (i, 0)),
                  pl.BlockSpec((1, d), lambda i: (0, 0)),
                  pl.BlockSpec((1, d), lambda i: (0, 0)),
                  pl.BlockSpec((d, LANES), lambda i: (0, 0))],
        out_specs=[pl.BlockSpec((tm, d), lambda i: (i, 0)),
                   pl.BlockSpec((tm, LANES), lambda i: (i, 0)),
                   pl.BlockSpec((8, LANES), lambda i: (0, 0))],
        out_shape=[jax.ShapeDtypeStruct((n, d), F32),
                   jax.ShapeDtypeStruct((n, LANES), F32),
                   jax.ShapeDtypeStruct((8, LANES), F32)],
        scratch_shapes=[pltpu.VMEM((1, LANES), F32)],
        compiler_params=_cparams(("arbitrary",)),
        name="outproj_route",
    )(a, w, x, g, b, router_w)


def _swiglu_acc(xb, wg_ref, wu_ref, wd_ref, acc_ref):
    n_chunks = wg_ref.shape[-1] // FF_CHUNK
    for c in range(n_chunks):
        cols = slice(c * FF_CHUNK, (c + 1) * FF_CHUNK)
        gate = _dot(xb, wg_ref[:, cols])
        up = _dot(xb, wu_ref[:, cols])
        hidden = (gate * jax.nn.sigmoid(gate) * up).astype(BF16)
        part = _dot(hidden, wd_ref[cols, :])
        if c == 0:
            acc_ref[...] = part
        else:
            acc_ref[...] += part


def _ffn_ln_kernel(x_ref, wg_ref, wu_ref, wd_ref, g_ref, b_ref, o_ref):
    x = x_ref[...]
    _swiglu_acc(x.astype(BF16), wg_ref, wu_ref, wd_ref, o_ref)
    o_ref[...] = _layer_norm(ALPHA * x + o_ref[...], g_ref[...], b_ref[...])


def _ffn_ln(x, wg, wu, wd, g, b, tm):
    n, d = x.shape
    f = wg.shape[1]
    once = lambda shape: pl.BlockSpec(shape, lambda i: (0,) * len(shape), pipeline_mode=pl.Buffered(1))
    return pl.pallas_call(
        _ffn_ln_kernel,
        grid=(n // tm,),
        in_specs=[pl.BlockSpec((tm, d), lambda i: (i, 0)),
                  once((d, f)), once((d, f)), once((f, d)), once((1, d)), once((1, d))],
        out_specs=pl.BlockSpec((tm, d), lambda i: (i, 0)),
        out_shape=jax.ShapeDtypeStruct((n, d), F32),
        compiler_params=_cparams(("parallel",)),
        name="ffn_ln",
    )(x, wg, wu, wd, g, b)


def _moe_kernel(te_ref, nu_ref, x_ref, wg_ref, wu_ref, wd_ref, o_ref):
    t = pl.program_id(0)

    @pl.when(t < nu_ref[0])
    def _():
        _swiglu_acc(x_ref[...].astype(BF16), wg_ref, wu_ref, wd_ref, o_ref)

    @pl.when(t >= nu_ref[0])
    def _():
        o_ref[...] = jnp.zeros_like(o_ref)


def _moe_ffn(x_sorted, tile_expert, n_used, wg, wu, wd):
    p, d = x_sorted.shape
    f = wg.shape[2]
    tm = MOE_TILE
    grid_spec = pltpu.PrefetchScalarGridSpec(
        num_scalar_prefetch=2,
        grid=(p // tm,),
        in_specs=[pl.BlockSpec((tm, d), lambda t, te, nu: (jnp.minimum(t, nu[0] - 1), 0)),
                  pl.BlockSpec((None, d, f), lambda t, te, nu: (te[t], 0, 0)),
                  pl.BlockSpec((None, d, f), lambda t, te, nu: (te[t], 0, 0)),
                  pl.BlockSpec((None, f, d), lambda t, te, nu: (te[t], 0, 0))],
        out_specs=pl.BlockSpec((tm, d), lambda t, te, nu: (t, 0)),
    )
    return pl.pallas_call(
        _moe_kernel,
        grid_spec=grid_spec,
        out_shape=jax.ShapeDtypeStruct((p, d), F32),
        compiler_params=_cparams(("arbitrary",)),
        name="moe_ffn",
    )(tile_expert, n_used, x_sorted, wg, wu, wd)


def _scatter_kernel(pos_ref, pad_ref, x_ref, o_hbm, zero_ref, sem, zsem):
    i = pl.program_id(0)
    tm = x_ref.shape[0]
    n_tiles = o_hbm.shape[0] // MOE_TILE

    @pl.when(i == 0)
    def _():
        zero_ref[...] = jnp.zeros_like(zero_ref)
        zero_row = zero_ref.at[pl.ds(0, 1)]
        for e in range(N_EXPERTS):
            start, count = pad_ref[e], pad_ref[N_EXPERTS + e]

            def fill(r, carry):
                pltpu.make_async_copy(zero_row, o_hbm.at[pl.ds(start + r, 1)], zsem).start()
                return carry

            lax.fori_loop(0, count, fill, 0)

            def drain(r, carry):
                pltpu.make_async_copy(zero_row, o_hbm.at[pl.ds(start + r, 1)], zsem).wait()
                return carry

            lax.fori_loop(0, count, drain, 0)

        def fill_tile(t, carry):
            dst = o_hbm.at[pl.ds(pl.multiple_of(t * MOE_TILE, MOE_TILE), MOE_TILE)]
            cp = pltpu.make_async_copy(zero_ref, dst, zsem)
            cp.start()
            cp.wait()
            return carry

        lax.fori_loop(pad_ref[2 * N_EXPERTS], n_tiles, fill_tile, 0)

    def issue(r, carry):
        for kk in range(TOP_K):
            dst = pos_ref[0, kk, r]
            pltpu.make_async_copy(x_ref.at[pl.ds(r, 1)], o_hbm.at[pl.ds(dst, 1)], sem).start()
        return carry

    lax.fori_loop(0, tm, issue, 0)

    def drain_rows(r, carry):
        for kk in range(TOP_K):
            pltpu.make_async_copy(x_ref.at[pl.ds(r, 1)], o_hbm.at[pl.ds(0, 1)], sem).wait()
        return carry

    lax.fori_loop(0, tm, drain_rows, 0)


def _scatter_rows(x, pos, pad_info, n_slots, tm):
    n, d = x.shape
    return pl.pallas_call(
        _scatter_kernel,
        grid=(n // tm,),
        in_specs=[pl.BlockSpec((1, TOP_K, tm), lambda i: (i, 0, 0), memory_space=pltpu.SMEM),
                  pl.BlockSpec(memory_space=pltpu.SMEM),
                  pl.BlockSpec((tm, d), lambda i: (i, 0))],
        out_specs=pl.BlockSpec(memory_space=pl.ANY),
        out_shape=jax.ShapeDtypeStruct((n_slots, d), F32),
        scratch_shapes=[pltpu.VMEM((MOE_TILE, d), F32),
                        pltpu.SemaphoreType.DMA(()),
                        pltpu.SemaphoreType.DMA(())],
        compiler_params=_cparams(("arbitrary",)),
        name="scatter_rows",
    )(pos, pad_info, x)


def _combine_kernel(pos_ref, x_ref, info_ref, g_ref, b_ref, y_hbm, o_ref, buf_ref, sem):
    tm = x_ref.shape[0]

    def issue(r, carry):
        for kk in range(TOP_K):
            src = pos_ref[0, kk, r]
            pltpu.make_async_copy(y_hbm.at[pl.ds(src, 1)], buf_ref.at[kk, pl.ds(r, 1)], sem).start()
        return carry

    lax.fori_loop(0, tm, issue, 0)

    def drain(r, carry):
        for kk in range(TOP_K):
            pltpu.make_async_copy(y_hbm.at[pl.ds(0, 1)], buf_ref.at[kk, pl.ds(r, 1)], sem).wait()
        return carry

    lax.fori_loop(0, tm, drain, 0)

    info = info_ref[...]
    w1 = info[:, 2:3]
    w2 = info[:, 3:4]
    f = w1 * buf_ref[0] + w2 * buf_ref[1]
    o_ref[...] = _layer_norm(ALPHA * x_ref[...] + f, g_ref[...], b_ref[...])


def _combine_ln(x, info, pos, y_sorted, g, b, tm):
    n, d = x.shape
    return pl.pallas_call(
        _combine_kernel,
        grid=(n // tm,),
        in_specs=[pl.BlockSpec((1, TOP_K, tm), lambda i: (i, 0, 0), memory_space=pltpu.SMEM),
                  pl.BlockSpec((tm, d), lambda i: (i, 0)),
                  pl.BlockSpec((tm, LANES), lambda i: (i, 0)),
                  pl.BlockSpec((1, d), lambda i: (0, 0)),
                  pl.BlockSpec((1, d), lambda i: (0, 0)),
                  pl.BlockSpec(memory_space=pl.ANY)],
        out_specs=pl.BlockSpec((tm, d), lambda i: (i, 0)),
        out_shape=jax.ShapeDtypeStruct((n, d), F32),
        scratch_shapes=[pltpu.VMEM((TOP_K, tm, d), F32),
                        pltpu.SemaphoreType.DMA(())],
        compiler_params=_cparams(("arbitrary",)),
        name="combine_ln",
    )(pos, x, info, g, b, y_sorted)


def _mla_proj_kernel(x_ref, win_ref, qg_ref, kvg_ref, wq1_ref, wq2_ref, wk_ref, wv_ref,
                     cq_ref, sq_ref, ck_ref, sk_ref, q_ref, k_ref, v_ref):
    c = _dot(x_ref[...].astype(BF16), win_ref[...])
    cq = c[:, :MLA_Q_LORA]
    ckv = c[:, MLA_Q_LORA:MLA_Q_LORA + MLA_KV_LORA]
    off = MLA_Q_LORA + MLA_KV_LORA
    k_rope = c[:, off:off + LANES] * ck_ref[...] + c[:, off + LANES:off + 2 * LANES] * sk_ref[...]
    cq = (cq * lax.rsqrt(jnp.mean(cq * cq, axis=-1, keepdims=True) + RMS_EPS) * qg_ref[...]).astype(BF16)
    ckv = (ckv * lax.rsqrt(jnp.mean(ckv * ckv, axis=-1, keepdims=True) + RMS_EPS) * kvg_ref[...]).astype(BF16)
    q1 = _dot(cq, wq1_ref[...])
    q2 = _dot(cq, wq2_ref[...])
    k1 = _dot(ckv, wk_ref[...])
    v_ref[...] = _dot(ckv, wv_ref[...]).astype(v_ref.dtype)
    cos_q, sin_q = cq_ref[...], sq_ref[...]
    for h in range(MLA_HEADS):
        sl = slice(h * LANES, (h + 1) * LANES)
        q_ref[:, sl] = (q1[:, sl] * cos_q + q2[:, sl] * sin_q).astype(q_ref.dtype)
        k_ref[:, sl] = (k1[:, sl] + k_rope).astype(k_ref.dtype)


def _mla_proj(x, win, qg, kvg, wq1, wq2, wk, wv, tabs, seq, tm):
    n, d = x.shape
    per_seq = seq // tm
    full = lambda a: pl.BlockSpec(a.shape, lambda i: (0,) * a.ndim)
    tab = pl.BlockSpec((tm, LANES), lambda i: (i % per_seq, 0))
    hw = MLA_HEADS * LANES
    return pl.pallas_call(
        _mla_proj_kernel,
        grid=(n // tm,),
        in_specs=[pl.BlockSpec((tm, d), lambda i: (i, 0)), full(win), full(qg), full(kvg),
                  full(wq1), full(wq2), full(wk), full(wv), tab, tab, tab, tab],
        out_specs=[pl.BlockSpec((tm, hw), lambda i: (i, 0)),
                   pl.BlockSpec((tm, hw), lambda i: (i, 0)),
                   pl.BlockSpec((tm, MLA_HEADS * MLA_DV), lambda i: (i, 0))],
        out_shape=[jax.ShapeDtypeStruct((n, hw), BF16),
                   jax.ShapeDtypeStruct((n, hw), BF16),
                   jax.ShapeDtypeStruct((n, MLA_HEADS * MLA_DV), BF16)],
        compiler_params=_cparams(("parallel",)),
        name="mla_proj",
    )(x, win, qg, kvg, wq1, wq2, wk, wv, *tabs)


def _mla_attn_kernel(q_ref, k_ref, v_ref, o_ref):
    v = v_ref[...]
    outs = []
    for hh in range(2):
        sl = slice(hh * LANES, (hh + 1) * LANES)
        outs.append(_softmax_pv(_dot_nt(q_ref[:, sl], k_ref[:, sl]), v))
    lane = lax.broadcasted_iota(jnp.int32, outs[0].shape, 1)
    o_ref[...] = jnp.where(lane < MLA_DV, outs[0], outs[1]).astype(o_ref.dtype)


def _mla_attention(q, k, v, tq):
    b, s, _ = q.shape
    pairs = MLA_HEADS // 2
    assert 2 * MLA_DV == LANES
    return pl.pallas_call(
        _mla_attn_kernel,
        grid=(b, pairs, s // tq),
        in_specs=[pl.BlockSpec((None, tq, 2 * LANES), lambda bb, p, i: (bb, i, p)),
                  pl.BlockSpec((None, s, 2 * LANES), lambda bb, p, i: (bb, 0, p)),
                  pl.BlockSpec((None, s, LANES), lambda bb, p, i: (bb, 0, p))],
        out_specs=pl.BlockSpec((None, tq, LANES), lambda bb, p, i: (bb, i, p)),
        out_shape=jax.ShapeDtypeStruct((b, s, MLA_HEADS * MLA_DV), BF16),
        compiler_params=_cparams(("parallel", "parallel", "parallel")),
        name="mla_attn",
    )(q, k, v)


def _even_in_weight(w):
    hk, hv = GLA_HEADS * GLA_DK, GLA_HEADS * GLA_DV
    widths = (hk, hk, hv, hv, 2 * GLA_RANK, DIFF_HEADS * 2 * DIFF_DK, DIFF_HEADS * 2 * DIFF_DK,
              DIFF_HEADS * DIFF_DV)
    offs = [0]
    for wd_ in widths:
        offs.append(offs[-1] + wd_)
    piece = lambda j: w[:, offs[j]:offs[j + 1]]
    pad = jnp.zeros((w.shape[0], EVEN_COLS - COL_ALR - 2 * GLA_RANK), w.dtype)
    return jnp.concatenate([piece(0), piece(1), piece(2), piece(3), piece(5), piece(6), piece(7),
                            piece(4), pad], axis=1).astype(BF16)


def _rot_half_cols(w):
    half = MLA_ROPE // 2
    shp = w.shape
    g = w.reshape(shp[0], -1, MLA_ROPE)
    return jnp.concatenate([-g[..., half:], g[..., :half]], axis=-1).reshape(shp)


def _mla_weights(w_in, w_uq, w_ukv):
    d = w_in.shape[0]
    dq = MLA_NOPE + MLA_ROPE
    z = lambda rows, cols: jnp.zeros((rows, cols), F32)
    w_kr = w_in[:, MLA_Q_LORA + MLA_KV_LORA:]
    kr_blk = lambda m: jnp.concatenate([z(d, MLA_NOPE), m, z(d, LANES - dq)], axis=1)
    win = jnp.concatenate([w_in[:, :MLA_Q_LORA + MLA_KV_LORA], kr_blk(w_kr), kr_blk(_rot_half_cols(w_kr))],
                          axis=1).astype(BF16)
    uq = w_uq.reshape(MLA_Q_LORA, MLA_HEADS, dq)
    pad_q = jnp.zeros((MLA_Q_LORA, MLA_HEADS, LANES - dq), F32)
    wq1 = jnp.concatenate([uq, pad_q], axis=-1).reshape(MLA_Q_LORA, -1).astype(BF16)
    rot = _rot_half_cols(uq[..., MLA_NOPE:].reshape(MLA_Q_LORA, -1)).reshape(MLA_Q_LORA, MLA_HEADS, MLA_ROPE)
    wq2 = jnp.concatenate([jnp.zeros((MLA_Q_LORA, MLA_HEADS, MLA_NOPE), F32), rot, pad_q],
                          axis=-1).reshape(MLA_Q_LORA, -1).astype(BF16)
    ukv = w_ukv.reshape(MLA_KV_LORA, MLA_HEADS, MLA_NOPE + MLA_DV)
    wk = jnp.concatenate([ukv[..., :MLA_NOPE], jnp.zeros((MLA_KV_LORA, MLA_HEADS, LANES - MLA_NOPE), F32)],
                         axis=-1).reshape(MLA_KV_LORA, -1).astype(BF16)
    wv = ukv[..., MLA_NOPE:].reshape(MLA_KV_LORA, -1).astype(BF16)
    return win, wq1, wq2, wk, wv


def _rope_tables(seq):
    half = MLA_ROPE // 2
    inv = ROPE_THETA ** (-jnp.arange(half, dtype=F32) / half)
    ang = jnp.arange(seq, dtype=F32)[:, None] * inv[None, :]
    cos = jnp.concatenate([jnp.cos(ang), jnp.cos(ang)], axis=1)
    sin = jnp.concatenate([jnp.sin(ang), jnp.sin(ang)], axis=1)
    dq = MLA_NOPE + MLA_ROPE
    scale = dq ** -0.5
    lay = lambda a, fill: jnp.concatenate([jnp.full((seq, MLA_NOPE), fill, F32), a,
                                           jnp.zeros((seq, LANES - dq), F32)], axis=1)
    return lay(cos, 1.0) * scale, lay(sin, 0.0) * scale, lay(cos, 0.0), lay(sin, 0.0)


def _routing_plan(info, counts, n_tokens):
    tm = MOE_TILE
    cnt = counts[0, :N_EXPERTS].astype(jnp.int32)
    padded = ((cnt + tm - 1) // tm) * tm
    ends = jnp.cumsum(padded)
    starts = ends - padded
    e = info[:, 0:TOP_K].astype(jnp.int32)
    rank = info[:, 4:4 + TOP_K].astype(jnp.int32)
    pos = starts[e] + rank
    n_tiles = (n_tokens * TOP_K) // tm + N_EXPERTS
    tile_start = jnp.arange(n_tiles, dtype=jnp.int32) * tm
    tile_expert = jnp.minimum(jnp.sum(tile_start[:, None] >= ends[None, :], axis=1), N_EXPERTS - 1)
    n_used = (ends[-1] // tm).reshape(1)
    pad_info = jnp.concatenate([starts + cnt, padded - cnt, n_used])
    return pos, tile_expert.astype(jnp.int32), n_used.astype(jnp.int32), pad_info.astype(jnp.int32), n_tiles * tm


def kernel(x, rel_bias_table, even_w_in, gla_gate_up, gla_gate_bias, gla_norm_gain, diff_lambda, diff_norm_gain, even_w_out, ffn_w_gate, ffn_w_up, ffn_w_down, odd_w_in, mla_q_norm_gain, mla_kv_norm_gain, mla_w_uq, mla_w_ukv, odd_w_out, router_w, moe_w_gate, moe_w_up, moe_w_down, ln_gain, ln_bias):
    b, s, d = x.shape
    n = b * s
    tm = min(512, s)
    tq = min(256, s)
    x2 = x.reshape(n, d)
    row = lambda v: v.reshape(1, -1)

    proj = _matmul(x2, _even_in_weight(even_w_in[0]), tm, BF16).reshape(b, s, EVEN_COLS)
    gup = jnp.zeros((2, LANES, GLA_HEADS * GLA_DK), F32)
    for dd in range(2):
        gup = gup.at[dd, dd * GLA_RANK:(dd + 1) * GLA_RANK].set(gla_gate_up[0, dd])
    o_gla = _gla(proj, gup.astype(BF16), gla_gate_bias[0][:, None, :], row(gla_norm_gain[0]))
    lam_init = 0.8 - 0.6 * math.exp(-0.3 * 0)
    lf = diff_lambda[0]
    lam = (jnp.exp(jnp.sum(lf[0] * lf[1])) - jnp.exp(jnp.sum(lf[2] * lf[3])) + lam_init).reshape(1)
    bias = _rel_bias(rel_bias_table, s, tq)
    o_diff = _diff_attention(proj, bias, lam, row(diff_norm_gain[0]), lam_init, tq)
    hv = GLA_HEADS * GLA_DV
    w_out = even_w_out[0].astype(BF16)
    x2 = _outproj_ln([o_gla.reshape(n, hv), o_diff.reshape(n, -1)], [w_out[:hv], w_out[hv:]], x2,
                     row(ln_gain[0, 0]), row(ln_bias[0, 0]), tm)
    x2 = _ffn_ln(x2, ffn_w_gate[0].astype(BF16), ffn_w_up[0].astype(BF16), ffn_w_down[0].astype(BF16),
                 row(ln_gain[0, 1]), row(ln_bias[0, 1]), tm)

    win, wq1, wq2, wk, wv = _mla_weights(odd_w_in[0], mla_w_uq[0], mla_w_ukv[0])
    q, k, v = _mla_proj(x2, win, row(mla_q_norm_gain[0]), row(mla_kv_norm_gain[0]), wq1, wq2, wk, wv,
                        _rope_tables(s), s, tm)
    o = _mla_attention(q.reshape(b, s, -1), k.reshape(b, s, -1), v.reshape(b, s, -1), tq)
    rw = jnp.concatenate([router_w[0], jnp.zeros((d, LANES - N_EXPERTS), F32)], axis=1)
    xn, info, counts = _outproj_route(o.reshape(n, -1), odd_w_out[0].astype(BF16), x2,
                                      row(ln_gain[1, 0]), row(ln_bias[1, 0]), rw, tm)
    pos, tile_expert, n_used, pad_info, n_slots = _routing_plan(info, counts, n)
    pos_t = pos.reshape(n // tm, tm, TOP_K).transpose(0, 2, 1)
    x_sorted = _scatter_rows(xn, pos_t, pad_info, n_slots, tm)
    y_sorted = _moe_ffn(x_sorted, tile_expert, n_used, moe_w_gate[0].astype(BF16),
                        moe_w_up[0].astype(BF16), moe_w_down[0].astype(BF16))
    out = _combine_ln(xn, info, pos_t, y_sorted, row(ln_gain[1, 1]), row(ln_bias[1, 1]), tm)
    return out.reshape(b, s, d)
```

```python
import functools
import math

import jax
import jax.numpy as jnp
from jax import lax
from jax.experimental import pallas as pl
from jax.experimental.pallas import tpu as pltpu

F32 = jnp.float32
BF16 = jnp.bfloat16

LANES = 128
V7X_VMEM_BYTES = 64 * 1024 * 1024
VMEM_LIMIT = V7X_VMEM_BYTES - 8 * 1024 * 1024

D_MODEL = 1024
GLA_HEADS, GLA_DK, GLA_DV = 4, 64, 128
GLA_RANK, GLA_TEMP, GLA_CHUNK = 16, 16.0, 64
DIFF_HEADS, DIFF_DK, DIFF_DV = 4, 64, 128
N_BUCKETS, MAX_DISTANCE = 32, 128
MLA_HEADS, MLA_Q_LORA, MLA_KV_LORA = 16, 256, 128
MLA_NOPE, MLA_ROPE, MLA_DV = 64, 32, 64
ROPE_THETA = 10000.0
D_FF, N_EXPERTS, TOP_K = 2816, 8, 2
DEPTH = 2
ALPHA = (2 * DEPTH) ** 0.25
LN_EPS, RMS_EPS = 1e-5, 1e-6

COL_QK_G, COL_V_G, COL_G_G = 0, 512, 1024
COL_Q_D, COL_K_D, COL_ALR = 1536, 2048, 2560
EVEN_COLS = 2688

ATT_KC = 128
ATT_TQ = 256
MLA_TQ = 512
LOG2E = math.log2(math.e)
NEG_BIG = -1e30

GLA_GROUP = 256
FF_CHUNK = 256
MOE_TILE = 512
DMA_UNROLL = 8


def _cparams(sem):
    return pltpu.CompilerParams(dimension_semantics=sem, vmem_limit_bytes=VMEM_LIMIT)


def _layer_norm(y, g, b):
    mu = jnp.mean(y, axis=-1, keepdims=True)
    d = y - mu
    var = jnp.mean(d * d, axis=-1, keepdims=True)
    return d * lax.rsqrt(var + LN_EPS) * g + b


def _split3(x):
    h1 = x.astype(BF16)
    r1 = x - h1.astype(F32)
    h2 = r1.astype(BF16)
    h3 = (r1 - h2.astype(F32)).astype(BF16)
    return h1, h2, h3


def _dot(a, b):
    return jnp.dot(a, b, preferred_element_type=F32)


def _dot_nt(a, b):
    return lax.dot_general(a, b, (((1,), (1,)), ((), ())), preferred_element_type=F32)


def _dot_tn(a, b):
    return lax.dot_general(a, b, (((0,), (0,)), ((), ())), preferred_element_type=F32)


def _store_chunks(ref, val_t, width):
    for j in range(val_t.shape[1] // width):
        ref[j] = val_t[:, j * width:(j + 1) * width].astype(ref.dtype)


def _in_proj_kernel(x_ref, w_ref, wvt_ref, o_ref, vt_ref):
    xb = x_ref[...].astype(BF16)
    o_ref[...] = _dot(xb, w_ref[...]).astype(o_ref.dtype)
    _store_chunks(vt_ref, _dot_nt(wvt_ref[...], xb), ATT_KC)


def _in_proj(x, w, wvt, seq, tm):
    n, k = x.shape
    m = w.shape[1]
    dvt = wvt.shape[0]
    per_seq = seq // tm
    return pl.pallas_call(
        _in_proj_kernel,
        grid=(n // tm,),
        in_specs=[pl.BlockSpec((tm, k), lambda i: (i, 0)),
                  pl.BlockSpec((k, m), lambda i: (0, 0)),
                  pl.BlockSpec((dvt, k), lambda i: (0, 0))],
        out_specs=[pl.BlockSpec((tm, m), lambda i: (i, 0)),
                   pl.BlockSpec((None, tm // ATT_KC, dvt, ATT_KC),
                                lambda i: (i // per_seq, i % per_seq, 0, 0))],
        out_shape=[jax.ShapeDtypeStruct((n, m), BF16),
                   jax.ShapeDtypeStruct((n // seq, seq // ATT_KC, dvt, ATT_KC), BF16)],
        compiler_params=_cparams(("parallel",)),
        name="in_proj",
    )(x, w, wvt)


def _gla_kernel(qk_ref, v_ref, g_ref, alr_ref, gup_ref, gbias_ref, gain_ref, o_ref,
                la_ref, qdec_ref, kdec_ref, decay_ref, acc_ref, st_ref):
    seq = qk_ref.shape[0]
    hk = GLA_HEADS * GLA_DK
    hv = GLA_HEADS * GLA_DV
    grp = GLA_GROUP
    c = GLA_CHUNK

    alr = alr_ref[...]
    for d in range(2):
        logits = _dot(alr, gup_ref[d]) + gbias_ref[d]
        la_ref[d] = jax.nn.log_sigmoid(logits) / GLA_TEMP

    row = lax.broadcasted_iota(jnp.int32, (grp, grp), 0)
    col = lax.broadcasted_iota(jnp.int32, (grp, grp), 1)
    same = (row // c) == (col // c)
    tri = [same & (row >= col), same & (row <= col)]
    cum_inc = [jnp.where(t, 1.0, 0.0).astype(BF16) for t in tri]
    cum_rem = [jnp.where(same & (row < col), 1.0, 0.0).astype(BF16),
               jnp.where(same & (row > col), 1.0, 0.0).astype(BF16)]
    lane_k = lax.broadcasted_iota(jnp.int32, (grp, hk), 1) // GLA_DK
    scale = GLA_DK ** -0.5

    def group_body(r, carry):
        rows = pl.ds(pl.multiple_of(r * grp, grp), grp)
        q = qk_ref[rows, 0:hk].astype(F32)
        k = qk_ref[rows, hk:2 * hk].astype(F32)
        v = v_ref[rows, :]
        o_heads = [jnp.zeros((grp, GLA_DV), F32) for _ in range(GLA_HEADS)]
        for d in range(2):
            parts = _split3(la_ref[d, rows, :])
            bcum = sum(_dot(cum_inc[d], p) for p in parts)
            brem = sum(_dot(cum_rem[d], p) for p in parts)
            q_dec = q * jnp.exp(bcum) * scale
            k_inv = (k * jnp.exp(-bcum)).astype(BF16)
            qdec_ref[d, rows, :] = q_dec.astype(BF16)
            kdec_ref[d, rows, :] = (k * jnp.exp(brem)).astype(BF16)
            decay_ref[d, rows, :] = jnp.exp(bcum + brem)
            for h in range(GLA_HEADS):
                q_h = jnp.where(lane_k == h, q_dec, 0.0).astype(BF16)
                attn = jnp.where(tri[d], _dot_nt(q_h, k_inv), 0.0)
                o_heads[h] = o_heads[h] + _dot(attn.astype(BF16), v[:, h * GLA_DV:(h + 1) * GLA_DV])
        for h in range(GLA_HEADS):
            acc_ref[rows, h * GLA_DV:(h + 1) * GLA_DV] = o_heads[h]
        return carry

    lax.fori_loop(0, seq // grp, group_body, 0)

    srow = lax.broadcasted_iota(jnp.int32, (hv, hk), 0) // GLA_DV
    scol = lax.broadcasted_iota(jnp.int32, (hv, hk), 1) // GLA_DK
    head_mask = srow == scol
    n_chunks = seq // c
    st_ref[...] = jnp.zeros_like(st_ref)

    def chunk_body(i, carry):
        for d in range(2):
            ci = i if d == 0 else n_chunks - 1 - i
            rows = pl.ds(pl.multiple_of(ci * c, c), c)
            state = st_ref[d]
            acc_ref[rows, :] += _dot_nt(qdec_ref[d, rows, :], state.astype(BF16))
            kv = _dot_tn(v_ref[rows, :], kdec_ref[d, rows, :])
            decay = decay_ref[d, pl.ds(pl.multiple_of(ci * c, c), 1), :]
            st_ref[d] = state * decay + jnp.where(head_mask, kv, 0.0)
        return carry

    lax.fori_loop(0, n_chunks, chunk_body, 0)

    gain = gain_ref[...]
    for h in range(GLA_HEADS):
        sl = slice(h * GLA_DV, (h + 1) * GLA_DV)
        o = acc_ref[:, sl]
        y = o * lax.rsqrt(jnp.mean(o * o, axis=-1, keepdims=True) + RMS_EPS) * gain
        gate = g_ref[:, sl].astype(F32)
        o_ref[:, sl] = (y * (gate * jax.nn.sigmoid(gate))).astype(o_ref.dtype)


def _gla(proj, gup, gbias, gain):
    b, s, _ = proj.shape
    hk, hv = GLA_HEADS * GLA_DK, GLA_HEADS * GLA_DV
    blk = lambda width, col: pl.BlockSpec((None, s, width), lambda i: (i, 0, col // width))
    return pl.pallas_call(
        _gla_kernel,
        grid=(b,),
        in_specs=[blk(2 * hk, COL_QK_G), blk(hv, COL_V_G), blk(hv, COL_G_G), blk(LANES, COL_ALR),
                  pl.BlockSpec((2, LANES, hk), lambda i: (0, 0, 0)),
                  pl.BlockSpec((2, 1, hk), lambda i: (0, 0, 0)),
                  pl.BlockSpec((1, GLA_DV), lambda i: (0, 0))],
        out_specs=pl.BlockSpec((None, s, hv), lambda i: (i, 0, 0)),
        out_shape=jax.ShapeDtypeStruct((b, s, hv), BF16),
        scratch_shapes=[pltpu.VMEM((2, s, hk), F32),
                        pltpu.VMEM((2, s, hk), BF16),
                        pltpu.VMEM((2, s, hk), BF16),
                        pltpu.VMEM((2, s, hk), F32),
                        pltpu.VMEM((s, hv), F32),
                        pltpu.VMEM((2, hv, hk), F32)],
        compiler_params=_cparams(("parallel",)),
        name="gla",
    )(proj, proj, proj, proj, gup, gbias, gain)


J_BELOW = (-MAX_DISTANCE - ATT_KC + 1) // ATT_KC
J_ABOVE = -(-(MAX_DISTANCE + ATT_TQ - 1) // ATT_KC)
N_BAND = J_ABOVE - J_BELOW + 1
assert ATT_KC * J_BELOW + ATT_KC - 1 <= -MAX_DISTANCE
assert ATT_KC * J_ABOVE - (ATT_TQ - 1) >= MAX_DISTANCE


def _band_kernel(tab_ref, o_ref, *, n_heads):
    h = pl.program_id(0)
    half = N_BUCKETS // 2
    max_exact = half // 2
    shape = (ATT_KC, ATT_TQ)
    for t in range(N_BAND):
        j = t + J_BELOW
        rel =(j * ATT_KC + lax.broadcasted_iota(jnp.int32, shape, 0)
               - lax.broadcasted_iota(jnp.int32, shape, 1))
        bucket = jnp.where(rel > 0, half, 0).astype(jnp.int32)
        n = jnp.abs(rel)
        n_large = max_exact + (jnp.log(jnp.maximum(n, max_exact).astype(F32) / max_exact)
                               / math.log(MAX_DISTANCE / max_exact) * (half - max_exact)).astype(jnp.int32)
        n_large = jnp.minimum(n_large, half - 1)
        bucket = bucket + jnp.where(n < max_exact, n, n_large)
        val = jnp.zeros(shape, F32)
        for bkt in range(N_BUCKETS):
            val = jnp.where(bucket == bkt, tab_ref[bkt * n_heads + h], val)
        o_ref[t] = val * LOG2E


def _bias_band(table):
    n_heads = table.shape[1]
    return pl.pallas_call(
        functools.partial(_band_kernel, n_heads=n_heads),
        grid=(n_heads,),
        in_specs=[pl.BlockSpec(memory_space=pltpu.SMEM)],
        out_specs=pl.BlockSpec((None, N_BAND, ATT_KC, ATT_TQ), lambda h: (h, 0, 0, 0)),
        out_shape=jax.ShapeDtypeStruct((n_heads, N_BAND, ATT_KC, ATT_TQ), F32),
        compiler_params=_cparams(("parallel",)),
        name="bias_band",
    )(table.reshape(-1))


def _flash_t(q_ts, k_ref, k_cols, vt_ref, vt_rows, band, n_chunks):
    n_streams = len(q_ts)
    tq = q_ts[0].shape[0]
    dv = vt_rows[0].stop - vt_rows[0].start
    m = [jnp.full((1, tq), NEG_BIG, F32)] * n_streams
    l = [jnp.zeros((1, tq), F32)] * n_streams
    acc = [jnp.zeros((dv, tq), F32)] * n_streams
    scores = lambda i, c: _dot_nt(k_ref[c * ATT_KC:(c + 1) * ATT_KC, k_cols[i]], q_ts[i])
    s_next = [scores(i, 0) for i in range(n_streams)]
    for c in range(n_chunks):
        s_cur = s_next
        if c + 1 < n_chunks:
            s_next = [scores(i, c + 1) for i in range(n_streams)]
        bias = band(c) if band is not None else None
        for i in range(n_streams):
            s = s_cur[i] if bias is None else s_cur[i] + bias
            m_new = jnp.maximum(m[i], jnp.max(s, axis=0, keepdims=True))
            p = jnp.exp2(s - m_new)
            alpha = jnp.exp2(m[i] - m_new)
            l[i] = alpha * l[i] + jnp.sum(p, axis=0, keepdims=True)
            acc[i] = alpha * acc[i] + _dot(vt_ref[c, vt_rows[i], :], p.astype(BF16))
            m[i] = m_new
    return [a * (1.0 / li) for a, li in zip(acc, l)]


def _diff_kernel(lam_ref, q_ref, k_ref, vt_ref, band_ref, gain_ref, o_ref, *, out_scale):
    seq = q_ref.shape[0]
    n_chunks = seq // ATT_KC
    lane = lax.broadcasted_iota(jnp.int32, (ATT_TQ, 2 * DIFF_DK), 1)
    scale = DIFF_DK ** -0.5 * LOG2E
    all_rows = slice(0, DIFF_DV)

    def tile_body(qi, carry):
        rows = pl.ds(pl.multiple_of(qi * ATT_TQ, ATT_TQ), ATT_TQ)
        q = (q_ref[rows, :].astype(F32) * scale).astype(BF16)
        zero = jnp.zeros_like(q)
        band = lambda c: band_ref[jnp.clip(c - (ATT_TQ // ATT_KC) * qi - J_BELOW, 0, N_BAND - 1)]
        maps = [jnp.where(lane < DIFF_DK, q, zero), jnp.where(lane >= DIFF_DK, q, zero)]
        o0, o1 = _flash_t(maps, k_ref, [slice(None)] * 2, vt_ref, [all_rows] * 2, band, n_chunks)
        o = o0 - lam_ref[0] * o1
        y = o * lax.rsqrt(jnp.mean(o * o, axis=0, keepdims=True) + RMS_EPS) * gain_ref[...]
        o_ref[qi] = (y * out_scale).astype(o_ref.dtype)
        return carry

    lax.fori_loop(0, seq // ATT_TQ, tile_body, 0)


def _diff_attention(proj, vt, band, lam, gain_col, lam_init):
    b, s, _ = proj.shape
    h = DIFF_HEADS
    w = 2 * DIFF_DK
    assert w == LANES
    return pl.pallas_call(
        functools.partial(_diff_kernel, out_scale=1.0 - lam_init),
        grid=(h, b),
        in_specs=[pl.BlockSpec(memory_space=pltpu.SMEM),
                  pl.BlockSpec((None, s, w), lambda hh, bb: (bb, 0, COL_Q_D // w + hh)),
                  pl.BlockSpec((None, s, w), lambda hh, bb: (bb, 0, COL_K_D // w + hh)),
                  pl.BlockSpec((None, s // ATT_KC, DIFF_DV, ATT_KC), lambda hh, bb: (bb, 0, hh, 0)),
                  pl.BlockSpec((None, N_BAND, ATT_KC, ATT_TQ), lambda hh, bb: (hh, 0, 0, 0)),
                  pl.BlockSpec((DIFF_DV, 1), lambda hh, bb: (0, 0))],
        out_specs=pl.BlockSpec((None, s // ATT_TQ, DIFF_DV, ATT_TQ), lambda hh, bb: (bb, 0, hh, 0)),
        out_shape=jax.ShapeDtypeStruct((b, s // ATT_TQ, h * DIFF_DV, ATT_TQ), BF16),
        compiler_params=_cparams(("parallel", "parallel")),
        name="diff_attn",
    )(lam, proj, proj, vt, band, gain_col)


def _residual_ln(x_ref, o_ref, g_ref, b_ref, row_pairs, col_pairs):
    for j in range(x_ref.shape[0] // ATT_TQ):
        rows = slice(j * ATT_TQ, (j + 1) * ATT_TQ)
        h = ALPHA * x_ref[rows, :]
        for a_ref, w_ref in row_pairs:
            h = h + _dot(a_ref[rows, :], w_ref[...])
        for a_ref, w_ref in col_pairs:
            h = h + _dot_tn(a_ref[j], w_ref[...])
        o_ref[rows, :] = _layer_norm(h, g_ref[...], b_ref[...])


def _outproj_kernel(a_ref, at_ref, wa_ref, wt_ref, x_ref, g_ref, b_ref, o_ref):
    _residual_ln(x_ref, o_ref, g_ref, b_ref, [(a_ref, wa_ref)], [(at_ref, wt_ref)])


def _tile_specs(n, d, seq, tm):
    per_seq = seq // tm
    rows = lambda width: pl.BlockSpec((tm, width), lambda i: (i, 0))
    cols = lambda k: pl.BlockSpec((None, tm // ATT_TQ, k, ATT_TQ), lambda i: (i // per_seq, i % per_seq, 0, 0))
    full = lambda a: pl.BlockSpec(a.shape, lambda i: (0,) * a.ndim)
    return rows, cols, full


def _outproj_ln(a, at, wa, wt, x, g, b, seq, tm):
    n, d = x.shape
    rows, cols, full = _tile_specs(n, d, seq, tm)
    return pl.pallas_call(
        _outproj_kernel,
        grid=(n // tm,),
        in_specs=[rows(a.shape[1]), cols(at.shape[2]), full(wa), full(wt), rows(d), full(g), full(b)],
        out_specs=rows(d),
        out_shape=jax.ShapeDtypeStruct((n, d), F32),
        compiler_params=_cparams(("parallel",)),
        name="outproj_ln",
    )(a, at, wa, wt, x, g, b)


def _outproj_route_kernel(a_ref, w_ref, x_ref, g_ref, b_ref, rw_ref, o_ref, info_ref, cnt_ref, carry_ref):
    i = pl.program_id(0)
    tm = x_ref.shape[0]

    @pl.when(i == 0)
    def _():
        carry_ref[...] = jnp.zeros_like(carry_ref)

    _residual_ln(x_ref, o_ref, g_ref, b_ref, [(a_ref, w_ref)], [])
    xn = o_ref[...]

    x_hi = xn.astype(BF16)
    x_lo = (xn - x_hi.astype(F32)).astype(BF16)
    logits = _dot(x_hi, rw_ref[0]) + (_dot(x_hi, rw_ref[1]) + _dot(x_lo, rw_ref[0]))
    lane = lax.broadcasted_iota(jnp.int32, logits.shape, 1).astype(F32)
    neg = jnp.float32(-jnp.inf)
    logits = jnp.where(lane < N_EXPERTS, logits, neg)
    v1 = jnp.max(logits, axis=-1, keepdims=True)
    e1 = jnp.min(jnp.where(logits == v1, lane, float(LANES)), axis=-1, keepdims=True)
    rest = jnp.where(lane == e1, neg, logits)
    v2 = jnp.max(rest, axis=-1, keepdims=True)
    e2 = jnp.min(jnp.where(rest == v2, lane, float(LANES)), axis=-1, keepdims=True)
    t = jnp.exp(v2 - v1)
    w1 = 1.0 / (1.0 + t)
    w2 = t / (1.0 + t)

    onehot = jnp.where((lane == e1) | (lane == e2), 1.0, 0.0)
    row = lax.broadcasted_iota(jnp.int32, (tm, tm), 0)
    col = lax.broadcasted_iota(jnp.int32, (tm, tm), 1)
    before = jnp.where(row > col, 1.0, 0.0).astype(BF16)
    prior = carry_ref[...] + _dot(before, onehot.astype(BF16))
    r1 = jnp.sum(jnp.where(lane == e1, prior, 0.0), axis=-1, keepdims=True)
    r2 = jnp.sum(jnp.where(lane == e2, prior, 0.0), axis=-1, keepdims=True)
    carry_ref[...] += jnp.sum(onehot, axis=0, keepdims=True)
    cnt_ref[...] = jnp.broadcast_to(carry_ref[...], cnt_ref.shape)

    info = jnp.zeros(logits.shape, F32)
    for idx, val in enumerate((e1, e2, w1, w2, r1, r2)):
        info = jnp.where(lane == idx, val, info)
    info_ref[...] = info


def _outproj_route(a, w, x, g, b, router_w2, seq, tm):
    n, d = x.shape
    rows, cols, full = _tile_specs(n, d, seq, tm)
    return pl.pallas_call(
        _outproj_route_kernel,
        grid=(n // tm,),
        in_specs=[rows(a.shape[1]), full(w), rows(d), full(g), full(b), full(router_w2)],
        out_specs=[rows(d), rows(LANES),
                   pl.BlockSpec((8, LANES), lambda i: (0, 0))],
        out_shape=[jax.ShapeDtypeStruct((n, d), F32),
                   jax.ShapeDtypeStruct((n, LANES), F32),
                   jax.ShapeDtypeStruct((8, LANES), F32)],
        scratch_shapes=[pltpu.VMEM((1, LANES), F32)],
        compiler_params=_cparams(("arbitrary",)),
        name="outproj_route",
    )(a, w, x, g, b, router_w2)


def _swiglu_acc(xb, wg_ref, wu_ref, wd_ref, acc_ref):
    n_chunks = wg_ref.shape[-1] // FF_CHUNK
    for c in range(n_chunks):
        cols = slice(c * FF_CHUNK, (c + 1) * FF_CHUNK)
        gate = _dot(xb, wg_ref[:, cols])
        up = _dot(xb, wu_ref[:, cols])
        hidden = (gate * jax.nn.sigmoid(gate) * up).astype(BF16)
        part = _dot(hidden, wd_ref[cols, :])
        if c == 0:
            acc_ref[...] = part
        else:
            acc_ref[...] += part


def _ffn_ln_kernel(x_ref, wg_ref, wu_ref, wd_ref, g_ref, b_ref, o_ref):
    x = x_ref[...]
    _swiglu_acc(x.astype(BF16), wg_ref, wu_ref, wd_ref, o_ref)
    o_ref[...] = _layer_norm(ALPHA * x + o_ref[...], g_ref[...], b_ref[...])


def _ffn_ln(x, wg, wu, wd, g, b, tm):
    n, d = x.shape
    f = wg.shape[1]
    once = lambda shape: pl.BlockSpec(shape, lambda i: (0,) * len(shape), pipeline_mode=pl.Buffered(1))
    return pl.pallas_call(
        _ffn_ln_kernel,
        grid=(n // tm,),
        in_specs=[pl.BlockSpec((tm, d), lambda i: (i, 0)),
                  once((d, f)), once((d, f)), once((f, d)), once((1, d)), once((1, d))],
        out_specs=pl.BlockSpec((tm, d), lambda i: (i, 0)),
        out_shape=jax.ShapeDtypeStruct((n, d), F32),
        compiler_params=_cparams(("parallel",)),
        name="ffn_ln",
    )(x, wg, wu, wd, g, b)


def _moe_kernel(te_ref, nu_ref, x_ref, wg_ref, wu_ref, wd_ref, o_ref):
    t = pl.program_id(0)

    @pl.when(t < nu_ref[0])
    def _():
        _swiglu_acc(x_ref[...].astype(BF16), wg_ref, wu_ref, wd_ref, o_ref)

    @pl.when(t >= nu_ref[0])
    def _():
        o_ref[...] = jnp.zeros_like(o_ref)


def _moe_ffn(x_sorted, tile_expert, n_used, wg, wu, wd):
    p, d = x_sorted.shape
    f = wg.shape[2]
    tm = MOE_TILE
    grid_spec = pltpu.PrefetchScalarGridSpec(
        num_scalar_prefetch=2,
        grid=(p // tm,),
        in_specs=[pl.BlockSpec((tm, d), lambda t, te, nu: (jnp.minimum(t, nu[0] - 1), 0)),
                  pl.BlockSpec((None, d, f), lambda t, te, nu: (te[t], 0, 0)),
                  pl.BlockSpec((None, d, f), lambda t, te, nu: (te[t], 0, 0)),
                  pl.BlockSpec((None, f, d), lambda t, te, nu: (te[t], 0, 0))],
        out_specs=pl.BlockSpec((tm, d), lambda t, te, nu: (t, 0)),
    )
    return pl.pallas_call(
        _moe_kernel,
        grid_spec=grid_spec,
        out_shape=jax.ShapeDtypeStruct((p, d), F32),
        compiler_params=_cparams(("arbitrary",)),
        name="moe_ffn",
    )(tile_expert, n_used, x_sorted, wg, wu, wd)


def _scatter_kernel(pos_ref, pad_ref, x_ref, o_hbm, zero_ref, sem, zsem):
    i = pl.program_id(0)
    tm = x_ref.shape[0]
    n_tiles = o_hbm.shape[0] // MOE_TILE

    @pl.when(i == 0)
    def _():
        zero_ref[...] = jnp.zeros_like(zero_ref)
        zero_row = zero_ref.at[pl.ds(0, 1)]
        for e in range(N_EXPERTS):
            start, count = pad_ref[e], pad_ref[N_EXPERTS + e]

            def fill(r, carry):
                pltpu.make_async_copy(zero_row, o_hbm.at[pl.ds(start + r, 1)], zsem).start()
                return carry

            lax.fori_loop(0, count, fill, 0)

            def drain(r, carry):
                pltpu.make_async_copy(zero_row, o_hbm.at[pl.ds(start + r, 1)], zsem).wait()
                return carry

            lax.fori_loop(0, count, drain, 0)

        def fill_tile(t, carry):
            dst = o_hbm.at[pl.ds(pl.multiple_of(t * MOE_TILE, MOE_TILE), MOE_TILE)]
            cp = pltpu.make_async_copy(zero_ref, dst, zsem)
            cp.start()
            cp.wait()
            return carry

        lax.fori_loop(pad_ref[2 * N_EXPERTS], n_tiles, fill_tile, 0)

    def issue(r, carry):
        for kk in range(TOP_K):
            dst = pos_ref[0, kk, r]
            pltpu.make_async_copy(x_ref.at[pl.ds(r, 1)], o_hbm.at[pl.ds(dst, 1)], sem).start()
        return carry

    lax.fori_loop(0, tm, issue, 0, unroll=DMA_UNROLL)

    for kk in range(TOP_K):
        pltpu.make_async_copy(x_ref, o_hbm.at[pl.ds(0, tm)], sem).wait()


def _scatter_rows(x, pos, pad_info, n_slots, tm):
    n, d = x.shape
    return pl.pallas_call(
        _scatter_kernel,
        grid=(n // tm,),
        in_specs=[pl.BlockSpec((1, TOP_K, tm), lambda i: (i, 0, 0), memory_space=pltpu.SMEM),
                  pl.BlockSpec(memory_space=pltpu.SMEM),
                  pl.BlockSpec((tm, d), lambda i: (i, 0))],
        out_specs=pl.BlockSpec(memory_space=pl.ANY),
        out_shape=jax.ShapeDtypeStruct((n_slots, d), F32),
        scratch_shapes=[pltpu.VMEM((MOE_TILE, d), F32),
                        pltpu.SemaphoreType.DMA(()),
                        pltpu.SemaphoreType.DMA(())],
        compiler_params=_cparams(("arbitrary",)),
        name="scatter_rows",
    )(pos, pad_info, x)


def _combine_kernel(pos_ref, x_ref, info_ref, g_ref, b_ref, y_hbm, o_ref, buf_ref, sem):
    tm = x_ref.shape[0]

    def issue(r, carry):
        for kk in range(TOP_K):
            src = pos_ref[0, kk, r]
            pltpu.make_async_copy(y_hbm.at[pl.ds(src, 1)], buf_ref.at[kk, pl.ds(r, 1)], sem).start()
        return carry

    lax.fori_loop(0, tm, issue, 0, unroll=DMA_UNROLL)

    for kk in range(TOP_K):
        pltpu.make_async_copy(y_hbm.at[pl.ds(0, tm)], buf_ref.at[kk], sem).wait()

    info = info_ref[...]
    w1 = info[:, 2:3]
    w2 = info[:, 3:4]
    f = w1 * buf_ref[0] + w2 * buf_ref[1]
    o_ref[...] = _layer_norm(ALPHA * x_ref[...] + f, g_ref[...], b_ref[...])


def _combine_ln(x, info, pos, y_sorted, g, b, tm):
    n, d = x.shape
    return pl.pallas_call(
        _combine_kernel,
        grid=(n // tm,),
        in_specs=[pl.BlockSpec((1, TOP_K, tm), lambda i: (i, 0, 0), memory_space=pltpu.SMEM),
                  pl.BlockSpec((tm, d), lambda i: (i, 0)),
                  pl.BlockSpec((tm, LANES), lambda i: (i, 0)),
                  pl.BlockSpec((1, d), lambda i: (0, 0)),
                  pl.BlockSpec((1, d), lambda i: (0, 0)),
                  pl.BlockSpec(memory_space=pl.ANY)],
        out_specs=pl.BlockSpec((tm, d), lambda i: (i, 0)),
        out_shape=jax.ShapeDtypeStruct((n, d), F32),
        scratch_shapes=[pltpu.VMEM((TOP_K, tm, d), F32),
                        pltpu.SemaphoreType.DMA(())],
        compiler_params=_cparams(("arbitrary",)),
        name="combine_ln",
    )(pos, x, info, g, b, y_sorted)


def _mla_proj_kernel(x_ref, win_ref, qg_ref, kvg_ref, wq1_ref, wq2_ref, wk_ref, wv_ref,
                     cq_ref, sq_ref, ck_ref, sk_ref, q_ref, k_ref, v_ref):
    c = _dot(x_ref[...].astype(BF16), win_ref[...])
    cq = c[:, :MLA_Q_LORA]
    ckv = c[:, MLA_Q_LORA:MLA_Q_LORA + MLA_KV_LORA]
    off = MLA_Q_LORA + MLA_KV_LORA
    k_rope = c[:, off:off + LANES] * ck_ref[...] + c[:, off + LANES:off + 2 * LANES] * sk_ref[...]
    cq = (cq * lax.rsqrt(jnp.mean(cq * cq, axis=-1, keepdims=True) + RMS_EPS) * qg_ref[...]).astype(BF16)
    ckv = (ckv * lax.rsqrt(jnp.mean(ckv * ckv, axis=-1, keepdims=True) + RMS_EPS) * kvg_ref[...]).astype(BF16)
    q1 = _dot(cq, wq1_ref[...])
    q2 = _dot(cq, wq2_ref[...])
    k1 = _dot(ckv, wk_ref[...])
    v_ref[...] = _dot(ckv, wv_ref[...]).astype(v_ref.dtype)
    cos_q, sin_q = cq_ref[...], sq_ref[...]
    for h in range(MLA_HEADS):
        sl = slice(h * LANES, (h + 1) * LANES)
        q_ref[:, sl] = (q1[:, sl] * cos_q + q2[:, sl] * sin_q).astype(q_ref.dtype)
        k_ref[:, sl] = (k1[:, sl] + k_rope).astype(k_ref.dtype)


def _mla_proj(x, win, qg, kvg, wq1, wq2, wk, wv, tabs, seq, tm):
    n, d = x.shape
    per_seq = seq // tm
    full = lambda a: pl.BlockSpec(a.shape, lambda i: (0,) * a.ndim)
    tab = pl.BlockSpec((tm, LANES), lambda i: (i % per_seq, 0))
    hw = MLA_HEADS * LANES
    return pl.pallas_call(
        _mla_proj_kernel,
        grid=(n // tm,),
        in_specs=[pl.BlockSpec((tm, d), lambda i: (i, 0)), full(win), full(qg), full(kvg),
                  full(wq1), full(wq2), full(wk), full(wv), tab, tab, tab, tab],
        out_specs=[pl.BlockSpec((tm, hw), lambda i: (i, 0)),
                   pl.BlockSpec((tm, hw), lambda i: (i, 0)),
                   pl.BlockSpec((tm, MLA_HEADS * MLA_DV), lambda i: (i, 0))],
        out_shape=[jax.ShapeDtypeStruct((n, hw), BF16),
                   jax.ShapeDtypeStruct((n, hw), BF16),
                   jax.ShapeDtypeStruct((n, MLA_HEADS * MLA_DV), BF16)],
        compiler_params=_cparams(("parallel",)),
        name="mla_proj",
    )(x, win, qg, kvg, wq1, wq2, wk, wv, *tabs)


def _mla_attn_kernel(q_ref, k_ref, v_ref, o_ref):
    v = v_ref[...]
    outs = []
    for hh in range(2):
        sl = slice(hh * LANES, (hh + 1) * LANES)
        s = _dot_nt(q_ref[:, sl], k_ref[:, sl])
        p = jnp.exp2(s - jnp.max(s, axis=-1, keepdims=True))
        l = jnp.sum(p, axis=-1, keepdims=True)
        outs.append(_dot(p.astype(BF16), v) * (1.0 / l))
    lane = lax.broadcasted_iota(jnp.int32, outs[0].shape, 1)
    o_ref[...] = jnp.where(lane < MLA_DV, outs[0], outs[1]).astype(o_ref.dtype)


def _mla_attention(q, k, v, tq):
    b, s, _ = q.shape
    pairs = MLA_HEADS // 2
    assert 2 * MLA_DV == LANES
    return pl.pallas_call(
        _mla_attn_kernel,
        grid=(b, pairs, s // tq),
        in_specs=[pl.BlockSpec((None, tq, 2 * LANES), lambda bb, p, i: (bb, i, p)),
                  pl.BlockSpec((None, s, 2 * LANES), lambda bb, p, i: (bb, 0, p)),
                  pl.BlockSpec((None, s, LANES), lambda bb, p, i: (bb, 0, p))],
        out_specs=pl.BlockSpec((None, tq, LANES), lambda bb, p, i: (bb, i, p)),
        out_shape=jax.ShapeDtypeStruct((b, s, MLA_HEADS * MLA_DV), BF16),
        compiler_params=_cparams(("parallel", "parallel", "parallel")),
        name="mla_attn",
    )(q, k, v)


def _even_in_weight(w):
    hk, hv = GLA_HEADS * GLA_DK, GLA_HEADS * GLA_DV
    widths = (hk, hk, hv, hv, 2 * GLA_RANK, DIFF_HEADS * 2 * DIFF_DK, DIFF_HEADS * 2 * DIFF_DK,
              DIFF_HEADS * DIFF_DV)
    offs = [0]
    for wd_ in widths:
        offs.append(offs[-1] + wd_)
    piece = lambda j: w[:, offs[j]:offs[j + 1]]
    pad = jnp.zeros((w.shape[0], EVEN_COLS - COL_ALR - 2 * GLA_RANK), w.dtype)
    main = jnp.concatenate([piece(0), piece(1), piece(2), piece(3), piece(5), piece(6), piece(4), pad], axis=1)
    return main.astype(BF16), piece(7).T.astype(BF16)


def _rot_half_cols(w):
    half = MLA_ROPE // 2
    shp = w.shape
    g = w.reshape(shp[0], -1, MLA_ROPE)
    return jnp.concatenate([-g[..., half:], g[..., :half]], axis=-1).reshape(shp)


def _mla_weights(w_in, w_uq, w_ukv):
    d = w_in.shape[0]
    dq = MLA_NOPE + MLA_ROPE
    z = lambda rows, cols: jnp.zeros((rows, cols), F32)
    w_kr = w_in[:, MLA_Q_LORA + MLA_KV_LORA:]
    kr_blk = lambda m: jnp.concatenate([z(d, MLA_NOPE), m, z(d, LANES - dq)], axis=1)
    win = jnp.concatenate([w_in[:, :MLA_Q_LORA + MLA_KV_LORA], kr_blk(w_kr), kr_blk(_rot_half_cols(w_kr))],
                          axis=1).astype(BF16)
    uq = w_uq.reshape(MLA_Q_LORA, MLA_HEADS, dq)
    pad_q = jnp.zeros((MLA_Q_LORA, MLA_HEADS, LANES - dq), F32)
    wq1 = jnp.concatenate([uq, pad_q], axis=-1).reshape(MLA_Q_LORA, -1).astype(BF16)
    rot = _rot_half_cols(uq[..., MLA_NOPE:].reshape(MLA_Q_LORA, -1)).reshape(MLA_Q_LORA, MLA_HEADS, MLA_ROPE)
    wq2 = jnp.concatenate([jnp.zeros((MLA_Q_LORA, MLA_HEADS, MLA_NOPE), F32), rot, pad_q],
                          axis=-1).reshape(MLA_Q_LORA, -1).astype(BF16)
    ukv = w_ukv.reshape(MLA_KV_LORA, MLA_HEADS, MLA_NOPE + MLA_DV)
    wk = jnp.concatenate([ukv[..., :MLA_NOPE], jnp.zeros((MLA_KV_LORA, MLA_HEADS, LANES - MLA_NOPE), F32)],
                         axis=-1).reshape(MLA_KV_LORA, -1).astype(BF16)
    wv = ukv[..., MLA_NOPE:].reshape(MLA_KV_LORA, -1).astype(BF16)
    return win, wq1, wq2, wk, wv


def _rope_tables(seq):
    half = MLA_ROPE // 2
    inv = ROPE_THETA ** (-jnp.arange(half, dtype=F32) / half)
    ang = jnp.arange(seq, dtype=F32)[:, None] * inv[None, :]
    cos = jnp.concatenate([jnp.cos(ang), jnp.cos(ang)], axis=1)
    sin = jnp.concatenate([jnp.sin(ang), jnp.sin(ang)], axis=1)
    dq = MLA_NOPE + MLA_ROPE
    scale = dq ** -0.5 * LOG2E
    lay = lambda a, fill: jnp.concatenate([jnp.full((seq, MLA_NOPE), fill, F32), a,
                                           jnp.zeros((seq, LANES - dq), F32)], axis=1)
    return lay(cos, 1.0) * scale, lay(sin, 0.0) * scale, lay(cos, 0.0), lay(sin, 0.0)


def _routing_plan(info, counts, n_tokens):
    tm = MOE_TILE
    cnt = counts[0, :N_EXPERTS].astype(jnp.int32)
    padded = ((cnt + tm - 1) // tm) * tm
    ends = jnp.cumsum(padded)
    starts = ends - padded
    e = info[:, 0:TOP_K].astype(jnp.int32)
    rank = info[:, 4:4 + TOP_K].astype(jnp.int32)
    pos = starts[e] + rank
    n_tiles = (n_tokens * TOP_K) // tm + N_EXPERTS
    tile_start = jnp.arange(n_tiles, dtype=jnp.int32) * tm
    tile_expert = jnp.minimum(jnp.sum(tile_start[:, None] >= ends[None, :], axis=1), N_EXPERTS - 1)
    n_used = (ends[-1] // tm).reshape(1)
    pad_info = jnp.concatenate([starts + cnt, padded - cnt, n_used])
    return pos, tile_expert.astype(jnp.int32), n_used.astype(jnp.int32), pad_info.astype(jnp.int32), n_tiles * tm


def kernel(x, rel_bias_table, even_w_in, gla_gate_up, gla_gate_bias, gla_norm_gain, diff_lambda, diff_norm_gain, even_w_out, ffn_w_gate, ffn_w_up, ffn_w_down, odd_w_in, mla_q_norm_gain, mla_kv_norm_gain, mla_w_uq, mla_w_ukv, odd_w_out, router_w, moe_w_gate, moe_w_up, moe_w_down, ln_gain, ln_bias):
    b, s, d = x.shape
    n = b * s
    tm = min(512, s)
    assert s % tm == 0 and tm % ATT_TQ == 0 and s % GLA_GROUP == 0
    x2 = x.reshape(n, d)
    row = lambda v: v.reshape(1, -1)

    w_main, w_vdt = _even_in_weight(even_w_in[0])
    proj, vd_t = _in_proj(x2, w_main, w_vdt, s, tm)
    proj = proj.reshape(b, s, EVEN_COLS)
    gup = jnp.zeros((2, LANES, GLA_HEADS * GLA_DK), F32)
    for dd in range(2):
        gup = gup.at[dd, dd * GLA_RANK:(dd + 1) * GLA_RANK].set(gla_gate_up[0, dd])
    o_gla = _gla(proj, gup.astype(BF16), gla_gate_bias[0][:, None, :], row(gla_norm_gain[0]))
    lam_init = 0.8 - 0.6 * math.exp(-0.3 * 0)
    lf = diff_lambda[0]
    lam = (jnp.exp(jnp.sum(lf[0] * lf[1])) - jnp.exp(jnp.sum(lf[2] * lf[3])) + lam_init).reshape(1)
    o_diff_t = _diff_attention(proj, vd_t, _bias_band(rel_bias_table), lam,
                               diff_norm_gain[0].reshape(-1, 1), lam_init)
    hv = GLA_HEADS * GLA_DV
    w_out = even_w_out[0].astype(BF16)
    x2 = _outproj_ln(o_gla.reshape(n, hv), o_diff_t, w_out[:hv], w_out[hv:], x2,
                     row(ln_gain[0, 0]), row(ln_bias[0, 0]), s, tm)
    x2 = _ffn_ln(x2, ffn_w_gate[0].astype(BF16), ffn_w_up[0].astype(BF16), ffn_w_down[0].astype(BF16),
                 row(ln_gain[0, 1]), row(ln_bias[0, 1]), tm)

    win, wq1, wq2, wk, wv = _mla_weights(odd_w_in[0], mla_w_uq[0], mla_w_ukv[0])
    q, k, v = _mla_proj(x2, win, row(mla_q_norm_gain[0]), row(mla_kv_norm_gain[0]), wq1, wq2, wk, wv,
                        _rope_tables(s), s, tm)
    o = _mla_attention(q.reshape(b, s, -1), k.reshape(b, s, -1), v.reshape(b, s, -1), min(MLA_TQ, s))
    rw = jnp.concatenate([router_w[0], jnp.zeros((d, LANES - N_EXPERTS), F32)], axis=1)
    rw_hi = rw.astype(BF16)
    rw2 = jnp.stack([rw_hi, (rw - rw_hi.astype(F32)).astype(BF16)])
    xn, info, counts = _outproj_route(o.reshape(n, -1), odd_w_out[0].astype(BF16), x2,
                                      row(ln_gain[1, 0]), row(ln_bias[1, 0]), rw2, s, tm)
    pos, tile_expert, n_used, pad_info, n_slots = _routing_plan(info, counts, n)
    pos_t = pos.reshape(n // tm, tm, TOP_K).transpose(0, 2, 1)
    x_sorted = _scatter_rows(xn, pos_t, pad_info, n_slots, tm)
    y_sorted = _moe_ffn(x_sorted, tile_expert, n_used, moe_w_gate[0].astype(BF16),
                        moe_w_up[0].astype(BF16), moe_w_down[0].astype(BF16))
    out = _combine_ln(xn, info, pos_t, y_sorted, row(ln_gain[1, 1]), row(ln_bias[1, 1]), tm)
    return out.reshape(b, s, d)
```

```python
import functools
import math

import jax
import jax.numpy as jnp
from jax import lax
from jax.experimental import pallas as pl
from jax.experimental.pallas import tpu as pltpu

F32 = jnp.float32
BF16 = jnp.bfloat16

LANES = 128
V7X_VMEM_BYTES = 64 * 1024 * 1024
VMEM_LIMIT = V7X_VMEM_BYTES - 8 * 1024 * 1024

D_MODEL = 1024
GLA_HEADS, GLA_DK, GLA_DV = 4, 64, 128
GLA_RANK, GLA_TEMP, GLA_CHUNK = 16, 16.0, 64
DIFF_HEADS, DIFF_DK, DIFF_DV = 4, 64, 128
N_BUCKETS, MAX_DISTANCE = 32, 128
MLA_HEADS, MLA_Q_LORA, MLA_KV_LORA = 16, 256, 128
MLA_NOPE, MLA_ROPE, MLA_DV = 64, 32, 64
ROPE_THETA = 10000.0
D_FF, N_EXPERTS, TOP_K = 2816, 8, 2
DEPTH = 2
ALPHA = (2 * DEPTH) ** 0.25
LN_EPS, RMS_EPS = 1e-5, 1e-6

COL_QK_G, COL_V_G, COL_G_G = 0, 512, 1024
COL_Q_D, COL_K_D, COL_ALR = 1536, 2048, 2560
EVEN_COLS = 2688

ATT_KC = 128
ATT_TQ = 256
MLA_TQ = 512
MLA_SUB = 256
LOG2E = math.log2(math.e)
NEG_BIG = -1e30

GLA_GROUP = 256
FF_CHUNK = 256
MOE_TILE = 512
DMA_UNROLL = 8


def _cparams(sem):
    return pltpu.CompilerParams(dimension_semantics=sem, vmem_limit_bytes=VMEM_LIMIT)


def _layer_norm(y, g, b):
    mu = jnp.mean(y, axis=-1, keepdims=True)
    d = y - mu
    var = jnp.mean(d * d, axis=-1, keepdims=True)
    return d * lax.rsqrt(var + LN_EPS) * g + b


def _split3(x):
    h1 = x.astype(BF16)
    r1 = x - h1.astype(F32)
    h2 = r1.astype(BF16)
    h3 = (r1 - h2.astype(F32)).astype(BF16)
    return h1, h2, h3


def _dot(a, b):
    return jnp.dot(a, b, preferred_element_type=F32)


def _dot_nt(a, b):
    return lax.dot_general(a, b, (((1,), (1,)), ((), ())), preferred_element_type=F32)


def _dot_tn(a, b):
    return lax.dot_general(a, b, (((0,), (0,)), ((), ())), preferred_element_type=F32)


def _store_chunks(ref, val_t, width):
    for j in range(val_t.shape[1] // width):
        ref[j] = val_t[:, j * width:(j + 1) * width].astype(ref.dtype)


def _in_proj_kernel(x_ref, w_ref, wvt_ref, o_ref, vt_ref):
    xb = x_ref[...].astype(BF16)
    o_ref[...] = _dot(xb, w_ref[...]).astype(o_ref.dtype)
    _store_chunks(vt_ref, _dot_nt(wvt_ref[...], xb), ATT_KC)


def _in_proj(x, w, wvt, seq, tm):
    n, k = x.shape
    m = w.shape[1]
    dvt = wvt.shape[0]
    per_seq = seq // tm
    return pl.pallas_call(
        _in_proj_kernel,
        grid=(n // tm,),
        in_specs=[pl.BlockSpec((tm, k), lambda i: (i, 0)),
                  pl.BlockSpec((k, m), lambda i: (0, 0)),
                  pl.BlockSpec((dvt, k), lambda i: (0, 0))],
        out_specs=[pl.BlockSpec((tm, m), lambda i: (i, 0)),
                   pl.BlockSpec((None, tm // ATT_KC, dvt, ATT_KC),
                                lambda i: (i // per_seq, i % per_seq, 0, 0))],
        out_shape=[jax.ShapeDtypeStruct((n, m), BF16),
                   jax.ShapeDtypeStruct((n // seq, seq // ATT_KC, dvt, ATT_KC), BF16)],
        compiler_params=_cparams(("parallel",)),
        name="in_proj",
    )(x, w, wvt)


def _gla_kernel(qk_ref, v_ref, g_ref, alr_ref, gup_ref, gbias_ref, gain_ref, o_ref,
                la_ref, qdec_ref, kdec_ref, decay_ref, acc_ref, st_ref):
    seq = qk_ref.shape[0]
    hk = GLA_HEADS * GLA_DK
    hv = GLA_HEADS * GLA_DV
    grp = GLA_GROUP
    c = GLA_CHUNK

    alr = alr_ref[...]
    for d in range(2):
        logits = _dot(alr, gup_ref[d]) + gbias_ref[d]
        log_sig = jnp.minimum(logits, 0.0) - jnp.log(1.0 + jnp.exp(-jnp.abs(logits)))
        la_ref[d] = log_sig / GLA_TEMP

    row = lax.broadcasted_iota(jnp.int32, (grp, grp), 0)
    col = lax.broadcasted_iota(jnp.int32, (grp, grp), 1)
    same = (row // c) == (col // c)
    tri = [same & (row >= col), same & (row <= col)]
    cum_inc = [jnp.where(t, 1.0, 0.0).astype(BF16) for t in tri]
    cum_rem = [jnp.where(same & (row < col), 1.0, 0.0).astype(BF16),
               jnp.where(same & (row > col), 1.0, 0.0).astype(BF16)]
    lane_k = lax.broadcasted_iota(jnp.int32, (grp, hk), 1) // GLA_DK
    scale = GLA_DK ** -0.5

    def group_body(r, carry):
        rows = pl.ds(pl.multiple_of(r * grp, grp), grp)
        q = qk_ref[rows, 0:hk].astype(F32)
        k = qk_ref[rows, hk:2 * hk].astype(F32)
        v = v_ref[rows, :]
        o_heads = [jnp.zeros((grp, GLA_DV), F32) for _ in range(GLA_HEADS)]
        for d in range(2):
            parts = _split3(la_ref[d, rows, :])
            bcum = sum(_dot(cum_inc[d], p) for p in parts)
            brem = sum(_dot(cum_rem[d], p) for p in parts)
            q_dec = q * jnp.exp(bcum) * scale
            k_inv = (k * jnp.exp(-bcum)).astype(BF16)
            qdec_ref[d, rows, :] = q_dec.astype(BF16)
            kdec_ref[d, rows, :] = (k * jnp.exp(brem)).astype(BF16)
            decay_ref[d, rows, :] = jnp.exp(bcum + brem)
            for h in range(GLA_HEADS):
                q_h = jnp.where(lane_k == h, q_dec, 0.0).astype(BF16)
                attn = jnp.where(tri[d], _dot_nt(q_h, k_inv), 0.0)
                o_heads[h] = o_heads[h] + _dot(attn.astype(BF16), v[:, h * GLA_DV:(h + 1) * GLA_DV])
        for h in range(GLA_HEADS):
            acc_ref[rows, h * GLA_DV:(h + 1) * GLA_DV] = o_heads[h]
        return carry

    lax.fori_loop(0, seq // grp, group_body, 0)

    srow = lax.broadcasted_iota(jnp.int32, (hv, hk), 0) // GLA_DV
    scol = lax.broadcasted_iota(jnp.int32, (hv, hk), 1) // GLA_DK
    head_mask = srow == scol
    n_chunks = seq // c
    st_ref[...] = jnp.zeros_like(st_ref)

    def chunk_body(i, carry):
        for d in range(2):
            ci = i if d == 0 else n_chunks - 1 - i
            rows = pl.ds(pl.multiple_of(ci * c, c), c)
            state = st_ref[d]
            acc_ref[rows, :] += _dot_nt(qdec_ref[d, rows, :], state.astype(BF16))
            kv = _dot_tn(v_ref[rows, :], kdec_ref[d, rows, :])
            decay = decay_ref[d, pl.ds(pl.multiple_of(ci * c, c), 1), :]
            st_ref[d] = state * decay + jnp.where(head_mask, kv, 0.0)
        return carry

    lax.fori_loop(0, n_chunks, chunk_body, 0, unroll=2)

    gain = gain_ref[...]
    for h in range(GLA_HEADS):
        sl = slice(h * GLA_DV, (h + 1) * GLA_DV)
        o = acc_ref[:, sl]
        y = o * lax.rsqrt(jnp.mean(o * o, axis=-1, keepdims=True) + RMS_EPS) * gain
        gate = g_ref[:, sl].astype(F32)
        o_ref[:, sl] = (y * (gate * jax.nn.sigmoid(gate))).astype(o_ref.dtype)


def _gla(proj, gup, gbias, gain):
    b, s, _ = proj.shape
    hk, hv = GLA_HEADS * GLA_DK, GLA_HEADS * GLA_DV
    blk = lambda width, col: pl.BlockSpec((None, s, width), lambda i: (i, 0, col // width))
    return pl.pallas_call(
        _gla_kernel,
        grid=(b,),
        in_specs=[blk(2 * hk, COL_QK_G), blk(hv, COL_V_G), blk(hv, COL_G_G), blk(LANES, COL_ALR),
                  pl.BlockSpec((2, LANES, hk), lambda i: (0, 0, 0)),
                  pl.BlockSpec((2, 1, hk), lambda i: (0, 0, 0)),
                  pl.BlockSpec((1, GLA_DV), lambda i: (0, 0))],
        out_specs=pl.BlockSpec((None, s, hv), lambda i: (i, 0, 0)),
        out_shape=jax.ShapeDtypeStruct((b, s, hv), BF16),
        scratch_shapes=[pltpu.VMEM((2, s, hk), F32),
                        pltpu.VMEM((2, s, hk), BF16),
                        pltpu.VMEM((2, s, hk), BF16),
                        pltpu.VMEM((2, s, hk), F32),
                        pltpu.VMEM((s, hv), F32),
                        pltpu.VMEM((2, hv, hk), F32)],
        compiler_params=_cparams(("parallel",)),
        name="gla",
    )(proj, proj, proj, proj, gup, gbias, gain)


J_BELOW = (-MAX_DISTANCE - ATT_KC + 1) // ATT_KC
J_ABOVE = -(-(MAX_DISTANCE + ATT_TQ - 1) // ATT_KC)
N_BAND = J_ABOVE - J_BELOW + 1
assert ATT_KC * J_BELOW + ATT_KC - 1 <= -MAX_DISTANCE
assert ATT_KC * J_ABOVE - (ATT_TQ - 1) >= MAX_DISTANCE


def _band_kernel(tab_ref, o_ref, *, n_heads):
    h = pl.program_id(0)
    half = N_BUCKETS // 2
    max_exact = half // 2
    shape = (ATT_KC, ATT_TQ)
    for t in range(N_BAND):
        j = t + J_BELOW
        rel =(j * ATT_KC + lax.broadcasted_iota(jnp.int32, shape, 0)
               - lax.broadcasted_iota(jnp.int32, shape, 1))
        bucket = jnp.where(rel > 0, half, 0).astype(jnp.int32)
        n = jnp.abs(rel)
        n_large = max_exact + (jnp.log(jnp.maximum(n, max_exact).astype(F32) / max_exact)
                               / math.log(MAX_DISTANCE / max_exact) * (half - max_exact)).astype(jnp.int32)
        n_large = jnp.minimum(n_large, half - 1)
        bucket = bucket + jnp.where(n < max_exact, n, n_large)
        val = jnp.zeros(shape, F32)
        for bkt in range(N_BUCKETS):
            val = jnp.where(bucket == bkt, tab_ref[bkt * n_heads + h], val)
        o_ref[t] = val * LOG2E


def _bias_band(table):
    n_heads = table.shape[1]
    return pl.pallas_call(
        functools.partial(_band_kernel, n_heads=n_heads),
        grid=(n_heads,),
        in_specs=[pl.BlockSpec(memory_space=pltpu.SMEM)],
        out_specs=pl.BlockSpec((None, N_BAND, ATT_KC, ATT_TQ), lambda h: (h, 0, 0, 0)),
        out_shape=jax.ShapeDtypeStruct((n_heads, N_BAND, ATT_KC, ATT_TQ), F32),
        compiler_params=_cparams(("parallel",)),
        name="bias_band",
    )(table.reshape(-1))


def _flash_t(q_ts, k_ref, k_cols, vt_ref, vt_rows, band, n_chunks):
    n_streams = len(q_ts)
    tq = q_ts[0].shape[0]
    dv = vt_rows[0].stop - vt_rows[0].start
    m = [jnp.full((1, tq), NEG_BIG, F32)] * n_streams
    l = [jnp.zeros((1, tq), F32)] * n_streams
    acc = [jnp.zeros((dv, tq), F32)] * n_streams
    scores = lambda i, c: _dot_nt(k_ref[c * ATT_KC:(c + 1) * ATT_KC, k_cols[i]], q_ts[i])
    s_next = [scores(i, 0) for i in range(n_streams)]
    for c in range(n_chunks):
        s_cur = s_next
        if c + 1 < n_chunks:
            s_next = [scores(i, c + 1) for i in range(n_streams)]
        bias = band(c) if band is not None else None
        for i in range(n_streams):
            s = s_cur[i] if bias is None else s_cur[i] + bias
            m_new = jnp.maximum(m[i], jnp.max(s, axis=0, keepdims=True))
            p = jnp.exp2(s - m_new)
            alpha = jnp.exp2(m[i] - m_new)
            l[i] = alpha * l[i] + jnp.sum(p, axis=0, keepdims=True)
            acc[i] = alpha * acc[i] + _dot(vt_ref[c, vt_rows[i], :], p.astype(BF16))
            m[i] = m_new
    return [a * (1.0 / li) for a, li in zip(acc, l)]


def _diff_kernel(lam_ref, q_ref, k_ref, vt_ref, band_ref, gain_ref, o_ref, *, out_scale):
    seq = q_ref.shape[0]
    n_chunks = seq // ATT_KC
    lane = lax.broadcasted_iota(jnp.int32, (ATT_TQ, 2 * DIFF_DK), 1)
    scale = DIFF_DK ** -0.5 * LOG2E
    all_rows = slice(0, DIFF_DV)

    def tile_body(qi, carry):
        rows = pl.ds(pl.multiple_of(qi * ATT_TQ, ATT_TQ), ATT_TQ)
        q = (q_ref[rows, :].astype(F32) * scale).astype(BF16)
        zero = jnp.zeros_like(q)
        band = lambda c: band_ref[jnp.clip(c - (ATT_TQ // ATT_KC) * qi - J_BELOW, 0, N_BAND - 1)]
        maps = [jnp.where(lane < DIFF_DK, q, zero), jnp.where(lane >= DIFF_DK, q, zero)]
        o0, o1 = _flash_t(maps, k_ref, [slice(None)] * 2, vt_ref, [all_rows] * 2, band, n_chunks)
        o = o0 - lam_ref[0] * o1
        y = o * lax.rsqrt(jnp.mean(o * o, axis=0, keepdims=True) + RMS_EPS) * gain_ref[...]
        o_ref[qi] = (y * out_scale).astype(o_ref.dtype)
        return carry

    lax.fori_loop(0, seq // ATT_TQ, tile_body, 0)


def _diff_attention(proj, vt, band, lam, gain_col, lam_init):
    b, s, _ = proj.shape
    h = DIFF_HEADS
    w = 2 * DIFF_DK
    assert w == LANES
    return pl.pallas_call(
        functools.partial(_diff_kernel, out_scale=1.0 - lam_init),
        grid=(h, b),
        in_specs=[pl.BlockSpec(memory_space=pltpu.SMEM),
                  pl.BlockSpec((None, s, w), lambda hh, bb: (bb, 0, COL_Q_D // w + hh)),
                  pl.BlockSpec((None, s, w), lambda hh, bb: (bb, 0, COL_K_D // w + hh)),
                  pl.BlockSpec((None, s // ATT_KC, DIFF_DV, ATT_KC), lambda hh, bb: (bb, 0, hh, 0)),
                  pl.BlockSpec((None, N_BAND, ATT_KC, ATT_TQ), lambda hh, bb: (hh, 0, 0, 0)),
                  pl.BlockSpec((DIFF_DV, 1), lambda hh, bb: (0, 0))],
        out_specs=pl.BlockSpec((None, s // ATT_TQ, DIFF_DV, ATT_TQ), lambda hh, bb: (bb, 0, hh, 0)),
        out_shape=jax.ShapeDtypeStruct((b, s // ATT_TQ, h * DIFF_DV, ATT_TQ), BF16),
        compiler_params=_cparams(("parallel", "parallel")),
        name="diff_attn",
    )(lam, proj, proj, vt, band, gain_col)


def _residual_ln(x_ref, o_ref, g_ref, b_ref, row_pairs, col_pairs):
    for j in range(x_ref.shape[0] // ATT_TQ):
        rows = slice(j * ATT_TQ, (j + 1) * ATT_TQ)
        h = ALPHA * x_ref[rows, :]
        for a_ref, w_ref in row_pairs:
            h = h + _dot(a_ref[rows, :], w_ref[...])
        for a_ref, w_ref in col_pairs:
            h = h + _dot_tn(a_ref[j], w_ref[...])
        o_ref[rows, :] = _layer_norm(h, g_ref[...], b_ref[...])


def _outproj_kernel(a_ref, at_ref, wa_ref, wt_ref, x_ref, g_ref, b_ref, o_ref):
    _residual_ln(x_ref, o_ref, g_ref, b_ref, [(a_ref, wa_ref)], [(at_ref, wt_ref)])


def _tile_specs(n, d, seq, tm):
    per_seq = seq // tm
    rows = lambda width: pl.BlockSpec((tm, width), lambda i: (i, 0))
    cols = lambda k: pl.BlockSpec((None, tm // ATT_TQ, k, ATT_TQ), lambda i: (i // per_seq, i % per_seq, 0, 0))
    full = lambda a: pl.BlockSpec(a.shape, lambda i: (0,) * a.ndim)
    return rows, cols, full


def _outproj_ln(a, at, wa, wt, x, g, b, seq, tm):
    n, d = x.shape
    rows, cols, full = _tile_specs(n, d, seq, tm)
    return pl.pallas_call(
        _outproj_kernel,
        grid=(n // tm,),
        in_specs=[rows(a.shape[1]), cols(at.shape[2]), full(wa), full(wt), rows(d), full(g), full(b)],
        out_specs=rows(d),
        out_shape=jax.ShapeDtypeStruct((n, d), F32),
        compiler_params=_cparams(("parallel",)),
        name="outproj_ln",
    )(a, at, wa, wt, x, g, b)


def _outproj_route_kernel(a_ref, w_ref, x_ref, g_ref, b_ref, rw_ref, o_ref, info_ref, cnt_ref, carry_ref):
    i = pl.program_id(0)
    tm = x_ref.shape[0]

    @pl.when(i == 0)
    def _():
        carry_ref[...] = jnp.zeros_like(carry_ref)

    _residual_ln(x_ref, o_ref, g_ref, b_ref, [(a_ref, w_ref)], [])
    xn = o_ref[...]

    x_hi = xn.astype(BF16)
    x_lo = (xn - x_hi.astype(F32)).astype(BF16)
    logits = _dot(x_hi, rw_ref[0]) + (_dot(x_hi, rw_ref[1]) + _dot(x_lo, rw_ref[0]))
    lane = lax.broadcasted_iota(jnp.int32, logits.shape, 1).astype(F32)
    neg = jnp.float32(-jnp.inf)
    logits = jnp.where(lane < N_EXPERTS, logits, neg)
    v1 = jnp.max(logits, axis=-1, keepdims=True)
    e1 = jnp.min(jnp.where(logits == v1, lane, float(LANES)), axis=-1, keepdims=True)
    rest = jnp.where(lane == e1, neg, logits)
    v2 = jnp.max(rest, axis=-1, keepdims=True)
    e2 = jnp.min(jnp.where(rest == v2, lane, float(LANES)), axis=-1, keepdims=True)
    t = jnp.exp(v2 - v1)
    w1 = 1.0 / (1.0 + t)
    w2 = t / (1.0 + t)

    onehot = jnp.where((lane == e1) | (lane == e2), 1.0, 0.0)
    row = lax.broadcasted_iota(jnp.int32, (tm, tm), 0)
    col = lax.broadcasted_iota(jnp.int32, (tm, tm), 1)
    before = jnp.where(row > col, 1.0, 0.0).astype(BF16)
    prior = carry_ref[...] + _dot(before, onehot.astype(BF16))
    r1 = jnp.sum(jnp.where(lane == e1, prior, 0.0), axis=-1, keepdims=True)
    r2 = jnp.sum(jnp.where(lane == e2, prior, 0.0), axis=-1, keepdims=True)
    carry_ref[...] += jnp.sum(onehot, axis=0, keepdims=True)
    cnt_ref[...] = jnp.broadcast_to(carry_ref[...], cnt_ref.shape)

    info = jnp.zeros(logits.shape, F32)
    for idx, val in enumerate((e1, e2, w1, w2, r1, r2)):
        info = jnp.where(lane == idx, val, info)
    info_ref[...] = info


def _outproj_route(a, w, x, g, b, router_w2, seq, tm):
    n, d = x.shape
    rows, cols, full = _tile_specs(n, d, seq, tm)
    return pl.pallas_call(
        _outproj_route_kernel,
        grid=(n // tm,),
        in_specs=[rows(a.shape[1]), full(w), rows(d), full(g), full(b), full(router_w2)],
        out_specs=[rows(d), rows(LANES),
                   pl.BlockSpec((8, LANES), lambda i: (0, 0))],
        out_shape=[jax.ShapeDtypeStruct((n, d), F32),
                   jax.ShapeDtypeStruct((n, LANES), F32),
                   jax.ShapeDtypeStruct((8, LANES), F32)],
        scratch_shapes=[pltpu.VMEM((1, LANES), F32)],
        compiler_params=_cparams(("arbitrary",)),
        name="outproj_route",
    )(a, w, x, g, b, router_w2)


def _swiglu_acc(xb, wg_ref, wu_ref, wd_ref, acc_ref, after_chunk=None):
    n_chunks = wg_ref.shape[-1] // FF_CHUNK
    for c in range(n_chunks):
        cols = slice(c * FF_CHUNK, (c + 1) * FF_CHUNK)
        gate = _dot(xb, wg_ref[:, cols])
        up = _dot(xb, wu_ref[:, cols])
        hidden = (gate * jax.nn.sigmoid(gate) * up).astype(BF16)
        part = _dot(hidden, wd_ref[cols, :])
        if c == 0:
            acc_ref[...] = part
        else:
            acc_ref[...] += part
        if after_chunk is not None:
            after_chunk(c, n_chunks)


def _ffn_ln_kernel(x_ref, wg_ref, wu_ref, wd_ref, g_ref, b_ref, o_ref):
    x = x_ref[...]
    _swiglu_acc(x.astype(BF16), wg_ref, wu_ref, wd_ref, o_ref)
    o_ref[...] = _layer_norm(ALPHA * x + o_ref[...], g_ref[...], b_ref[...])


def _ffn_ln(x, wg, wu, wd, g, b, tm):
    n, d = x.shape
    f = wg.shape[1]
    once = lambda shape: pl.BlockSpec(shape, lambda i: (0,) * len(shape), pipeline_mode=pl.Buffered(1))
    return pl.pallas_call(
        _ffn_ln_kernel,
        grid=(n // tm,),
        in_specs=[pl.BlockSpec((tm, d), lambda i: (i, 0)),
                  once((d, f)), once((d, f)), once((f, d)), once((1, d)), once((1, d))],
        out_specs=pl.BlockSpec((tm, d), lambda i: (i, 0)),
        out_shape=jax.ShapeDtypeStruct((n, d), F32),
        compiler_params=_cparams(("parallel",)),
        name="ffn_ln",
    )(x, wg, wu, wd, g, b)


def _moe_kernel(te_ref, nu_ref, first_ref, next_ref, x_hbm, wg_ref, wu_ref, wd_ref, o_ref, xbuf, gsem):
    t = pl.program_id(0)
    tm = o_ref.shape[0]
    slot = t % 2

    def row_copy(idx_ref, r, s):
        return pltpu.make_async_copy(x_hbm.at[pl.ds(idx_ref[0, 0, r], 1)], xbuf.at[s, pl.ds(r, 1)], gsem.at[s])

    def gather_tile(idx_ref, s):
        def body(r, carry):
            row_copy(idx_ref, r, s).start()
            return carry
        lax.fori_loop(0, tm, body, 0, unroll=DMA_UNROLL)

    def wait_tile(s):
        pltpu.make_async_copy(x_hbm.at[pl.ds(0, tm)], xbuf.at[s], gsem.at[s]).wait()

    @pl.when(t == 0)
    def _():
        gather_tile(first_ref, slot)

    wait_tile(slot)

    @pl.when(t < nu_ref[0])
    def _():
        def gather_part(c, n_chunks):
            for r in range(c * tm // n_chunks, (c + 1) * tm // n_chunks):
                row_copy(next_ref, r, 1 - slot).start()
        _swiglu_acc(xbuf[slot].astype(BF16), wg_ref, wu_ref, wd_ref, o_ref, gather_part)

    @pl.when(t >= nu_ref[0])
    def _():
        o_ref[...] = jnp.zeros_like(o_ref)
        gather_tile(next_ref, 1 - slot)

    @pl.when(t == pl.num_programs(0) - 1)
    def _():
        wait_tile(1 - slot)


def _moe_ffn(x, src_rows, tile_expert, n_used, wg, wu, wd):
    n_tiles = src_rows.shape[0]
    d = x.shape[1]
    f = wg.shape[2]
    tm = MOE_TILE
    smem_tile = lambda index_map: pl.BlockSpec((1, 1, tm), index_map, memory_space=pltpu.SMEM)
    grid_spec = pltpu.PrefetchScalarGridSpec(
        num_scalar_prefetch=2,
        grid=(n_tiles,),
        in_specs=[smem_tile(lambda t, te, nu: (0, 0, 0)),
                  smem_tile(lambda t, te, nu: (jnp.minimum(t + 1, n_tiles - 1), 0, 0)),
                  pl.BlockSpec(memory_space=pl.ANY),
                  pl.BlockSpec((None, d, f), lambda t, te, nu: (te[t], 0, 0)),
                  pl.BlockSpec((None, d, f), lambda t, te, nu: (te[t], 0, 0)),
                  pl.BlockSpec((None, f, d), lambda t, te, nu: (te[t], 0, 0))],
        out_specs=pl.BlockSpec((tm, d), lambda t, te, nu: (t, 0)),
        scratch_shapes=[pltpu.VMEM((2, tm, d), F32),
                        pltpu.SemaphoreType.DMA((2,))],
    )
    return pl.pallas_call(
        _moe_kernel,
        grid_spec=grid_spec,
        out_shape=jax.ShapeDtypeStruct((n_tiles * tm, d), F32),
        compiler_params=_cparams(("arbitrary",)),
        name="moe_ffn",
    )(tile_expert, n_used, src_rows, src_rows, x, wg, wu, wd)


def _combine_kernel(pos_ref, x_ref, info_ref, g_ref, b_ref, y_hbm, o_ref, buf_ref, sem):
    tm = x_ref.shape[0]

    def issue(r, carry):
        for kk in range(TOP_K):
            src = pos_ref[0, kk, r]
            pltpu.make_async_copy(y_hbm.at[pl.ds(src, 1)], buf_ref.at[kk, pl.ds(r, 1)], sem).start()
        return carry

    lax.fori_loop(0, tm, issue, 0, unroll=DMA_UNROLL)

    for kk in range(TOP_K):
        pltpu.make_async_copy(y_hbm.at[pl.ds(0, tm)], buf_ref.at[kk], sem).wait()

    info = info_ref[...]
    w1 = info[:, 2:3]
    w2 = info[:, 3:4]
    f = w1 * buf_ref[0] + w2 * buf_ref[1]
    o_ref[...] = _layer_norm(ALPHA * x_ref[...] + f, g_ref[...], b_ref[...])


def _combine_ln(x, info, pos, y_sorted, g, b, tm):
    n, d = x.shape
    return pl.pallas_call(
        _combine_kernel,
        grid=(n // tm,),
        in_specs=[pl.BlockSpec((1, TOP_K, tm), lambda i: (i, 0, 0), memory_space=pltpu.SMEM),
                  pl.BlockSpec((tm, d), lambda i: (i, 0)),
                  pl.BlockSpec((tm, LANES), lambda i: (i, 0)),
                  pl.BlockSpec((1, d), lambda i: (0, 0)),
                  pl.BlockSpec((1, d), lambda i: (0, 0)),
                  pl.BlockSpec(memory_space=pl.ANY)],
        out_specs=pl.BlockSpec((tm, d), lambda i: (i, 0)),
        out_shape=jax.ShapeDtypeStruct((n, d), F32),
        scratch_shapes=[pltpu.VMEM((TOP_K, tm, d), F32),
                        pltpu.SemaphoreType.DMA(())],
        compiler_params=_cparams(("arbitrary",)),
        name="combine_ln",
    )(pos, x, info, g, b, y_sorted)


def _mla_proj_kernel(x_ref, win_ref, qg_ref, kvg_ref, wq1_ref, wq2_ref, wk_ref, wv_ref,
                     cq_ref, sq_ref, ck_ref, sk_ref, q_ref, k_ref, v_ref):
    c = _dot(x_ref[...].astype(BF16), win_ref[...])
    cq = c[:, :MLA_Q_LORA]
    ckv = c[:, MLA_Q_LORA:MLA_Q_LORA + MLA_KV_LORA]
    off = MLA_Q_LORA + MLA_KV_LORA
    k_rope = c[:, off:off + LANES] * ck_ref[...] + c[:, off + LANES:off + 2 * LANES] * sk_ref[...]
    cq = (cq * lax.rsqrt(jnp.mean(cq * cq, axis=-1, keepdims=True) + RMS_EPS) * qg_ref[...]).astype(BF16)
    ckv = (ckv * lax.rsqrt(jnp.mean(ckv * ckv, axis=-1, keepdims=True) + RMS_EPS) * kvg_ref[...]).astype(BF16)
    q1 = _dot(cq, wq1_ref[...])
    q2 = _dot(cq, wq2_ref[...])
    k1 = _dot(ckv, wk_ref[...])
    v_ref[...] = _dot(ckv, wv_ref[...]).astype(v_ref.dtype)
    cos_q, sin_q = cq_ref[...], sq_ref[...]
    for h in range(MLA_HEADS):
        sl = slice(h * LANES, (h + 1) * LANES)
        q_ref[:, sl] = (q1[:, sl] * cos_q + q2[:, sl] * sin_q).astype(q_ref.dtype)
        k_ref[:, sl] = (k1[:, sl] + k_rope).astype(k_ref.dtype)


def _mla_proj(x, win, qg, kvg, wq1, wq2, wk, wv, tabs, seq, tm):
    n, d = x.shape
    per_seq = seq // tm
    full = lambda a: pl.BlockSpec(a.shape, lambda i: (0,) * a.ndim)
    tab = pl.BlockSpec((tm, LANES), lambda i: (i % per_seq, 0))
    hw = MLA_HEADS * LANES
    return pl.pallas_call(
        _mla_proj_kernel,
        grid=(n // tm,),
        in_specs=[pl.BlockSpec((tm, d), lambda i: (i, 0)), full(win), full(qg), full(kvg),
                  full(wq1), full(wq2), full(wk), full(wv), tab, tab, tab, tab],
        out_specs=[pl.BlockSpec((tm, hw), lambda i: (i, 0)),
                   pl.BlockSpec((tm, hw), lambda i: (i, 0)),
                   pl.BlockSpec((tm, MLA_HEADS * MLA_DV), lambda i: (i, 0))],
        out_shape=[jax.ShapeDtypeStruct((n, hw), BF16),
                   jax.ShapeDtypeStruct((n, hw), BF16),
                   jax.ShapeDtypeStruct((n, MLA_HEADS * MLA_DV), BF16)],
        compiler_params=_cparams(("parallel",)),
        name="mla_proj",
    )(x, win, qg, kvg, wq1, wq2, wk, wv, *tabs)


def _mla_attn_kernel(q_ref, k_ref, v_ref, o_ref):
    v = v_ref[...]
    n_sub = q_ref.shape[0] // MLA_SUB
    items = [(r, hh) for r in range(n_sub) for hh in range(2)]

    def scores(item):
        r, hh = item
        sl = slice(hh * LANES, (hh + 1) * LANES)
        return _dot_nt(q_ref[r * MLA_SUB:(r + 1) * MLA_SUB, sl], k_ref[:, sl])

    outs = {}
    s_next = scores(items[0])
    for idx, item in enumerate(items):
        s = s_next
        if idx + 1 < len(items):
            s_next = scores(items[idx + 1])
        p = jnp.exp2(s - jnp.max(s, axis=-1, keepdims=True))
        l = jnp.sum(p, axis=-1, keepdims=True)
        outs[item] = _dot(p.astype(BF16), v) * (1.0 / l)
    lane = lax.broadcasted_iota(jnp.int32, (MLA_SUB, LANES), 1)
    for r in range(n_sub):
        o_ref[r * MLA_SUB:(r + 1) * MLA_SUB, :] = jnp.where(lane < MLA_DV, outs[(r, 0)], outs[(r, 1)]).astype(o_ref.dtype)


def _mla_attention(q, k, v, tq):
    b, s, _ = q.shape
    pairs = MLA_HEADS // 2
    assert 2 * MLA_DV == LANES
    return pl.pallas_call(
        _mla_attn_kernel,
        grid=(b, pairs, s // tq),
        in_specs=[pl.BlockSpec((None, tq, 2 * LANES), lambda bb, p, i: (bb, i, p)),
                  pl.BlockSpec((None, s, 2 * LANES), lambda bb, p, i: (bb, 0, p)),
                  pl.BlockSpec((None, s, LANES), lambda bb, p, i: (bb, 0, p))],
        out_specs=pl.BlockSpec((None, tq, LANES), lambda bb, p, i: (bb, i, p)),
        out_shape=jax.ShapeDtypeStruct((b, s, MLA_HEADS * MLA_DV), BF16),
        compiler_params=_cparams(("parallel", "parallel", "parallel")),
        name="mla_attn",
    )(q, k, v)


def _even_in_weight(w):
    hk, hv = GLA_HEADS * GLA_DK, GLA_HEADS * GLA_DV
    widths = (hk, hk, hv, hv, 2 * GLA_RANK, DIFF_HEADS * 2 * DIFF_DK, DIFF_HEADS * 2 * DIFF_DK,
              DIFF_HEADS * DIFF_DV)
    offs = [0]
    for wd_ in widths:
        offs.append(offs[-1] + wd_)
    piece = lambda j: w[:, offs[j]:offs[j + 1]]
    pad = jnp.zeros((w.shape[0], EVEN_COLS - COL_ALR - 2 * GLA_RANK), w.dtype)
    main = jnp.concatenate([piece(0), piece(1), piece(2), piece(3), piece(5), piece(6), piece(4), pad], axis=1)
    return main.astype(BF16), piece(7).T.astype(BF16)


def _rot_half_cols(w):
    half = MLA_ROPE // 2
    shp = w.shape
    g = w.reshape(shp[0], -1, MLA_ROPE)
    return jnp.concatenate([-g[..., half:], g[..., :half]], axis=-1).reshape(shp)


def _mla_weights(w_in, w_uq, w_ukv):
    d = w_in.shape[0]
    dq = MLA_NOPE + MLA_ROPE
    z = lambda rows, cols: jnp.zeros((rows, cols), F32)
    w_kr = w_in[:, MLA_Q_LORA + MLA_KV_LORA:]
    kr_blk = lambda m: jnp.concatenate([z(d, MLA_NOPE), m, z(d, LANES - dq)], axis=1)
    win = jnp.concatenate([w_in[:, :MLA_Q_LORA + MLA_KV_LORA], kr_blk(w_kr), kr_blk(_rot_half_cols(w_kr))],
                          axis=1).astype(BF16)
    uq = w_uq.reshape(MLA_Q_LORA, MLA_HEADS, dq)
    pad_q = jnp.zeros((MLA_Q_LORA, MLA_HEADS, LANES - dq), F32)
    wq1 = jnp.concatenate([uq, pad_q], axis=-1).reshape(MLA_Q_LORA, -1).astype(BF16)
    rot = _rot_half_cols(uq[..., MLA_NOPE:].reshape(MLA_Q_LORA, -1)).reshape(MLA_Q_LORA, MLA_HEADS, MLA_ROPE)
    wq2 = jnp.concatenate([jnp.zeros((MLA_Q_LORA, MLA_HEADS, MLA_NOPE), F32), rot, pad_q],
                          axis=-1).reshape(MLA_Q_LORA, -1).astype(BF16)
    ukv = w_ukv.reshape(MLA_KV_LORA, MLA_HEADS, MLA_NOPE + MLA_DV)
    wk = jnp.concatenate([ukv[..., :MLA_NOPE], jnp.zeros((MLA_KV_LORA, MLA_HEADS, LANES - MLA_NOPE), F32)],
                         axis=-1).reshape(MLA_KV_LORA, -1).astype(BF16)
    wv = ukv[..., MLA_NOPE:].reshape(MLA_KV_LORA, -1).astype(BF16)
    return win, wq1, wq2, wk, wv


def _rope_tables(seq):
    half = MLA_ROPE // 2
    inv = ROPE_THETA ** (-jnp.arange(half, dtype=F32) / half)
    ang = jnp.arange(seq, dtype=F32)[:, None] * inv[None, :]
    cos = jnp.concatenate([jnp.cos(ang), jnp.cos(ang)], axis=1)
    sin = jnp.concatenate([jnp.sin(ang), jnp.sin(ang)], axis=1)
    dq = MLA_NOPE + MLA_ROPE
    scale = dq ** -0.5 * LOG2E
    lay = lambda a, fill: jnp.concatenate([jnp.full((seq, MLA_NOPE), fill, F32), a,
                                           jnp.zeros((seq, LANES - dq), F32)], axis=1)
    return lay(cos, 1.0) * scale, lay(sin, 0.0) * scale, lay(cos, 0.0), lay(sin, 0.0)


def _routing_plan(info, counts, n_tokens):
    tm = MOE_TILE
    cnt = counts[0, :N_EXPERTS].astype(jnp.int32)
    padded = ((cnt + tm - 1) // tm) * tm
    ends = jnp.cumsum(padded)
    starts = ends - padded
    e = info[:, 0:TOP_K].astype(jnp.int32)
    rank = info[:, 4:4 + TOP_K].astype(jnp.int32)
    pos = starts[e] + rank
    n_tiles = (n_tokens * TOP_K) // tm + N_EXPERTS
    tile_start = jnp.arange(n_tiles, dtype=jnp.int32) * tm
    tile_expert = jnp.minimum(jnp.sum(tile_start[:, None] >= ends[None, :], axis=1), N_EXPERTS - 1)
    n_used = (ends[-1] // tm).reshape(1)
    tokens = jnp.broadcast_to(jnp.arange(n_tokens, dtype=jnp.int32)[:, None], pos.shape)
    src_rows = jnp.zeros((n_tiles * tm,), jnp.int32).at[pos.reshape(-1)].set(tokens.reshape(-1))
    return pos, src_rows.reshape(n_tiles, 1, tm), tile_expert.astype(jnp.int32), n_used.astype(jnp.int32)


def kernel(x, rel_bias_table, even_w_in, gla_gate_up, gla_gate_bias, gla_norm_gain, diff_lambda, diff_norm_gain, even_w_out, ffn_w_gate, ffn_w_up, ffn_w_down, odd_w_in, mla_q_norm_gain, mla_kv_norm_gain, mla_w_uq, mla_w_ukv, odd_w_out, router_w, moe_w_gate, moe_w_up, moe_w_down, ln_gain, ln_bias):
    b, s, d = x.shape
    n = b * s
    tm = min(512, s)
    assert s % tm == 0 and tm % ATT_TQ == 0 and s % GLA_GROUP == 0
    x2 = x.reshape(n, d)
    row = lambda v: v.reshape(1, -1)

    w_main, w_vdt = _even_in_weight(even_w_in[0])
    proj, vd_t = _in_proj(x2, w_main, w_vdt, s, tm)
    proj = proj.reshape(b, s, EVEN_COLS)
    gup = jnp.zeros((2, LANES, GLA_HEADS * GLA_DK), F32)
    for dd in range(2):
        gup = gup.at[dd, dd * GLA_RANK:(dd + 1) * GLA_RANK].set(gla_gate_up[0, dd])
    o_gla = _gla(proj, gup.astype(BF16), gla_gate_bias[0][:, None, :], row(gla_norm_gain[0]))
    lam_init = 0.8 - 0.6 * math.exp(-0.3 * 0)
    lf = diff_lambda[0]
    lam = (jnp.exp(jnp.sum(lf[0] * lf[1])) - jnp.exp(jnp.sum(lf[2] * lf[3])) + lam_init).reshape(1)
    o_diff_t = _diff_attention(proj, vd_t, _bias_band(rel_bias_table), lam,
                               diff_norm_gain[0].reshape(-1, 1), lam_init)
    hv = GLA_HEADS * GLA_DV
    w_out = even_w_out[0].astype(BF16)
    x2 = _outproj_ln(o_gla.reshape(n, hv), o_diff_t, w_out[:hv], w_out[hv:], x2,
                     row(ln_gain[0, 0]), row(ln_bias[0, 0]), s, tm)
    x2 = _ffn_ln(x2, ffn_w_gate[0].astype(BF16), ffn_w_up[0].astype(BF16), ffn_w_down[0].astype(BF16),
                 row(ln_gain[0, 1]), row(ln_bias[0, 1]), tm)

    win, wq1, wq2, wk, wv = _mla_weights(odd_w_in[0], mla_w_uq[0], mla_w_ukv[0])
    q, k, v = _mla_proj(x2, win, row(mla_q_norm_gain[0]), row(mla_kv_norm_gain[0]), wq1, wq2, wk, wv,
                        _rope_tables(s), s, tm)
    o = _mla_attention(q.reshape(b, s, -1), k.reshape(b, s, -1), v.reshape(b, s, -1), min(MLA_TQ, s))
    rw = jnp.concatenate([router_w[0], jnp.zeros((d, LANES - N_EXPERTS), F32)], axis=1)
    rw_hi = rw.astype(BF16)
    rw2 = jnp.stack([rw_hi, (rw - rw_hi.astype(F32)).astype(BF16)])
    xn, info, counts = _outproj_route(o.reshape(n, -1), odd_w_out[0].astype(BF16), x2,
                                      row(ln_gain[1, 0]), row(ln_bias[1, 0]), rw2, s, tm)
    pos, src_rows, tile_expert, n_used = _routing_plan(info, counts, n)
    pos_t = pos.reshape(n // tm, tm, TOP_K).transpose(0, 2, 1)
    y_sorted = _moe_ffn(xn, src_rows, tile_expert, n_used, moe_w_gate[0].astype(BF16),
                        moe_w_up[0].astype(BF16), moe_w_down[0].astype(BF16))
    out = _combine_ln(xn, info, pos_t, y_sorted, row(ln_gain[1, 1]), row(ln_bias[1, 1]), tm)
    return out.reshape(b, s, d)
```

```python
import functools
import math

import jax
import jax.numpy as jnp
from jax import lax
from jax.experimental import pallas as pl
from jax.experimental.pallas import tpu as pltpu

F32 = jnp.float32
BF16 = jnp.bfloat16

LANES = 128
V7X_VMEM_BYTES = 64 * 1024 * 1024
VMEM_LIMIT = V7X_VMEM_BYTES - 8 * 1024 * 1024

D_MODEL = 1024
GLA_HEADS, GLA_DK, GLA_DV = 4, 64, 128
GLA_RANK, GLA_TEMP, GLA_CHUNK = 16, 16.0, 64
DIFF_HEADS, DIFF_DK, DIFF_DV = 4, 64, 128
N_BUCKETS, MAX_DISTANCE = 32, 128
MLA_HEADS, MLA_Q_LORA, MLA_KV_LORA = 16, 256, 128
MLA_NOPE, MLA_ROPE, MLA_DV = 64, 32, 64
ROPE_THETA = 10000.0
D_FF, N_EXPERTS, TOP_K = 2816, 8, 2
DEPTH = 2
ALPHA = (2 * DEPTH) ** 0.25
LN_EPS, RMS_EPS = 1e-5, 1e-6

COL_QK_G, COL_V_G, COL_G_G = 0, 512, 1024
COL_Q_D, COL_K_D, COL_ALR = 1536, 2048, 2560
EVEN_COLS = 2688

ATT_KC = 128
ATT_TQ = 256
MLA_TQ = 2048
COL_REDUCE_SLAB = 128
LOG2E = math.log2(math.e)

GLA_GROUP = 256
FF_CHUNK = 256
MOE_TILE = 512
DMA_UNROLL = 8


def _cparams(sem):
    return pltpu.CompilerParams(dimension_semantics=sem, vmem_limit_bytes=VMEM_LIMIT)


def _layer_norm(y, g, b):
    mu = jnp.mean(y, axis=-1, keepdims=True)
    d = y - mu
    var = jnp.mean(d * d, axis=-1, keepdims=True)
    return d * lax.rsqrt(var + LN_EPS) * g + b


def _split3(x):
    h1 = x.astype(BF16)
    r1 = x - h1.astype(F32)
    h2 = r1.astype(BF16)
    h3 = (r1 - h2.astype(F32)).astype(BF16)
    return h1, h2, h3


def _dot(a, b):
    return jnp.dot(a, b, preferred_element_type=F32)


def _dot_nt(a, b):
    return lax.dot_general(a, b, (((1,), (1,)), ((), ())), preferred_element_type=F32)


def _dot_tn(a, b):
    return lax.dot_general(a, b, (((0,), (0,)), ((), ())), preferred_element_type=F32)


def _in_proj_kernel(x_ref, w_ref, wvt_ref, o_ref, vt_ref):
    xb = x_ref[...].astype(BF16)
    o_ref[...] = _dot(xb, w_ref[...]).astype(o_ref.dtype)
    vt_ref[...] = _dot_nt(wvt_ref[...], xb).astype(vt_ref.dtype)


def _in_proj(x, w, wvt, seq, tm):
    n, k = x.shape
    m = w.shape[1]
    dvt = wvt.shape[0]
    per_seq = seq // tm
    return pl.pallas_call(
        _in_proj_kernel,
        grid=(n // tm,),
        in_specs=[pl.BlockSpec((tm, k), lambda i: (i, 0)),
                  pl.BlockSpec((k, m), lambda i: (0, 0)),
                  pl.BlockSpec((dvt, k), lambda i: (0, 0))],
        out_specs=[pl.BlockSpec((tm, m), lambda i: (i, 0)),
                   pl.BlockSpec((None, dvt, tm), lambda i: (i // per_seq, 0, i % per_seq))],
        out_shape=[jax.ShapeDtypeStruct((n, m), BF16),
                   jax.ShapeDtypeStruct((n // seq, dvt, seq), BF16)],
        compiler_params=_cparams(("parallel",)),
        name="in_proj",
    )(x, w, wvt)


def _gla_kernel(qk_ref, v_ref, g_ref, alr_ref, gup_ref, gbias_ref, gain_ref, o_ref,
                la_ref, qdec_ref, kdec_ref, decay_ref, acc_ref, st_ref):
    seq = qk_ref.shape[0]
    hk = GLA_HEADS * GLA_DK
    hv = GLA_HEADS * GLA_DV
    grp = GLA_GROUP
    c = GLA_CHUNK

    alr = alr_ref[...]
    for d in range(2):
        logits = _dot(alr, gup_ref[d]) + gbias_ref[d]
        log_sig = jnp.minimum(logits, 0.0) - jnp.log(1.0 + jnp.exp(-jnp.abs(logits)))
        la_ref[d] = log_sig / GLA_TEMP

    row = lax.broadcasted_iota(jnp.int32, (grp, grp), 0)
    col = lax.broadcasted_iota(jnp.int32, (grp, grp), 1)
    same = (row // c) == (col // c)
    tri = [same & (row >= col), same & (row <= col)]
    cum_inc = [jnp.where(t, 1.0, 0.0).astype(BF16) for t in tri]
    cum_rem = [jnp.where(same & (row < col), 1.0, 0.0).astype(BF16),
               jnp.where(same & (row > col), 1.0, 0.0).astype(BF16)]
    lane_k = lax.broadcasted_iota(jnp.int32, (grp, hk), 1) // GLA_DK
    scale = GLA_DK ** -0.5

    def group_body(r, carry):
        rows = pl.ds(pl.multiple_of(r * grp, grp), grp)
        q = qk_ref[rows, 0:hk].astype(F32)
        k = qk_ref[rows, hk:2 * hk].astype(F32)
        v = v_ref[rows, :]
        o_heads = [jnp.zeros((grp, GLA_DV), F32) for _ in range(GLA_HEADS)]
        for d in range(2):
            parts = _split3(la_ref[d, rows, :])
            bcum = sum(_dot(cum_inc[d], p) for p in parts)
            brem = sum(_dot(cum_rem[d], p) for p in parts)
            q_dec = q * jnp.exp(bcum) * scale
            k_inv = (k * jnp.exp(-bcum)).astype(BF16)
            qdec_ref[d, rows, :] = q_dec.astype(BF16)
            kdec_ref[d, rows, :] = (k * jnp.exp(brem)).astype(BF16)
            decay_ref[d, rows, :] = jnp.exp(bcum + brem)
            for h in range(GLA_HEADS):
                q_h = jnp.where(lane_k == h, q_dec, 0.0).astype(BF16)
                attn = jnp.where(tri[d], _dot_nt(q_h, k_inv), 0.0)
                o_heads[h] = o_heads[h] + _dot(attn.astype(BF16), v[:, h * GLA_DV:(h + 1) * GLA_DV])
        for h in range(GLA_HEADS):
            acc_ref[rows, h * GLA_DV:(h + 1) * GLA_DV] = o_heads[h]
        return carry

    lax.fori_loop(0, seq // grp, group_body, 0)

    srow = lax.broadcasted_iota(jnp.int32, (hv, hk), 0) // GLA_DV
    scol = lax.broadcasted_iota(jnp.int32, (hv, hk), 1) // GLA_DK
    head_mask = srow == scol
    n_chunks = seq // c
    st_ref[...] = jnp.zeros_like(st_ref)

    def chunk_body(i, carry):
        for d in range(2):
            ci = i if d == 0 else n_chunks - 1 - i
            rows = pl.ds(pl.multiple_of(ci * c, c), c)
            state = st_ref[d]
            acc_ref[rows, :] += _dot_nt(qdec_ref[d, rows, :], state.astype(BF16))
            kv = _dot_tn(v_ref[rows, :], kdec_ref[d, rows, :])
            decay = decay_ref[d, pl.ds(pl.multiple_of(ci * c, c), 1), :]
            st_ref[d] = state * decay + jnp.where(head_mask, kv, 0.0)
        return carry

    lax.fori_loop(0, n_chunks, chunk_body, 0, unroll=2)

    gain = gain_ref[...]
    for h in range(GLA_HEADS):
        sl = slice(h * GLA_DV, (h + 1) * GLA_DV)
        o = acc_ref[:, sl]
        y = o * lax.rsqrt(jnp.mean(o * o, axis=-1, keepdims=True) + RMS_EPS) * gain
        gate = g_ref[:, sl].astype(F32)
        o_ref[:, sl] = (y * (gate * jax.nn.sigmoid(gate))).astype(o_ref.dtype)


def _gla(proj, gup, gbias, gain):
    b, s, _ = proj.shape
    hk, hv = GLA_HEADS * GLA_DK, GLA_HEADS * GLA_DV
    blk = lambda width, col: pl.BlockSpec((None, s, width), lambda i: (i, 0, col // width))
    return pl.pallas_call(
        _gla_kernel,
        grid=(b,),
        in_specs=[blk(2 * hk, COL_QK_G), blk(hv, COL_V_G), blk(hv, COL_G_G), blk(LANES, COL_ALR),
                  pl.BlockSpec((2, LANES, hk), lambda i: (0, 0, 0)),
                  pl.BlockSpec((2, 1, hk), lambda i: (0, 0, 0)),
                  pl.BlockSpec((1, GLA_DV), lambda i: (0, 0))],
        out_specs=pl.BlockSpec((None, s, hv), lambda i: (i, 0, 0)),
        out_shape=jax.ShapeDtypeStruct((b, s, hv), BF16),
        scratch_shapes=[pltpu.VMEM((2, s, hk), F32),
                        pltpu.VMEM((2, s, hk), BF16),
                        pltpu.VMEM((2, s, hk), BF16),
                        pltpu.VMEM((2, s, hk), F32),
                        pltpu.VMEM((s, hv), F32),
                        pltpu.VMEM((2, hv, hk), F32)],
        compiler_params=_cparams(("parallel",)),
        name="gla",
    )(proj, proj, proj, proj, gup, gbias, gain)


J_BELOW = (-MAX_DISTANCE - ATT_KC + 1) // ATT_KC
J_ABOVE = -(-(MAX_DISTANCE + ATT_TQ - 1) // ATT_KC)
N_BAND = J_ABOVE - J_BELOW + 1
assert ATT_KC * J_BELOW + ATT_KC - 1 <= -MAX_DISTANCE
assert ATT_KC * J_ABOVE - (ATT_TQ - 1) >= MAX_DISTANCE


def _band_buckets():
    half = N_BUCKETS // 2
    max_exact = half // 2
    j = (jnp.arange(N_BAND, dtype=jnp.int32) + J_BELOW)[:, None, None]
    rel = (j * ATT_KC + jnp.arange(ATT_KC, dtype=jnp.int32)[None, :, None]
           - jnp.arange(ATT_TQ, dtype=jnp.int32)[None, None, :])
    bucket = jnp.where(rel > 0, half, 0).astype(jnp.int32)
    n = jnp.abs(rel)
    n_large = max_exact + (jnp.log(jnp.maximum(n, max_exact).astype(F32) / max_exact)
                           / math.log(MAX_DISTANCE / max_exact) * (half - max_exact)).astype(jnp.int32)
    n_large = jnp.minimum(n_large, half - 1)
    return bucket + jnp.where(n < max_exact, n, n_large)


def _bias_band(table):
    return jnp.moveaxis(table[_band_buckets()], -1, 0).astype(F32) * LOG2E


def _diff_kernel(lam_ref, q_ref, k_ref, vt_ref, band_ref, gain_ref, o_ref, *, out_scale):
    seq = q_ref.shape[0]
    n_chunks = seq // ATT_KC
    lane = lax.broadcasted_iota(jnp.int32, (ATT_TQ, 2 * DIFF_DK), 1)
    scale = DIFF_DK ** -0.5 * LOG2E
    items = [(r, mp) for r in range(seq // ATT_TQ) for mp in range(2)]
    first_map = {}

    def scores(item):
        r, mp = item
        q = (q_ref[r * ATT_TQ:(r + 1) * ATT_TQ, :].astype(F32) * scale).astype(BF16)
        q = jnp.where((lane >= DIFF_DK) == (mp == 1), q, jnp.zeros_like(q))
        tiles = [band_ref[min(max(c - (ATT_TQ // ATT_KC) * r - J_BELOW, 0), N_BAND - 1)] for c in range(n_chunks)]
        return _dot_nt(k_ref[...], q) + jnp.concatenate(tiles, axis=0)

    def finish(item, s_t):
        r, mp = item
        o_t = _softmax_pv_t(s_t, vt_ref[...])
        if mp == 0:
            first_map[r] = o_t
            return
        o = first_map.pop(r) - lam_ref[0] * o_t
        y = o * lax.rsqrt(jnp.mean(o * o, axis=0, keepdims=True) + RMS_EPS) * gain_ref[...]
        o_ref[r] = (y * out_scale).astype(o_ref.dtype)

    _pipelined(items, scores, finish)


def _diff_attention(proj, vt, band, lam, gain_col, lam_init):
    b, s, _ = proj.shape
    h = DIFF_HEADS
    w = 2 * DIFF_DK
    assert w == LANES
    return pl.pallas_call(
        functools.partial(_diff_kernel, out_scale=1.0 - lam_init),
        grid=(h, b),
        in_specs=[pl.BlockSpec(memory_space=pltpu.SMEM),
                  pl.BlockSpec((None, s, w), lambda hh, bb: (bb, 0, COL_Q_D // w + hh)),
                  pl.BlockSpec((None, s, w), lambda hh, bb: (bb, 0, COL_K_D // w + hh)),
                  pl.BlockSpec((None, DIFF_DV, s), lambda hh, bb: (bb, hh, 0)),
                  pl.BlockSpec((None, N_BAND, ATT_KC, ATT_TQ), lambda hh, bb: (hh, 0, 0, 0)),
                  pl.BlockSpec((DIFF_DV, 1), lambda hh, bb: (0, 0))],
        out_specs=pl.BlockSpec((None, s // ATT_TQ, DIFF_DV, ATT_TQ), lambda hh, bb: (bb, 0, hh, 0)),
        out_shape=jax.ShapeDtypeStruct((b, s // ATT_TQ, h * DIFF_DV, ATT_TQ), BF16),
        compiler_params=_cparams(("parallel", "parallel")),
        name="diff_attn",
    )(lam, proj, proj, vt, band, gain_col)


def _residual_ln(x_ref, o_ref, g_ref, b_ref, row_pairs, col_pairs):
    for j in range(x_ref.shape[0] // ATT_TQ):
        rows = slice(j * ATT_TQ, (j + 1) * ATT_TQ)
        h = ALPHA * x_ref[rows, :]
        for a_ref, w_ref in row_pairs:
            h = h + _dot(a_ref[rows, :], w_ref[...])
        for a_ref, w_ref in col_pairs:
            h = h + _dot_tn(a_ref[j], w_ref[...])
        o_ref[rows, :] = _layer_norm(h, g_ref[...], b_ref[...])


def _outproj_kernel(a_ref, at_ref, wa_ref, wt_ref, x_ref, g_ref, b_ref, o_ref):
    _residual_ln(x_ref, o_ref, g_ref, b_ref, [(a_ref, wa_ref)], [(at_ref, wt_ref)])


def _tile_specs(n, d, seq, tm):
    per_seq = seq // tm
    rows = lambda width: pl.BlockSpec((tm, width), lambda i: (i, 0))
    cols = lambda k: pl.BlockSpec((None, tm // ATT_TQ, k, ATT_TQ), lambda i: (i // per_seq, i % per_seq, 0, 0))
    full = lambda a: pl.BlockSpec(a.shape, lambda i: (0,) * a.ndim)
    return rows, cols, full


def _outproj_ln(a, at, wa, wt, x, g, b, seq, tm):
    n, d = x.shape
    rows, cols, full = _tile_specs(n, d, seq, tm)
    return pl.pallas_call(
        _outproj_kernel,
        grid=(n // tm,),
        in_specs=[rows(a.shape[1]), cols(at.shape[2]), full(wa), full(wt), rows(d), full(g), full(b)],
        out_specs=rows(d),
        out_shape=jax.ShapeDtypeStruct((n, d), F32),
        compiler_params=_cparams(("parallel",)),
        name="outproj_ln",
    )(a, at, wa, wt, x, g, b)


def _outproj_route_kernel(at_ref, w_ref, x_ref, g_ref, b_ref, rw_ref, o_ref, info_ref, cnt_ref, carry_ref):
    i = pl.program_id(0)
    tm = x_ref.shape[0]

    @pl.when(i == 0)
    def _():
        carry_ref[...] = jnp.zeros_like(carry_ref)

    _residual_ln(x_ref, o_ref, g_ref, b_ref, [], [(at_ref, w_ref)])
    xn = o_ref[...]

    x_hi = xn.astype(BF16)
    x_lo = (xn - x_hi.astype(F32)).astype(BF16)
    logits = _dot(x_hi, rw_ref[0]) + (_dot(x_hi, rw_ref[1]) + _dot(x_lo, rw_ref[0]))
    lane = lax.broadcasted_iota(jnp.int32, logits.shape, 1).astype(F32)
    neg = jnp.float32(-jnp.inf)
    logits = jnp.where(lane < N_EXPERTS, logits, neg)
    v1 = jnp.max(logits, axis=-1, keepdims=True)
    e1 = jnp.min(jnp.where(logits == v1, lane, float(LANES)), axis=-1, keepdims=True)
    rest = jnp.where(lane == e1, neg, logits)
    v2 = jnp.max(rest, axis=-1, keepdims=True)
    e2 = jnp.min(jnp.where(rest == v2, lane, float(LANES)), axis=-1, keepdims=True)
    t = jnp.exp(v2 - v1)
    w1 = 1.0 / (1.0 + t)
    w2 = t / (1.0 + t)

    onehot = jnp.where((lane == e1) | (lane == e2), 1.0, 0.0)
    row = lax.broadcasted_iota(jnp.int32, (tm, tm), 0)
    col = lax.broadcasted_iota(jnp.int32, (tm, tm), 1)
    before = jnp.where(row > col, 1.0, 0.0).astype(BF16)
    prior = carry_ref[...] + _dot(before, onehot.astype(BF16))
    r1 = jnp.sum(jnp.where(lane == e1, prior, 0.0), axis=-1, keepdims=True)
    r2 = jnp.sum(jnp.where(lane == e2, prior, 0.0), axis=-1, keepdims=True)
    carry_ref[...] += jnp.sum(onehot, axis=0, keepdims=True)
    cnt_ref[...] = jnp.broadcast_to(carry_ref[...], cnt_ref.shape)

    info = jnp.zeros(logits.shape, F32)
    for idx, val in enumerate((e1, e2, w1, w2, r1, r2)):
        info = jnp.where(lane == idx, val, info)
    info_ref[...] = info


def _outproj_route(at, w, x, g, b, router_w2, seq, tm):
    n, d = x.shape
    rows, cols, full = _tile_specs(n, d, seq, tm)
    return pl.pallas_call(
        _outproj_route_kernel,
        grid=(n // tm,),
        in_specs=[cols(at.shape[2]), full(w), rows(d), full(g), full(b), full(router_w2)],
        out_specs=[rows(d), rows(LANES),
                   pl.BlockSpec((8, LANES), lambda i: (0, 0))],
        out_shape=[jax.ShapeDtypeStruct((n, d), F32),
                   jax.ShapeDtypeStruct((n, LANES), F32),
                   jax.ShapeDtypeStruct((8, LANES), F32)],
        scratch_shapes=[pltpu.VMEM((1, LANES), F32)],
        compiler_params=_cparams(("arbitrary",)),
        name="outproj_route",
    )(at, w, x, g, b, router_w2)


def _swiglu_acc(xb, wg_ref, wu_ref, wd_ref, acc_ref):
    n_chunks = wg_ref.shape[-1] // FF_CHUNK
    for c in range(n_chunks):
        cols = slice(c * FF_CHUNK, (c + 1) * FF_CHUNK)
        gate = _dot(xb, wg_ref[:, cols])
        up = _dot(xb, wu_ref[:, cols])
        hidden = (gate * jax.nn.sigmoid(gate) * up).astype(BF16)
        part = _dot(hidden, wd_ref[cols, :])
        if c == 0:
            acc_ref[...] = part
        else:
            acc_ref[...] += part


def _ffn_ln_kernel(x_ref, wg_ref, wu_ref, wd_ref, g_ref, b_ref, o_ref):
    x = x_ref[...]
    _swiglu_acc(x.astype(BF16), wg_ref, wu_ref, wd_ref, o_ref)
    o_ref[...] = _layer_norm(ALPHA * x + o_ref[...], g_ref[...], b_ref[...])


def _ffn_ln(x, wg, wu, wd, g, b, tm):
    n, d = x.shape
    f = wg.shape[1]
    once = lambda shape: pl.BlockSpec(shape, lambda i: (0,) * len(shape), pipeline_mode=pl.Buffered(1))
    return pl.pallas_call(
        _ffn_ln_kernel,
        grid=(n // tm,),
        in_specs=[pl.BlockSpec((tm, d), lambda i: (i, 0)),
                  once((d, f)), once((d, f)), once((f, d)), once((1, d)), once((1, d))],
        out_specs=pl.BlockSpec((tm, d), lambda i: (i, 0)),
        out_shape=jax.ShapeDtypeStruct((n, d), F32),
        compiler_params=_cparams(("parallel",)),
        name="ffn_ln",
    )(x, wg, wu, wd, g, b)


def _moe_kernel(te_ref, nu_ref, x_ref, wg_ref, wu_ref, wd_ref, o_ref):
    t = pl.program_id(0)

    @pl.when(t < nu_ref[0])
    def _():
        _swiglu_acc(x_ref[...].astype(BF16), wg_ref, wu_ref, wd_ref, o_ref)

    @pl.when(t >= nu_ref[0])
    def _():
        o_ref[...] = jnp.zeros_like(o_ref)


def _moe_ffn(x_sorted, tile_expert, n_used, wg, wu, wd):
    p, d = x_sorted.shape
    f = wg.shape[2]
    tm = MOE_TILE
    grid_spec = pltpu.PrefetchScalarGridSpec(
        num_scalar_prefetch=2,
        grid=(p // tm,),
        in_specs=[pl.BlockSpec((tm, d), lambda t, te, nu: (jnp.minimum(t, nu[0] - 1), 0)),
                  pl.BlockSpec((None, d, f), lambda t, te, nu: (te[t], 0, 0)),
                  pl.BlockSpec((None, d, f), lambda t, te, nu: (te[t], 0, 0)),
                  pl.BlockSpec((None, f, d), lambda t, te, nu: (te[t], 0, 0))],
        out_specs=pl.BlockSpec((tm, d), lambda t, te, nu: (t, 0)),
    )
    return pl.pallas_call(
        _moe_kernel,
        grid_spec=grid_spec,
        out_shape=jax.ShapeDtypeStruct((p, d), F32),
        compiler_params=_cparams(("arbitrary",)),
        name="moe_ffn",
    )(tile_expert, n_used, x_sorted, wg, wu, wd)


def _scatter_kernel(pos_ref, pad_ref, x_ref, o_hbm, zero_ref, sem, zsem):
    i = pl.program_id(0)
    tm = x_ref.shape[0]
    n_tiles = o_hbm.shape[0] // MOE_TILE

    @pl.when(i == 0)
    def _():
        zero_ref[...] = jnp.zeros_like(zero_ref)
        zero_row = zero_ref.at[pl.ds(0, 1)]
        for e in range(N_EXPERTS):
            start, count = pad_ref[e], pad_ref[N_EXPERTS + e]

            def fill(r, carry):
                pltpu.make_async_copy(zero_row, o_hbm.at[pl.ds(start + r, 1)], zsem).start()
                return carry

            lax.fori_loop(0, count, fill, 0)

            def drain(r, carry):
                pltpu.make_async_copy(zero_row, o_hbm.at[pl.ds(start + r, 1)], zsem).wait()
                return carry

            lax.fori_loop(0, count, drain, 0)

        def fill_tile(t, carry):
            dst = o_hbm.at[pl.ds(pl.multiple_of(t * MOE_TILE, MOE_TILE), MOE_TILE)]
            cp = pltpu.make_async_copy(zero_ref, dst, zsem)
            cp.start()
            cp.wait()
            return carry

        lax.fori_loop(pad_ref[2 * N_EXPERTS], n_tiles, fill_tile, 0)

    def issue(r, carry):
        for kk in range(TOP_K):
            dst = pos_ref[0, kk, r]
            pltpu.make_async_copy(x_ref.at[pl.ds(r, 1)], o_hbm.at[pl.ds(dst, 1)], sem).start()
        return carry

    lax.fori_loop(0, tm, issue, 0, unroll=DMA_UNROLL)

    for kk in range(TOP_K):
        pltpu.make_async_copy(x_ref, o_hbm.at[pl.ds(0, tm)], sem).wait()


def _scatter_rows(x, pos, pad_info, n_slots, tm):
    n, d = x.shape
    return pl.pallas_call(
        _scatter_kernel,
        grid=(n // tm,),
        in_specs=[pl.BlockSpec((1, TOP_K, tm), lambda i: (i, 0, 0), memory_space=pltpu.SMEM),
                  pl.BlockSpec(memory_space=pltpu.SMEM),
                  pl.BlockSpec((tm, d), lambda i: (i, 0))],
        out_specs=pl.BlockSpec(memory_space=pl.ANY),
        out_shape=jax.ShapeDtypeStruct((n_slots, d), F32),
        scratch_shapes=[pltpu.VMEM((MOE_TILE, d), F32),
                        pltpu.SemaphoreType.DMA(()),
                        pltpu.SemaphoreType.DMA(())],
        compiler_params=_cparams(("arbitrary",)),
        name="scatter_rows",
    )(pos, pad_info, x)


def _combine_kernel(pos_ref, x_ref, info_ref, g_ref, b_ref, y_hbm, o_ref, buf_ref, sem):
    tm = x_ref.shape[0]

    def issue(r, carry):
        for kk in range(TOP_K):
            src = pos_ref[0, kk, r]
            pltpu.make_async_copy(y_hbm.at[pl.ds(src, 1)], buf_ref.at[kk, pl.ds(r, 1)], sem).start()
        return carry

    lax.fori_loop(0, tm, issue, 0, unroll=DMA_UNROLL)

    for kk in range(TOP_K):
        pltpu.make_async_copy(y_hbm.at[pl.ds(0, tm)], buf_ref.at[kk], sem).wait()

    info = info_ref[...]
    w1 = info[:, 2:3]
    w2 = info[:, 3:4]
    f = w1 * buf_ref[0] + w2 * buf_ref[1]
    o_ref[...] = _layer_norm(ALPHA * x_ref[...] + f, g_ref[...], b_ref[...])


def _combine_ln(x, info, pos, y_sorted, g, b, tm):
    n, d = x.shape
    return pl.pallas_call(
        _combine_kernel,
        grid=(n // tm,),
        in_specs=[pl.BlockSpec((1, TOP_K, tm), lambda i: (i, 0, 0), memory_space=pltpu.SMEM),
                  pl.BlockSpec((tm, d), lambda i: (i, 0)),
                  pl.BlockSpec((tm, LANES), lambda i: (i, 0)),
                  pl.BlockSpec((1, d), lambda i: (0, 0)),
                  pl.BlockSpec((1, d), lambda i: (0, 0)),
                  pl.BlockSpec(memory_space=pl.ANY)],
        out_specs=pl.BlockSpec((tm, d), lambda i: (i, 0)),
        out_shape=jax.ShapeDtypeStruct((n, d), F32),
        scratch_shapes=[pltpu.VMEM((TOP_K, tm, d), F32),
                        pltpu.SemaphoreType.DMA(())],
        compiler_params=_cparams(("arbitrary",)),
        name="combine_ln",
    )(pos, x, info, g, b, y_sorted)


def _mla_proj_kernel(x_ref, win_ref, qg_ref, kvg_ref, wq1_ref, wq2_ref, wk_ref, wvt_ref,
                     cq_ref, sq_ref, ck_ref, sk_ref, q_ref, k_ref, vt_ref):
    c = _dot(x_ref[...].astype(BF16), win_ref[...])
    cq = c[:, :MLA_Q_LORA]
    ckv = c[:, MLA_Q_LORA:MLA_Q_LORA + MLA_KV_LORA]
    off = MLA_Q_LORA + MLA_KV_LORA
    k_rope = c[:, off:off + LANES] * ck_ref[...] + c[:, off + LANES:off + 2 * LANES] * sk_ref[...]
    cq = (cq * lax.rsqrt(jnp.mean(cq * cq, axis=-1, keepdims=True) + RMS_EPS) * qg_ref[...]).astype(BF16)
    ckv = (ckv * lax.rsqrt(jnp.mean(ckv * ckv, axis=-1, keepdims=True) + RMS_EPS) * kvg_ref[...]).astype(BF16)
    q1 = _dot(cq, wq1_ref[...])
    q2 = _dot(cq, wq2_ref[...])
    k1 = _dot(ckv, wk_ref[...])
    vt_ref[...] = _dot_nt(wvt_ref[...], ckv).astype(vt_ref.dtype)
    cos_q, sin_q = cq_ref[...], sq_ref[...]
    for h in range(MLA_HEADS):
        sl = slice(h * LANES, (h + 1) * LANES)
        q_ref[:, sl] = (q1[:, sl] * cos_q + q2[:, sl] * sin_q).astype(q_ref.dtype)
        k_ref[:, sl] = (k1[:, sl] + k_rope).astype(k_ref.dtype)


def _mla_proj(x, win, qg, kvg, wq1, wq2, wk, wvt, tabs, seq, tm):
    n, d = x.shape
    per_seq = seq // tm
    full = lambda a: pl.BlockSpec(a.shape, lambda i: (0,) * a.ndim)
    tab = pl.BlockSpec((tm, LANES), lambda i: (i % per_seq, 0))
    hw = MLA_HEADS * LANES
    dvt = wvt.shape[0]
    return pl.pallas_call(
        _mla_proj_kernel,
        grid=(n // tm,),
        in_specs=[pl.BlockSpec((tm, d), lambda i: (i, 0)), full(win), full(qg), full(kvg),
                  full(wq1), full(wq2), full(wk), full(wvt), tab, tab, tab, tab],
        out_specs=[pl.BlockSpec((tm, hw), lambda i: (i, 0)),
                   pl.BlockSpec((tm, hw), lambda i: (i, 0)),
                   pl.BlockSpec((None, dvt, tm), lambda i: (i // per_seq, 0, i % per_seq))],
        out_shape=[jax.ShapeDtypeStruct((n, hw), BF16),
                   jax.ShapeDtypeStruct((n, hw), BF16),
                   jax.ShapeDtypeStruct((n // seq, dvt, seq), BF16)],
        compiler_params=_cparams(("parallel",)),
        name="mla_proj",
    )(x, win, qg, kvg, wq1, wq2, wk, wvt, *tabs)


def _softmax_pv_t(s_t, vt):
    p = jnp.exp2(s_t - _col_reduce(s_t, jnp.max))
    l = _col_reduce(p, jnp.sum)
    return _dot(vt, p.astype(BF16)) * (1.0 / l)


def _col_reduce(x, op):
    rows, n = x.shape
    slab = COL_REDUCE_SLAB if rows % COL_REDUCE_SLAB == 0 else rows
    return op(op(x.reshape(rows // slab, slab, n), axis=0), axis=0, keepdims=True)


def _pipelined(items, scores, finish):
    s_next = scores(items[0])
    for idx, item in enumerate(items):
        s = s_next
        if idx + 1 < len(items):
            s_next = scores(items[idx + 1])
        finish(item, s)


def _mla_attn_kernel(q_ref, k_ref, vt_ref, o_ref):
    n_sub = q_ref.shape[0] // ATT_TQ
    items = [(r, hh) for r in range(n_sub) for hh in range(2)]
    cols = lambda hh: slice(hh * LANES, (hh + 1) * LANES)

    def scores(item):
        r, hh = item
        return _dot_nt(k_ref[:, cols(hh)], q_ref[r * ATT_TQ:(r + 1) * ATT_TQ, cols(hh)])

    def finish(item, s_t):
        r, hh = item
        vrows = slice(hh * MLA_DV, (hh + 1) * MLA_DV)
        o_ref[r, vrows, :] = _softmax_pv_t(s_t, vt_ref[vrows, :]).astype(o_ref.dtype)

    _pipelined(items, scores, finish)


def _mla_attention(q, k, vt, tq):
    b, s, _ = q.shape
    pairs = MLA_HEADS // 2
    pair_dv = 2 * MLA_DV
    return pl.pallas_call(
        _mla_attn_kernel,
        grid=(b, pairs, s // tq),
        in_specs=[pl.BlockSpec((None, tq, 2 * LANES), lambda bb, p, i: (bb, i, p)),
                  pl.BlockSpec((None, s, 2 * LANES), lambda bb, p, i: (bb, 0, p)),
                  pl.BlockSpec((None, pair_dv, s), lambda bb, p, i: (bb, p, 0))],
        out_specs=pl.BlockSpec((None, tq // ATT_TQ, pair_dv, ATT_TQ), lambda bb, p, i: (bb, i, p, 0)),
        out_shape=jax.ShapeDtypeStruct((b, s // ATT_TQ, MLA_HEADS * MLA_DV, ATT_TQ), BF16),
        compiler_params=_cparams(("parallel", "parallel", "parallel")),
        name="mla_attn",
    )(q, k, vt)


def _even_in_weight(w):
    hk, hv = GLA_HEADS * GLA_DK, GLA_HEADS * GLA_DV
    widths = (hk, hk, hv, hv, 2 * GLA_RANK, DIFF_HEADS * 2 * DIFF_DK, DIFF_HEADS * 2 * DIFF_DK,
              DIFF_HEADS * DIFF_DV)
    offs = [0]
    for wd_ in widths:
        offs.append(offs[-1] + wd_)
    piece = lambda j: w[:, offs[j]:offs[j + 1]]
    pad = jnp.zeros((w.shape[0], EVEN_COLS - COL_ALR - 2 * GLA_RANK), w.dtype)
    main = jnp.concatenate([piece(0), piece(1), piece(2), piece(3), piece(5), piece(6), piece(4), pad], axis=1)
    return main.astype(BF16), piece(7).T.astype(BF16)


def _rot_half_cols(w):
    half = MLA_ROPE // 2
    shp = w.shape
    g = w.reshape(shp[0], -1, MLA_ROPE)
    return jnp.concatenate([-g[..., half:], g[..., :half]], axis=-1).reshape(shp)


def _mla_weights(w_in, w_uq, w_ukv):
    d = w_in.shape[0]
    dq = MLA_NOPE + MLA_ROPE
    z = lambda rows, cols: jnp.zeros((rows, cols), F32)
    w_kr = w_in[:, MLA_Q_LORA + MLA_KV_LORA:]
    kr_blk = lambda m: jnp.concatenate([z(d, MLA_NOPE), m, z(d, LANES - dq)], axis=1)
    win = jnp.concatenate([w_in[:, :MLA_Q_LORA + MLA_KV_LORA], kr_blk(w_kr), kr_blk(_rot_half_cols(w_kr))],
                          axis=1).astype(BF16)
    uq = w_uq.reshape(MLA_Q_LORA, MLA_HEADS, dq)
    pad_q = jnp.zeros((MLA_Q_LORA, MLA_HEADS, LANES - dq), F32)
    wq1 = jnp.concatenate([uq, pad_q], axis=-1).reshape(MLA_Q_LORA, -1).astype(BF16)
    rot = _rot_half_cols(uq[..., MLA_NOPE:].reshape(MLA_Q_LORA, -1)).reshape(MLA_Q_LORA, MLA_HEADS, MLA_ROPE)
    wq2 = jnp.concatenate([jnp.zeros((MLA_Q_LORA, MLA_HEADS, MLA_NOPE), F32), rot, pad_q],
                          axis=-1).reshape(MLA_Q_LORA, -1).astype(BF16)
    ukv = w_ukv.reshape(MLA_KV_LORA, MLA_HEADS, MLA_NOPE + MLA_DV)
    wk = jnp.concatenate([ukv[..., :MLA_NOPE], jnp.zeros((MLA_KV_LORA, MLA_HEADS, LANES - MLA_NOPE), F32)],
                         axis=-1).reshape(MLA_KV_LORA, -1).astype(BF16)
    wvt = ukv[..., MLA_NOPE:].reshape(MLA_KV_LORA, -1).T.astype(BF16)
    return win, wq1, wq2, wk, wvt


def _rope_tables(seq):
    half = MLA_ROPE // 2
    inv = ROPE_THETA ** (-jnp.arange(half, dtype=F32) / half)
    ang = jnp.arange(seq, dtype=F32)[:, None] * inv[None, :]
    cos = jnp.concatenate([jnp.cos(ang), jnp.cos(ang)], axis=1)
    sin = jnp.concatenate([jnp.sin(ang), jnp.sin(ang)], axis=1)
    dq = MLA_NOPE + MLA_ROPE
    scale = dq ** -0.5 * LOG2E
    lay = lambda a, fill: jnp.concatenate([jnp.full((seq, MLA_NOPE), fill, F32), a,
                                           jnp.zeros((seq, LANES - dq), F32)], axis=1)
    return lay(cos, 1.0) * scale, lay(sin, 0.0) * scale, lay(cos, 0.0), lay(sin, 0.0)


def _routing_plan(info, counts, n_tokens):
    tm = MOE_TILE
    cnt = counts[0, :N_EXPERTS].astype(jnp.int32)
    padded = ((cnt + tm - 1) // tm) * tm
    ends = jnp.cumsum(padded)
    starts = ends - padded
    e = info[:, 0:TOP_K].astype(jnp.int32)
    rank = info[:, 4:4 + TOP_K].astype(jnp.int32)
    pos = starts[e] + rank
    n_tiles = (n_tokens * TOP_K) // tm + N_EXPERTS
    tile_start = jnp.arange(n_tiles, dtype=jnp.int32) * tm
    tile_expert = jnp.minimum(jnp.sum(tile_start[:, None] >= ends[None, :], axis=1), N_EXPERTS - 1)
    n_used = (ends[-1] // tm).reshape(1)
    pad_info = jnp.concatenate([starts + cnt, padded - cnt, n_used])
    return pos, tile_expert.astype(jnp.int32), n_used.astype(jnp.int32), pad_info.astype(jnp.int32), n_tiles * tm


def kernel(x, rel_bias_table, even_w_in, gla_gate_up, gla_gate_bias, gla_norm_gain, diff_lambda, diff_norm_gain, even_w_out, ffn_w_gate, ffn_w_up, ffn_w_down, odd_w_in, mla_q_norm_gain, mla_kv_norm_gain, mla_w_uq, mla_w_ukv, odd_w_out, router_w, moe_w_gate, moe_w_up, moe_w_down, ln_gain, ln_bias):
    b, s, d = x.shape
    n = b * s
    tm = min(512, s)
    assert s % tm == 0 and tm % ATT_TQ == 0 and s % GLA_GROUP == 0
    x2 = x.reshape(n, d)
    row = lambda v: v.reshape(1, -1)

    w_main, w_vdt = _even_in_weight(even_w_in[0])
    proj, vd_t = _in_proj(x2, w_main, w_vdt, s, tm)
    proj = proj.reshape(b, s, EVEN_COLS)
    gup = jnp.zeros((2, LANES, GLA_HEADS * GLA_DK), F32)
    for dd in range(2):
        gup = gup.at[dd, dd * GLA_RANK:(dd + 1) * GLA_RANK].set(gla_gate_up[0, dd])
    o_gla = _gla(proj, gup.astype(BF16), gla_gate_bias[0][:, None, :], row(gla_norm_gain[0]))
    lam_init = 0.8 - 0.6 * math.exp(-0.3 * 0)
    lf = diff_lambda[0]
    lam = (jnp.exp(jnp.sum(lf[0] * lf[1])) - jnp.exp(jnp.sum(lf[2] * lf[3])) + lam_init).reshape(1)
    o_diff_t = _diff_attention(proj, vd_t, _bias_band(rel_bias_table), lam,
                               diff_norm_gain[0].reshape(-1, 1), lam_init)
    hv = GLA_HEADS * GLA_DV
    w_out = even_w_out[0].astype(BF16)
    x2 = _outproj_ln(o_gla.reshape(n, hv), o_diff_t, w_out[:hv], w_out[hv:], x2,
                     row(ln_gain[0, 0]), row(ln_bias[0, 0]), s, tm)
    x2 = _ffn_ln(x2, ffn_w_gate[0].astype(BF16), ffn_w_up[0].astype(BF16), ffn_w_down[0].astype(BF16),
                 row(ln_gain[0, 1]), row(ln_bias[0, 1]), tm)

    win, wq1, wq2, wk, wvt = _mla_weights(odd_w_in[0], mla_w_uq[0], mla_w_ukv[0])
    q, k, v_t = _mla_proj(x2, win, row(mla_q_norm_gain[0]), row(mla_kv_norm_gain[0]), wq1, wq2, wk, wvt,
                          _rope_tables(s), s, tm)
    o_t = _mla_attention(q.reshape(b, s, -1), k.reshape(b, s, -1), v_t, min(MLA_TQ, s))
    rw = jnp.concatenate([router_w[0], jnp.zeros((d, LANES - N_EXPERTS), F32)], axis=1)
    rw_hi = rw.astype(BF16)
    rw2 = jnp.stack([rw_hi, (rw - rw_hi.astype(F32)).astype(BF16)])
    xn, info, counts = _outproj_route(o_t, odd_w_out[0].astype(BF16), x2,
                                      row(ln_gain[1, 0]), row(ln_bias[1, 0]), rw2, s, tm)
    pos, tile_expert, n_used, pad_info, n_slots = _routing_plan(info, counts, n)
    pos_t = pos.reshape(n // tm, tm, TOP_K).transpose(0, 2, 1)
    x_sorted = _scatter_rows(xn, pos_t, pad_info, n_slots, tm)
    y_sorted = _moe_ffn(x_sorted, tile_expert, n_used, moe_w_gate[0].astype(BF16),
                        moe_w_up[0].astype(BF16), moe_w_down[0].astype(BF16))
    out = _combine_ln(xn, info, pos_t, y_sorted, row(ln_gain[1, 1]), row(ln_bias[1, 1]), tm)
    return out.reshape(b, s, d)
```

```python
import functools
import math

import jax
import jax.numpy as jnp
from jax import lax
from jax.experimental import pallas as pl
from jax.experimental.pallas import tpu as pltpu

F32 = jnp.float32
BF16 = jnp.bfloat16

LANES = 128
V7X_VMEM_BYTES = 64 * 1024 * 1024
VMEM_LIMIT = V7X_VMEM_BYTES - 8 * 1024 * 1024

D_MODEL = 1024
GLA_HEADS, GLA_DK, GLA_DV = 4, 64, 128
GLA_RANK, GLA_TEMP, GLA_CHUNK = 16, 16.0, 64
DIFF_HEADS, DIFF_DK, DIFF_DV = 4, 64, 128
N_BUCKETS, MAX_DISTANCE = 32, 128
MLA_HEADS, MLA_Q_LORA, MLA_KV_LORA = 16, 256, 128
MLA_NOPE, MLA_ROPE, MLA_DV = 64, 32, 64
ROPE_THETA = 10000.0
D_FF, N_EXPERTS, TOP_K = 2816, 8, 2
DEPTH = 2
ALPHA = (2 * DEPTH) ** 0.25
LN_EPS, RMS_EPS = 1e-5, 1e-6

COL_QK_G, COL_V_G, COL_G_G = 0, 512, 1024
COL_Q_D, COL_K_D, COL_ALR = 1536, 2048, 2560
EVEN_COLS = 2688

ATT_KC = 128
ATT_TQ = 256
MLA_TQ = 2048
COL_REDUCE_SLAB = 128
LOG2E = math.log2(math.e)

GLA_GROUP = 256
FF_CHUNK = 256
MOE_TILE = 512
DMA_UNROLL = 8


def _cparams(sem):
    return pltpu.CompilerParams(dimension_semantics=sem, vmem_limit_bytes=VMEM_LIMIT)


def _layer_norm(y, g, b):
    mu = jnp.mean(y, axis=-1, keepdims=True)
    d = y - mu
    var = jnp.mean(d * d, axis=-1, keepdims=True)
    return d * lax.rsqrt(var + LN_EPS) * g + b


def _split3(x):
    h1 = x.astype(BF16)
    r1 = x - h1.astype(F32)
    h2 = r1.astype(BF16)
    h3 = (r1 - h2.astype(F32)).astype(BF16)
    return h1, h2, h3


def _dot(a, b):
    return jnp.dot(a, b, preferred_element_type=F32)


def _dot_nt(a, b):
    return lax.dot_general(a, b, (((1,), (1,)), ((), ())), preferred_element_type=F32)


def _dot_tn(a, b):
    return lax.dot_general(a, b, (((0,), (0,)), ((), ())), preferred_element_type=F32)


def _in_proj_kernel(x_ref, w_ref, wvt_ref, o_ref, vt_ref):
    xb = x_ref[...].astype(BF16)
    o_ref[...] = _dot(xb, w_ref[...]).astype(o_ref.dtype)
    vt_ref[...] = _dot_nt(wvt_ref[...], xb).astype(vt_ref.dtype)


def _in_proj(x, w, wvt, seq, tm):
    n, k = x.shape
    m = w.shape[1]
    dvt = wvt.shape[0]
    per_seq = seq // tm
    return pl.pallas_call(
        _in_proj_kernel,
        grid=(n // tm,),
        in_specs=[pl.BlockSpec((tm, k), lambda i: (i, 0)),
                  pl.BlockSpec((k, m), lambda i: (0, 0)),
                  pl.BlockSpec((dvt, k), lambda i: (0, 0))],
        out_specs=[pl.BlockSpec((tm, m), lambda i: (i, 0)),
                   pl.BlockSpec((None, dvt, tm), lambda i: (i // per_seq, 0, i % per_seq))],
        out_shape=[jax.ShapeDtypeStruct((n, m), BF16),
                   jax.ShapeDtypeStruct((n // seq, dvt, seq), BF16)],
        compiler_params=_cparams(("parallel",)),
        name="in_proj",
    )(x, w, wvt)


def _gla_kernel(qk_ref, v_ref, g_ref, alr_ref, gup_ref, gbias_ref, gain_ref, o_ref,
                la_ref, qdec_ref, kdec_ref, decay_ref, acc_ref, st_ref):
    seq = qk_ref.shape[0]
    hk = GLA_HEADS * GLA_DK
    hv = GLA_HEADS * GLA_DV
    grp = GLA_GROUP
    c = GLA_CHUNK

    alr = alr_ref[...]
    for d in range(2):
        logits = _dot(alr, gup_ref[d]) + gbias_ref[d]
        log_sig = jnp.minimum(logits, 0.0) - jnp.log(1.0 + jnp.exp(-jnp.abs(logits)))
        la_ref[d] = log_sig / GLA_TEMP

    row = lax.broadcasted_iota(jnp.int32, (grp, grp), 0)
    col = lax.broadcasted_iota(jnp.int32, (grp, grp), 1)
    same = (row // c) == (col // c)
    tri = [same & (row >= col), same & (row <= col)]
    cum_inc = [jnp.where(t, 1.0, 0.0).astype(BF16) for t in tri]
    cum_rem = [jnp.where(same & (row < col), 1.0, 0.0).astype(BF16),
               jnp.where(same & (row > col), 1.0, 0.0).astype(BF16)]
    lane_k = lax.broadcasted_iota(jnp.int32, (grp, hk), 1) // GLA_DK
    scale = GLA_DK ** -0.5

    def group_body(r, carry):
        rows = pl.ds(pl.multiple_of(r * grp, grp), grp)
        q = qk_ref[rows, 0:hk].astype(F32)
        k = qk_ref[rows, hk:2 * hk].astype(F32)
        v = v_ref[rows, :]
        o_heads = [jnp.zeros((grp, GLA_DV), F32) for _ in range(GLA_HEADS)]
        for d in range(2):
            parts = _split3(la_ref[d, rows, :])
            bcum = sum(_dot(cum_inc[d], p) for p in parts)
            brem = sum(_dot(cum_rem[d], p) for p in parts)
            q_dec = q * jnp.exp(bcum) * scale
            k_inv = (k * jnp.exp(-bcum)).astype(BF16)
            qdec_ref[d, rows, :] = q_dec.astype(BF16)
            kdec_ref[d, rows, :] = (k * jnp.exp(brem)).astype(BF16)
            decay_ref[d, rows, :] = jnp.exp(bcum + brem)
            for h in range(GLA_HEADS):
                q_h = jnp.where(lane_k == h, q_dec, 0.0).astype(BF16)
                attn = jnp.where(tri[d], _dot_nt(q_h, k_inv), 0.0)
                o_heads[h] = o_heads[h] + _dot(attn.astype(BF16), v[:, h * GLA_DV:(h + 1) * GLA_DV])
        for h in range(GLA_HEADS):
            acc_ref[rows, h * GLA_DV:(h + 1) * GLA_DV] = o_heads[h]
        return carry

    lax.fori_loop(0, seq // grp, group_body, 0)

    srow = lax.broadcasted_iota(jnp.int32, (hv, hk), 0) // GLA_DV
    scol = lax.broadcasted_iota(jnp.int32, (hv, hk), 1) // GLA_DK
    head_mask = srow == scol
    n_chunks = seq // c
    st_ref[...] = jnp.zeros_like(st_ref)

    def chunk_body(i, carry):
        for d in range(2):
            ci = i if d == 0 else n_chunks - 1 - i
            rows = pl.ds(pl.multiple_of(ci * c, c), c)
            state = st_ref[d]
            acc_ref[rows, :] += _dot_nt(qdec_ref[d, rows, :], state.astype(BF16))
            kv = _dot_tn(v_ref[rows, :], kdec_ref[d, rows, :])
            decay = decay_ref[d, pl.ds(pl.multiple_of(ci * c, c), 1), :]
            st_ref[d] = state * decay + jnp.where(head_mask, kv, 0.0)
        return carry

    lax.fori_loop(0, n_chunks, chunk_body, 0, unroll=2)

    gain = gain_ref[...]
    for h in range(GLA_HEADS):
        sl = slice(h * GLA_DV, (h + 1) * GLA_DV)
        o = acc_ref[:, sl]
        y = o * lax.rsqrt(jnp.mean(o * o, axis=-1, keepdims=True) + RMS_EPS) * gain
        gate = g_ref[:, sl].astype(F32)
        o_ref[:, sl] = (y * (gate * jax.nn.sigmoid(gate))).astype(o_ref.dtype)


def _gla(proj, gup, gbias, gain):
    b, s, _ = proj.shape
    hk, hv = GLA_HEADS * GLA_DK, GLA_HEADS * GLA_DV
    blk = lambda width, col: pl.BlockSpec((None, s, width), lambda i: (i, 0, col // width))
    return pl.pallas_call(
        _gla_kernel,
        grid=(b,),
        in_specs=[blk(2 * hk, COL_QK_G), blk(hv, COL_V_G), blk(hv, COL_G_G), blk(LANES, COL_ALR),
                  pl.BlockSpec((2, LANES, hk), lambda i: (0, 0, 0)),
                  pl.BlockSpec((2, 1, hk), lambda i: (0, 0, 0)),
                  pl.BlockSpec((1, GLA_DV), lambda i: (0, 0))],
        out_specs=pl.BlockSpec((None, s, hv), lambda i: (i, 0, 0)),
        out_shape=jax.ShapeDtypeStruct((b, s, hv), BF16),
        scratch_shapes=[pltpu.VMEM((2, s, hk), F32),
                        pltpu.VMEM((2, s, hk), BF16),
                        pltpu.VMEM((2, s, hk), BF16),
                        pltpu.VMEM((2, s, hk), F32),
                        pltpu.VMEM((s, hv), F32),
                        pltpu.VMEM((2, hv, hk), F32)],
        compiler_params=_cparams(("parallel",)),
        name="gla",
    )(proj, proj, proj, proj, gup, gbias, gain)


J_BELOW = (-MAX_DISTANCE - ATT_KC + 1) // ATT_KC
J_ABOVE = -(-(MAX_DISTANCE + ATT_TQ - 1) // ATT_KC)
N_BAND = J_ABOVE - J_BELOW + 1
assert ATT_KC * J_BELOW + ATT_KC - 1 <= -MAX_DISTANCE
assert ATT_KC * J_ABOVE - (ATT_TQ - 1) >= MAX_DISTANCE


def _band_buckets():
    half = N_BUCKETS // 2
    max_exact = half // 2
    j = (jnp.arange(N_BAND, dtype=jnp.int32) + J_BELOW)[:, None, None]
    rel = (j * ATT_KC + jnp.arange(ATT_KC, dtype=jnp.int32)[None, :, None]
           - jnp.arange(ATT_TQ, dtype=jnp.int32)[None, None, :])
    bucket = jnp.where(rel > 0, half, 0).astype(jnp.int32)
    n = jnp.abs(rel)
    n_large = max_exact + (jnp.log(jnp.maximum(n, max_exact).astype(F32) / max_exact)
                           / math.log(MAX_DISTANCE / max_exact) * (half - max_exact)).astype(jnp.int32)
    n_large = jnp.minimum(n_large, half - 1)
    return bucket + jnp.where(n < max_exact, n, n_large)


def _band_kernel(tab_ref, bucket_ref, o_ref, *, n_heads):
    h = pl.program_id(0)
    for t in range(N_BAND):
        bucket = bucket_ref[t]
        val = jnp.zeros(bucket.shape, F32)
        for bkt in range(N_BUCKETS):
            val = jnp.where(bucket == bkt, tab_ref[bkt * n_heads + h], val)
        o_ref[t] = val * LOG2E


def _bias_band(table):
    n_heads = table.shape[1]
    bucket = jnp.bitwise_and(_band_buckets(), N_BUCKETS - 1)
    return pl.pallas_call(
        functools.partial(_band_kernel, n_heads=n_heads),
        grid=(n_heads,),
        in_specs=[pl.BlockSpec(memory_space=pltpu.SMEM),
                  pl.BlockSpec((N_BAND, ATT_KC, ATT_TQ), lambda h: (0, 0, 0))],
        out_specs=pl.BlockSpec((None, N_BAND, ATT_KC, ATT_TQ), lambda h: (h, 0, 0, 0)),
        out_shape=jax.ShapeDtypeStruct((n_heads, N_BAND, ATT_KC, ATT_TQ), F32),
        compiler_params=_cparams(("parallel",)),
        name="bias_band",
    )(table.reshape(-1), bucket)


def _diff_kernel(lam_ref, q_ref, k_ref, vt_ref, band_ref, gain_ref, o_ref, *, out_scale):
    seq = q_ref.shape[0]
    n_chunks = seq // ATT_KC
    lane = lax.broadcasted_iota(jnp.int32, (ATT_TQ, 2 * DIFF_DK), 1)
    scale = DIFF_DK ** -0.5 * LOG2E
    items = [(r, mp) for r in range(seq // ATT_TQ) for mp in range(2)]
    first_map = {}

    def scores(item):
        r, mp = item
        q = (q_ref[r * ATT_TQ:(r + 1) * ATT_TQ, :].astype(F32) * scale).astype(BF16)
        q = jnp.where((lane >= DIFF_DK) == (mp == 1), q, jnp.zeros_like(q))
        tiles = [band_ref[min(max(c - (ATT_TQ // ATT_KC) * r - J_BELOW, 0), N_BAND - 1)] for c in range(n_chunks)]
        return _dot_nt(k_ref[...], q) + jnp.concatenate(tiles, axis=0)

    def finish(item, s_t):
        r, mp = item
        o_t = _softmax_pv_t(s_t, vt_ref[...])
        if mp == 0:
            first_map[r] = o_t
            return
        o = first_map.pop(r) - lam_ref[0] * o_t
        y = o * lax.rsqrt(jnp.mean(o * o, axis=0, keepdims=True) + RMS_EPS) * gain_ref[...]
        o_ref[r] = (y * out_scale).astype(o_ref.dtype)

    _pipelined(items, scores, finish)


def _diff_attention(proj, vt, band, lam, gain_col, lam_init):
    b, s, _ = proj.shape
    h = DIFF_HEADS
    w = 2 * DIFF_DK
    assert w == LANES
    return pl.pallas_call(
        functools.partial(_diff_kernel, out_scale=1.0 - lam_init),
        grid=(h, b),
        in_specs=[pl.BlockSpec(memory_space=pltpu.SMEM),
                  pl.BlockSpec((None, s, w), lambda hh, bb: (bb, 0, COL_Q_D // w + hh)),
                  pl.BlockSpec((None, s, w), lambda hh, bb: (bb, 0, COL_K_D // w + hh)),
                  pl.BlockSpec((None, DIFF_DV, s), lambda hh, bb: (bb, hh, 0)),
                  pl.BlockSpec((None, N_BAND, ATT_KC, ATT_TQ), lambda hh, bb: (hh, 0, 0, 0)),
                  pl.BlockSpec((DIFF_DV, 1), lambda hh, bb: (0, 0))],
        out_specs=pl.BlockSpec((None, s // ATT_TQ, DIFF_DV, ATT_TQ), lambda hh, bb: (bb, 0, hh, 0)),
        out_shape=jax.ShapeDtypeStruct((b, s // ATT_TQ, h * DIFF_DV, ATT_TQ), BF16),
        compiler_params=_cparams(("parallel", "parallel")),
        name="diff_attn",
    )(lam, proj, proj, vt, band, gain_col)


def _residual_ln(x_ref, o_ref, g_ref, b_ref, row_pairs, col_pairs):
    for j in range(x_ref.shape[0] // ATT_TQ):
        rows = slice(j * ATT_TQ, (j + 1) * ATT_TQ)
        h = ALPHA * x_ref[rows, :]
        for a_ref, w_ref in row_pairs:
            h = h + _dot(a_ref[rows, :], w_ref[...])
        for a_ref, w_ref in col_pairs:
            h = h + _dot_tn(a_ref[j], w_ref[...])
        o_ref[rows, :] = _layer_norm(h, g_ref[...], b_ref[...])


def _outproj_kernel(a_ref, at_ref, wa_ref, wt_ref, x_ref, g_ref, b_ref, o_ref):
    _residual_ln(x_ref, o_ref, g_ref, b_ref, [(a_ref, wa_ref)], [(at_ref, wt_ref)])


def _tile_specs(n, d, seq, tm):
    per_seq = seq // tm
    rows = lambda width: pl.BlockSpec((tm, width), lambda i: (i, 0))
    cols = lambda k: pl.BlockSpec((None, tm // ATT_TQ, k, ATT_TQ), lambda i: (i // per_seq, i % per_seq, 0, 0))
    full = lambda a: pl.BlockSpec(a.shape, lambda i: (0,) * a.ndim)
    return rows, cols, full


def _outproj_ln(a, at, wa, wt, x, g, b, seq, tm):
    n, d = x.shape
    rows, cols, full = _tile_specs(n, d, seq, tm)
    return pl.pallas_call(
        _outproj_kernel,
        grid=(n // tm,),
        in_specs=[rows(a.shape[1]), cols(at.shape[2]), full(wa), full(wt), rows(d), full(g), full(b)],
        out_specs=rows(d),
        out_shape=jax.ShapeDtypeStruct((n, d), F32),
        compiler_params=_cparams(("parallel",)),
        name="outproj_ln",
    )(a, at, wa, wt, x, g, b)


def _outproj_route_kernel(at_ref, w_ref, x_ref, g_ref, b_ref, rw_ref, o_ref, info_ref, cnt_ref, carry_ref):
    i = pl.program_id(0)
    tm = x_ref.shape[0]

    @pl.when(i == 0)
    def _():
        carry_ref[...] = jnp.zeros_like(carry_ref)

    _residual_ln(x_ref, o_ref, g_ref, b_ref, [], [(at_ref, w_ref)])
    xn = o_ref[...]

    x_hi = xn.astype(BF16)
    x_lo = (xn - x_hi.astype(F32)).astype(BF16)
    logits = _dot(x_hi, rw_ref[0]) + (_dot(x_hi, rw_ref[1]) + _dot(x_lo, rw_ref[0]))
    lane = lax.broadcasted_iota(jnp.int32, logits.shape, 1).astype(F32)
    neg = jnp.float32(-jnp.inf)
    logits = jnp.where(lane < N_EXPERTS, logits, neg)
    v1 = jnp.max(logits, axis=-1, keepdims=True)
    e1 = jnp.min(jnp.where(logits == v1, lane, float(LANES)), axis=-1, keepdims=True)
    rest = jnp.where(lane == e1, neg, logits)
    v2 = jnp.max(rest, axis=-1, keepdims=True)
    e2 = jnp.min(jnp.where(rest == v2, lane, float(LANES)), axis=-1, keepdims=True)
    t = jnp.exp(v2 - v1)
    w1 = 1.0 / (1.0 + t)
    w2 = t / (1.0 + t)

    onehot = jnp.where((lane == e1) | (lane == e2), 1.0, 0.0)
    row = lax.broadcasted_iota(jnp.int32, (tm, tm), 0)
    col = lax.broadcasted_iota(jnp.int32, (tm, tm), 1)
    before = jnp.where(row > col, 1.0, 0.0).astype(BF16)
    prior = carry_ref[...] + _dot(before, onehot.astype(BF16))
    r1 = jnp.sum(jnp.where(lane == e1, prior, 0.0), axis=-1, keepdims=True)
    r2 = jnp.sum(jnp.where(lane == e2, prior, 0.0), axis=-1, keepdims=True)
    carry_ref[...] += jnp.sum(onehot, axis=0, keepdims=True)
    cnt_ref[...] = jnp.broadcast_to(carry_ref[...], cnt_ref.shape)

    info = jnp.zeros(logits.shape, F32)
    for idx, val in enumerate((e1, e2, w1, w2, r1, r2)):
        info = jnp.where(lane == idx, val, info)
    info_ref[...] = info


def _outproj_route(at, w, x, g, b, router_w2, seq, tm):
    n, d = x.shape
    rows, cols, full = _tile_specs(n, d, seq, tm)
    return pl.pallas_call(
        _outproj_route_kernel,
        grid=(n // tm,),
        in_specs=[cols(at.shape[2]), full(w), rows(d), full(g), full(b), full(router_w2)],
        out_specs=[rows(d), rows(LANES),
                   pl.BlockSpec((8, LANES), lambda i: (0, 0))],
        out_shape=[jax.ShapeDtypeStruct((n, d), F32),
                   jax.ShapeDtypeStruct((n, LANES), F32),
                   jax.ShapeDtypeStruct((8, LANES), F32)],
        scratch_shapes=[pltpu.VMEM((1, LANES), F32)],
        compiler_params=_cparams(("arbitrary",)),
        name="outproj_route",
    )(at, w, x, g, b, router_w2)


def _swiglu_acc(xb, wg_ref, wu_ref, wd_ref, acc_ref):
    n_chunks = wg_ref.shape[-1] // FF_CHUNK
    for c in range(n_chunks):
        cols = slice(c * FF_CHUNK, (c + 1) * FF_CHUNK)
        gate = _dot(xb, wg_ref[:, cols])
        up = _dot(xb, wu_ref[:, cols])
        hidden = (gate * jax.nn.sigmoid(gate) * up).astype(BF16)
        part = _dot(hidden, wd_ref[cols, :])
        if c == 0:
            acc_ref[...] = part
        else:
            acc_ref[...] += part


def _ffn_ln_kernel(x_ref, wg_ref, wu_ref, wd_ref, g_ref, b_ref, o_ref):
    x = x_ref[...]
    _swiglu_acc(x.astype(BF16), wg_ref, wu_ref, wd_ref, o_ref)
    o_ref[...] = _layer_norm(ALPHA * x + o_ref[...], g_ref[...], b_ref[...])


def _ffn_ln(x, wg, wu, wd, g, b, tm):
    n, d = x.shape
    f = wg.shape[1]
    once = lambda shape: pl.BlockSpec(shape, lambda i: (0,) * len(shape), pipeline_mode=pl.Buffered(1))
    return pl.pallas_call(
        _ffn_ln_kernel,
        grid=(n // tm,),
        in_specs=[pl.BlockSpec((tm, d), lambda i: (i, 0)),
                  once((d, f)), once((d, f)), once((f, d)), once((1, d)), once((1, d))],
        out_specs=pl.BlockSpec((tm, d), lambda i: (i, 0)),
        out_shape=jax.ShapeDtypeStruct((n, d), F32),
        compiler_params=_cparams(("parallel",)),
        name="ffn_ln",
    )(x, wg, wu, wd, g, b)


def _moe_kernel(te_ref, nu_ref, x_ref, wg_ref, wu_ref, wd_ref, o_ref):
    t = pl.program_id(0)

    @pl.when(t < nu_ref[0])
    def _():
        _swiglu_acc(x_ref[...].astype(BF16), wg_ref, wu_ref, wd_ref, o_ref)

    @pl.when(t >= nu_ref[0])
    def _():
        o_ref[...] = jnp.zeros_like(o_ref)


def _moe_ffn(x_sorted, tile_expert, n_used, wg, wu, wd):
    p, d = x_sorted.shape
    f = wg.shape[2]
    tm = MOE_TILE
    grid_spec = pltpu.PrefetchScalarGridSpec(
        num_scalar_prefetch=2,
        grid=(p // tm,),
        in_specs=[pl.BlockSpec((tm, d), lambda t, te, nu: (jnp.minimum(t, nu[0] - 1), 0)),
                  pl.BlockSpec((None, d, f), lambda t, te, nu: (te[t], 0, 0)),
                  pl.BlockSpec((None, d, f), lambda t, te, nu: (te[t], 0, 0)),
                  pl.BlockSpec((None, f, d), lambda t, te, nu: (te[t], 0, 0))],
        out_specs=pl.BlockSpec((tm, d), lambda t, te, nu: (t, 0)),
    )
    return pl.pallas_call(
        _moe_kernel,
        grid_spec=grid_spec,
        out_shape=jax.ShapeDtypeStruct((p, d), F32),
        compiler_params=_cparams(("arbitrary",)),
        name="moe_ffn",
    )(tile_expert, n_used, x_sorted, wg, wu, wd)


def _scatter_kernel(pos_ref, pad_ref, x_ref, o_hbm, zero_ref, sem, zsem):
    i = pl.program_id(0)
    tm = x_ref.shape[0]
    n_tiles = o_hbm.shape[0] // MOE_TILE

    @pl.when(i == 0)
    def _():
        zero_ref[...] = jnp.zeros_like(zero_ref)
        zero_row = zero_ref.at[pl.ds(0, 1)]
        for e in range(N_EXPERTS):
            start, count = pad_ref[e], pad_ref[N_EXPERTS + e]

            def fill(r, carry):
                pltpu.make_async_copy(zero_row, o_hbm.at[pl.ds(start + r, 1)], zsem).start()
                return carry

            lax.fori_loop(0, count, fill, 0)

            def drain(r, carry):
                pltpu.make_async_copy(zero_row, o_hbm.at[pl.ds(start + r, 1)], zsem).wait()
                return carry

            lax.fori_loop(0, count, drain, 0)

        def fill_tile(t, carry):
            dst = o_hbm.at[pl.ds(pl.multiple_of(t * MOE_TILE, MOE_TILE), MOE_TILE)]
            cp = pltpu.make_async_copy(zero_ref, dst, zsem)
            cp.start()
            cp.wait()
            return carry

        lax.fori_loop(pad_ref[2 * N_EXPERTS], n_tiles, fill_tile, 0)

    def issue(r, carry):
        for kk in range(TOP_K):
            dst = pos_ref[0, kk, r]
            pltpu.make_async_copy(x_ref.at[pl.ds(r, 1)], o_hbm.at[pl.ds(dst, 1)], sem).start()
        return carry

    lax.fori_loop(0, tm, issue, 0, unroll=DMA_UNROLL)

    for kk in range(TOP_K):
        pltpu.make_async_copy(x_ref, o_hbm.at[pl.ds(0, tm)], sem).wait()


def _scatter_rows(x, pos, pad_info, n_slots, tm):
    n, d = x.shape
    return pl.pallas_call(
        _scatter_kernel,
        grid=(n // tm,),
        in_specs=[pl.BlockSpec((1, TOP_K, tm), lambda i: (i, 0, 0), memory_space=pltpu.SMEM),
                  pl.BlockSpec(memory_space=pltpu.SMEM),
                  pl.BlockSpec((tm, d), lambda i: (i, 0))],
        out_specs=pl.BlockSpec(memory_space=pl.ANY),
        out_shape=jax.ShapeDtypeStruct((n_slots, d), F32),
        scratch_shapes=[pltpu.VMEM((MOE_TILE, d), F32),
                        pltpu.SemaphoreType.DMA(()),
                        pltpu.SemaphoreType.DMA(())],
        compiler_params=_cparams(("arbitrary",)),
        name="scatter_rows",
    )(pos, pad_info, x)


def _combine_kernel(pos_ref, x_ref, info_ref, g_ref, b_ref, y_hbm, o_ref, buf_ref, sem):
    tm = x_ref.shape[0]

    def issue(r, carry):
        for kk in range(TOP_K):
            src = pos_ref[0, kk, r]
            pltpu.make_async_copy(y_hbm.at[pl.ds(src, 1)], buf_ref.at[kk, pl.ds(r, 1)], sem).start()
        return carry

    lax.fori_loop(0, tm, issue, 0, unroll=DMA_UNROLL)

    for kk in range(TOP_K):
        pltpu.make_async_copy(y_hbm.at[pl.ds(0, tm)], buf_ref.at[kk], sem).wait()

    info = info_ref[...]
    w1 = info[:, 2:3]
    w2 = info[:, 3:4]
    f = w1 * buf_ref[0] + w2 * buf_ref[1]
    o_ref[...] = _layer_norm(ALPHA * x_ref[...] + f, g_ref[...], b_ref[...])


def _combine_ln(x, info, pos, y_sorted, g, b, tm):
    n, d = x.shape
    return pl.pallas_call(
        _combine_kernel,
        grid=(n // tm,),
        in_specs=[pl.BlockSpec((1, TOP_K, tm), lambda i: (i, 0, 0), memory_space=pltpu.SMEM),
                  pl.BlockSpec((tm, d), lambda i: (i, 0)),
                  pl.BlockSpec((tm, LANES), lambda i: (i, 0)),
                  pl.BlockSpec((1, d), lambda i: (0, 0)),
                  pl.BlockSpec((1, d), lambda i: (0, 0)),
                  pl.BlockSpec(memory_space=pl.ANY)],
        out_specs=pl.BlockSpec((tm, d), lambda i: (i, 0)),
        out_shape=jax.ShapeDtypeStruct((n, d), F32),
        scratch_shapes=[pltpu.VMEM((TOP_K, tm, d), F32),
                        pltpu.SemaphoreType.DMA(())],
        compiler_params=_cparams(("arbitrary",)),
        name="combine_ln",
    )(pos, x, info, g, b, y_sorted)


def _mla_proj_kernel(x_ref, win_ref, qg_ref, kvg_ref, wq1_ref, wq2_ref, wk_ref, wvt_ref,
                     cq_ref, sq_ref, ck_ref, sk_ref, q_ref, k_ref, vt_ref):
    c = _dot(x_ref[...].astype(BF16), win_ref[...])
    cq = c[:, :MLA_Q_LORA]
    ckv = c[:, MLA_Q_LORA:MLA_Q_LORA + MLA_KV_LORA]
    off = MLA_Q_LORA + MLA_KV_LORA
    k_rope = c[:, off:off + LANES] * ck_ref[...] + c[:, off + LANES:off + 2 * LANES] * sk_ref[...]
    cq = (cq * lax.rsqrt(jnp.mean(cq * cq, axis=-1, keepdims=True) + RMS_EPS) * qg_ref[...]).astype(BF16)
    ckv = (ckv * lax.rsqrt(jnp.mean(ckv * ckv, axis=-1, keepdims=True) + RMS_EPS) * kvg_ref[...]).astype(BF16)
    q1 = _dot(cq, wq1_ref[...])
    q2 = _dot(cq, wq2_ref[...])
    k1 = _dot(ckv, wk_ref[...])
    vt_ref[...] = _dot_nt(wvt_ref[...], ckv).astype(vt_ref.dtype)
    cos_q, sin_q = cq_ref[...], sq_ref[...]
    for h in range(MLA_HEADS):
        sl = slice(h * LANES, (h + 1) * LANES)
        q_ref[:, sl] = (q1[:, sl] * cos_q + q2[:, sl] * sin_q).astype(q_ref.dtype)
        k_ref[:, sl] = (k1[:, sl] + k_rope).astype(k_ref.dtype)


def _mla_proj(x, win, qg, kvg, wq1, wq2, wk, wvt, tabs, seq, tm):
    n, d = x.shape
    per_seq = seq // tm
    full = lambda a: pl.BlockSpec(a.shape, lambda i: (0,) * a.ndim)
    tab = pl.BlockSpec((tm, LANES), lambda i: (i % per_seq, 0))
    hw = MLA_HEADS * LANES
    dvt = wvt.shape[0]
    return pl.pallas_call(
        _mla_proj_kernel,
        grid=(n // tm,),
        in_specs=[pl.BlockSpec((tm, d), lambda i: (i, 0)), full(win), full(qg), full(kvg),
                  full(wq1), full(wq2), full(wk), full(wvt), tab, tab, tab, tab],
        out_specs=[pl.BlockSpec((tm, hw), lambda i: (i, 0)),
                   pl.BlockSpec((tm, hw), lambda i: (i, 0)),
                   pl.BlockSpec((None, dvt, tm), lambda i: (i // per_seq, 0, i % per_seq))],
        out_shape=[jax.ShapeDtypeStruct((n, hw), BF16),
                   jax.ShapeDtypeStruct((n, hw), BF16),
                   jax.ShapeDtypeStruct((n // seq, dvt, seq), BF16)],
        compiler_params=_cparams(("parallel",)),
        name="mla_proj",
    )(x, win, qg, kvg, wq1, wq2, wk, wvt, *tabs)


def _softmax_pv_t(s_t, vt):
    p = jnp.exp2(s_t - _col_reduce(s_t, jnp.max))
    l = _col_reduce(p, jnp.sum)
    return _dot(vt, p.astype(BF16)) * (1.0 / l)


def _col_reduce(x, op):
    rows, n = x.shape
    slab = COL_REDUCE_SLAB if rows % COL_REDUCE_SLAB == 0 else rows
    return op(op(x.reshape(rows // slab, slab, n), axis=0), axis=0, keepdims=True)


def _pipelined(items, scores, finish):
    s_next = scores(items[0])
    for idx, item in enumerate(items):
        s = s_next
        if idx + 1 < len(items):
            s_next = scores(items[idx + 1])
        finish(item, s)


def _mla_attn_kernel(q_ref, k_ref, vt_ref, o_ref):
    n_sub = q_ref.shape[0] // ATT_TQ
    items = [(r, hh) for r in range(n_sub) for hh in range(2)]
    cols = lambda hh: slice(hh * LANES, (hh + 1) * LANES)

    def scores(item):
        r, hh = item
        return _dot_nt(k_ref[:, cols(hh)], q_ref[r * ATT_TQ:(r + 1) * ATT_TQ, cols(hh)])

    def finish(item, s_t):
        r, hh = item
        vrows = slice(hh * MLA_DV, (hh + 1) * MLA_DV)
        o_ref[r, vrows, :] = _softmax_pv_t(s_t, vt_ref[vrows, :]).astype(o_ref.dtype)

    _pipelined(items, scores, finish)


def _mla_attention(q, k, vt, tq):
    b, s, _ = q.shape
    pairs = MLA_HEADS // 2
    pair_dv = 2 * MLA_DV
    return pl.pallas_call(
        _mla_attn_kernel,
        grid=(b, pairs, s // tq),
        in_specs=[pl.BlockSpec((None, tq, 2 * LANES), lambda bb, p, i: (bb, i, p)),
                  pl.BlockSpec((None, s, 2 * LANES), lambda bb, p, i: (bb, 0, p)),
                  pl.BlockSpec((None, pair_dv, s), lambda bb, p, i: (bb, p, 0))],
        out_specs=pl.BlockSpec((None, tq // ATT_TQ, pair_dv, ATT_TQ), lambda bb, p, i: (bb, i, p, 0)),
        out_shape=jax.ShapeDtypeStruct((b, s // ATT_TQ, MLA_HEADS * MLA_DV, ATT_TQ), BF16),
        compiler_params=_cparams(("parallel", "parallel", "parallel")),
        name="mla_attn",
    )(q, k, vt)


def _even_in_weight(w):
    hk, hv = GLA_HEADS * GLA_DK, GLA_HEADS * GLA_DV
    widths = (hk, hk, hv, hv, 2 * GLA_RANK, DIFF_HEADS * 2 * DIFF_DK, DIFF_HEADS * 2 * DIFF_DK,
              DIFF_HEADS * DIFF_DV)
    offs = [0]
    for wd_ in widths:
        offs.append(offs[-1] + wd_)
    piece = lambda j: w[:, offs[j]:offs[j + 1]]
    pad = jnp.zeros((w.shape[0], EVEN_COLS - COL_ALR - 2 * GLA_RANK), w.dtype)
    main = jnp.concatenate([piece(0), piece(1), piece(2), piece(3), piece(5), piece(6), piece(4), pad], axis=1)
    return main.astype(BF16), piece(7).T.astype(BF16)


def _rot_half_cols(w):
    half = MLA_ROPE // 2
    shp = w.shape
    g = w.reshape(shp[0], -1, MLA_ROPE)
    return jnp.concatenate([-g[..., half:], g[..., :half]], axis=-1).reshape(shp)


def _mla_weights(w_in, w_uq, w_ukv):
    d = w_in.shape[0]
    dq = MLA_NOPE + MLA_ROPE
    z = lambda rows, cols: jnp.zeros((rows, cols), F32)
    w_kr = w_in[:, MLA_Q_LORA + MLA_KV_LORA:]
    kr_blk = lambda m: jnp.concatenate([z(d, MLA_NOPE), m, z(d, LANES - dq)], axis=1)
    win = jnp.concatenate([w_in[:, :MLA_Q_LORA + MLA_KV_LORA], kr_blk(w_kr), kr_blk(_rot_half_cols(w_kr))],
                          axis=1).astype(BF16)
    uq = w_uq.reshape(MLA_Q_LORA, MLA_HEADS, dq)
    pad_q = jnp.zeros((MLA_Q_LORA, MLA_HEADS, LANES - dq), F32)
    wq1 = jnp.concatenate([uq, pad_q], axis=-1).reshape(MLA_Q_LORA, -1).astype(BF16)
    rot = _rot_half_cols(uq[..., MLA_NOPE:].reshape(MLA_Q_LORA, -1)).reshape(MLA_Q_LORA, MLA_HEADS, MLA_ROPE)
    wq2 = jnp.concatenate([jnp.zeros((MLA_Q_LORA, MLA_HEADS, MLA_NOPE), F32), rot, pad_q],
                          axis=-1).reshape(MLA_Q_LORA, -1).astype(BF16)
    ukv = w_ukv.reshape(MLA_KV_LORA, MLA_HEADS, MLA_NOPE + MLA_DV)
    wk = jnp.concatenate([ukv[..., :MLA_NOPE], jnp.zeros((MLA_KV_LORA, MLA_HEADS, LANES - MLA_NOPE), F32)],
                         axis=-1).reshape(MLA_KV_LORA, -1).astype(BF16)
    wvt = ukv[..., MLA_NOPE:].reshape(MLA_KV_LORA, -1).T.astype(BF16)
    return win, wq1, wq2, wk, wvt


def _rope_tables(seq):
    half = MLA_ROPE // 2
    inv = ROPE_THETA ** (-jnp.arange(half, dtype=F32) / half)
    ang = jnp.arange(seq, dtype=F32)[:, None] * inv[None, :]
    cos = jnp.concatenate([jnp.cos(ang), jnp.cos(ang)], axis=1)
    sin = jnp.concatenate([jnp.sin(ang), jnp.sin(ang)], axis=1)
    dq = MLA_NOPE + MLA_ROPE
    scale = dq ** -0.5 * LOG2E
    lay = lambda a, fill: jnp.concatenate([jnp.full((seq, MLA_NOPE), fill, F32), a,
                                           jnp.zeros((seq, LANES - dq), F32)], axis=1)
    return lay(cos, 1.0) * scale, lay(sin, 0.0) * scale, lay(cos, 0.0), lay(sin, 0.0)


def _routing_plan(info, counts, n_tokens):
    tm = MOE_TILE
    cnt = counts[0, :N_EXPERTS].astype(jnp.int32)
    padded = ((cnt + tm - 1) // tm) * tm
    ends = jnp.cumsum(padded)
    starts = ends - padded
    e = info[:, 0:TOP_K].astype(jnp.int32)
    rank = info[:, 4:4 + TOP_K].astype(jnp.int32)
    pos = starts[e] + rank
    n_tiles = (n_tokens * TOP_K) // tm + N_EXPERTS
    tile_start = jnp.arange(n_tiles, dtype=jnp.int32) * tm
    tile_expert = jnp.minimum(jnp.sum(tile_start[:, None] >= ends[None, :], axis=1), N_EXPERTS - 1)
    n_used = (ends[-1] // tm).reshape(1)
    pad_info = jnp.concatenate([starts + cnt, padded - cnt, n_used])
    return pos, tile_expert.astype(jnp.int32), n_used.astype(jnp.int32), pad_info.astype(jnp.int32), n_tiles * tm


def kernel(x, rel_bias_table, even_w_in, gla_gate_up, gla_gate_bias, gla_norm_gain, diff_lambda, diff_norm_gain, even_w_out, ffn_w_gate, ffn_w_up, ffn_w_down, odd_w_in, mla_q_norm_gain, mla_kv_norm_gain, mla_w_uq, mla_w_ukv, odd_w_out, router_w, moe_w_gate, moe_w_up, moe_w_down, ln_gain, ln_bias):
    b, s, d = x.shape
    n = b * s
    tm = min(512, s)
    assert s % tm == 0 and tm % ATT_TQ == 0 and s % GLA_GROUP == 0
    x2 = x.reshape(n, d)
    row = lambda v: v.reshape(1, -1)

    w_main, w_vdt = _even_in_weight(even_w_in[0])
    proj, vd_t = _in_proj(x2, w_main, w_vdt, s, tm)
    proj = proj.reshape(b, s, EVEN_COLS)
    gup = jnp.zeros((2, LANES, GLA_HEADS * GLA_DK), F32)
    for dd in range(2):
        gup = gup.at[dd, dd * GLA_RANK:(dd + 1) * GLA_RANK].set(gla_gate_up[0, dd])
    o_gla = _gla(proj, gup.astype(BF16), gla_gate_bias[0][:, None, :], row(gla_norm_gain[0]))
    lam_init = 0.8 - 0.6 * math.exp(-0.3 * 0)
    lf = diff_lambda[0]
    lam = (jnp.exp(jnp.sum(lf[0] * lf[1])) - jnp.exp(jnp.sum(lf[2] * lf[3])) + lam_init).reshape(1)
    o_diff_t = _diff_attention(proj, vd_t, _bias_band(rel_bias_table), lam,
                               diff_norm_gain[0].reshape(-1, 1), lam_init)
    hv = GLA_HEADS * GLA_DV
    w_out = even_w_out[0].astype(BF16)
    x2 = _outproj_ln(o_gla.reshape(n, hv), o_diff_t, w_out[:hv], w_out[hv:], x2,
                     row(ln_gain[0, 0]), row(ln_bias[0, 0]), s, tm)
    x2 = _ffn_ln(x2, ffn_w_gate[0].astype(BF16), ffn_w_up[0].astype(BF16), ffn_w_down[0].astype(BF16),
                 row(ln_gain[0, 1]), row(ln_bias[0, 1]), tm)

    win, wq1, wq2, wk, wvt = _mla_weights(odd_w_in[0], mla_w_uq[0], mla_w_ukv[0])
    q, k, v_t = _mla_proj(x2, win, row(mla_q_norm_gain[0]), row(mla_kv_norm_gain[0]), wq1, wq2, wk, wvt,
                          _rope_tables(s), s, tm)
    o_t = _mla_attention(q.reshape(b, s, -1), k.reshape(b, s, -1), v_t, min(MLA_TQ, s))
    rw = jnp.concatenate([router_w[0], jnp.zeros((d, LANES - N_EXPERTS), F32)], axis=1)
    rw_hi = rw.astype(BF16)
    rw2 = jnp.stack([rw_hi, (rw - rw_hi.astype(F32)).astype(BF16)])
    xn, info, counts = _outproj_route(o_t, odd_w_out[0].astype(BF16), x2,
                                      row(ln_gain[1, 0]), row(ln_bias[1, 0]), rw2, s, tm)
    pos, tile_expert, n_used, pad_info, n_slots = _routing_plan(info, counts, n)
    pos_t = pos.reshape(n // tm, tm, TOP_K).transpose(0, 2, 1)
    x_sorted = _scatter_rows(xn, pos_t, pad_info, n_slots, tm)
    y_sorted = _moe_ffn(x_sorted, tile_expert, n_used, moe_w_gate[0].astype(BF16),
                        moe_w_up[0].astype(BF16), moe_w_down[0].astype(BF16))
    out = _combine_ln(xn, info, pos_t, y_sorted, row(ln_gain[1, 1]), row(ln_bias[1, 1]), tm)
    return out.reshape(b, s, d)
```

```python
import functools
import math

import jax
import jax.numpy as jnp
from jax import lax
from jax.experimental import pallas as pl
from jax.experimental.pallas import tpu as pltpu

F32 = jnp.float32
BF16 = jnp.bfloat16

LANES = 128
V7X_VMEM_BYTES = 64 * 1024 * 1024
VMEM_LIMIT = V7X_VMEM_BYTES - 8 * 1024 * 1024

D_MODEL = 1024
GLA_HEADS, GLA_DK, GLA_DV = 4, 64, 128
GLA_RANK, GLA_TEMP, GLA_CHUNK = 16, 16.0, 64
DIFF_HEADS, DIFF_DK, DIFF_DV = 4, 64, 128
N_BUCKETS, MAX_DISTANCE = 32, 128
MLA_HEADS, MLA_Q_LORA, MLA_KV_LORA = 16, 256, 128
MLA_NOPE, MLA_ROPE, MLA_DV = 64, 32, 64
ROPE_THETA = 10000.0
D_FF, N_EXPERTS, TOP_K = 2816, 8, 2
DEPTH = 2
ALPHA = (2 * DEPTH) ** 0.25
LN_EPS, RMS_EPS = 1e-5, 1e-6

COL_QK_G, COL_V_G, COL_G_G = 0, 512, 1024
COL_Q_D, COL_K_D, COL_ALR = 1536, 2048, 2560
EVEN_COLS = 2688

ATT_KC = 128
ATT_TQ = 256
MLA_TQ = 2048
COL_REDUCE_SLAB = 128
LOG2E = math.log2(math.e)

GLA_GROUP = 256
FF_CHUNK = 256
MOE_TILE = 512
DMA_UNROLL = 8


def _cparams(sem):
    return pltpu.CompilerParams(dimension_semantics=sem, vmem_limit_bytes=VMEM_LIMIT)


def _layer_norm(y, g, b):
    mu = jnp.mean(y, axis=-1, keepdims=True)
    d = y - mu
    var = jnp.mean(d * d, axis=-1, keepdims=True)
    return d * lax.rsqrt(var + LN_EPS) * g + b


def _split3(x):
    h1 = x.astype(BF16)
    r1 = x - h1.astype(F32)
    h2 = r1.astype(BF16)
    h3 = (r1 - h2.astype(F32)).astype(BF16)
    return h1, h2, h3


def _dot(a, b):
    return jnp.dot(a, b, preferred_element_type=F32)


def _dot_nt(a, b):
    return lax.dot_general(a, b, (((1,), (1,)), ((), ())), preferred_element_type=F32)


def _dot_tn(a, b):
    return lax.dot_general(a, b, (((0,), (0,)), ((), ())), preferred_element_type=F32)


def _in_proj_kernel(x_ref, w_ref, wvt_ref, o_ref, vt_ref):
    xb = x_ref[...].astype(BF16)
    o_ref[...] = _dot(xb, w_ref[...]).astype(o_ref.dtype)
    vt_ref[...] = _dot_nt(wvt_ref[...], xb).astype(vt_ref.dtype)


def _in_proj(x, w, wvt, seq, tm):
    n, k = x.shape
    m = w.shape[1]
    dvt = wvt.shape[0]
    per_seq = seq // tm
    return pl.pallas_call(
        _in_proj_kernel,
        grid=(n // tm,),
        in_specs=[pl.BlockSpec((tm, k), lambda i: (i, 0)),
                  pl.BlockSpec((k, m), lambda i: (0, 0)),
                  pl.BlockSpec((dvt, k), lambda i: (0, 0))],
        out_specs=[pl.BlockSpec((tm, m), lambda i: (i, 0)),
                   pl.BlockSpec((None, dvt, tm), lambda i: (i // per_seq, 0, i % per_seq))],
        out_shape=[jax.ShapeDtypeStruct((n, m), BF16),
                   jax.ShapeDtypeStruct((n // seq, dvt, seq), BF16)],
        compiler_params=_cparams(("parallel",)),
        name="in_proj",
    )(x, w, wvt)


def _gla_kernel(qk_ref, v_ref, g_ref, alr_ref, gup_ref, gbias_ref, gain_ref, o_ref,
                la_ref, qdec_ref, kdec_ref, decay_ref, acc_ref, st_ref):
    seq = qk_ref.shape[0]
    hk = GLA_HEADS * GLA_DK
    hv = GLA_HEADS * GLA_DV
    grp = GLA_GROUP
    c = GLA_CHUNK

    alr = alr_ref[...]
    for d in range(2):
        logits = _dot(alr, gup_ref[d]) + gbias_ref[d]
        log_sig = jnp.minimum(logits, 0.0) - jnp.log(1.0 + jnp.exp(-jnp.abs(logits)))
        la_ref[d] = log_sig / GLA_TEMP

    row = lax.broadcasted_iota(jnp.int32, (grp, grp), 0)
    col = lax.broadcasted_iota(jnp.int32, (grp, grp), 1)
    same = (row // c) == (col // c)
    tri = [same & (row >= col), same & (row <= col)]
    cum_inc = [jnp.where(t, 1.0, 0.0).astype(BF16) for t in tri]
    cum_rem = [jnp.where(same & (row < col), 1.0, 0.0).astype(BF16),
               jnp.where(same & (row > col), 1.0, 0.0).astype(BF16)]
    lane_k = lax.broadcasted_iota(jnp.int32, (grp, hk), 1) // GLA_DK
    scale = GLA_DK ** -0.5

    def group_body(r, carry):
        rows = pl.ds(pl.multiple_of(r * grp, grp), grp)
        q = qk_ref[rows, 0:hk].astype(F32)
        k = qk_ref[rows, hk:2 * hk].astype(F32)
        v = v_ref[rows, :]
        o_heads = [jnp.zeros((grp, GLA_DV), F32) for _ in range(GLA_HEADS)]
        for d in range(2):
            parts = _split3(la_ref[d, rows, :])
            bcum = sum(_dot(cum_inc[d], p) for p in parts)
            brem = sum(_dot(cum_rem[d], p) for p in parts)
            q_dec = q * jnp.exp(bcum) * scale
            k_inv = (k * jnp.exp(-bcum)).astype(BF16)
            qdec_ref[d, rows, :] = q_dec.astype(BF16)
            kdec_ref[d, rows, :] = (k * jnp.exp(brem)).astype(BF16)
            decay_ref[d, rows, :] = jnp.exp(bcum + brem)
            for h in range(GLA_HEADS):
                q_h = jnp.where(lane_k == h, q_dec, 0.0).astype(BF16)
                attn = jnp.where(tri[d], _dot_nt(q_h, k_inv), 0.0)
                o_heads[h] = o_heads[h] + _dot(attn.astype(BF16), v[:, h * GLA_DV:(h + 1) * GLA_DV])
        for h in range(GLA_HEADS):
            acc_ref[rows, h * GLA_DV:(h + 1) * GLA_DV] = o_heads[h]
        return carry

    lax.fori_loop(0, seq // grp, group_body, 0)

    srow = lax.broadcasted_iota(jnp.int32, (hv, hk), 0) // GLA_DV
    scol = lax.broadcasted_iota(jnp.int32, (hv, hk), 1) // GLA_DK
    head_mask = srow == scol
    n_chunks = seq // c
    st_ref[...] = jnp.zeros_like(st_ref)

    def chunk_body(i, carry):
        for d in range(2):
            ci = i if d == 0 else n_chunks - 1 - i
            rows = pl.ds(pl.multiple_of(ci * c, c), c)
            state = st_ref[d]
            acc_ref[rows, :] += _dot_nt(qdec_ref[d, rows, :], state.astype(BF16))
            kv = _dot_tn(v_ref[rows, :], kdec_ref[d, rows, :])
            decay = decay_ref[d, pl.ds(pl.multiple_of(ci * c, c), 1), :]
            st_ref[d] = state * decay + jnp.where(head_mask, kv, 0.0)
        return carry

    lax.fori_loop(0, n_chunks, chunk_body, 0, unroll=2)

    gain = gain_ref[...]
    for h in range(GLA_HEADS):
        sl = slice(h * GLA_DV, (h + 1) * GLA_DV)
        o = acc_ref[:, sl]
        y = o * lax.rsqrt(jnp.mean(o * o, axis=-1, keepdims=True) + RMS_EPS) * gain
        gate = g_ref[:, sl].astype(F32)
        o_ref[:, sl] = (y * (gate * jax.nn.sigmoid(gate))).astype(o_ref.dtype)


def _gla(proj, gup, gbias, gain):
    b, s, _ = proj.shape
    hk, hv = GLA_HEADS * GLA_DK, GLA_HEADS * GLA_DV
    blk = lambda width, col: pl.BlockSpec((None, s, width), lambda i: (i, 0, col // width))
    return pl.pallas_call(
        _gla_kernel,
        grid=(b,),
        in_specs=[blk(2 * hk, COL_QK_G), blk(hv, COL_V_G), blk(hv, COL_G_G), blk(LANES, COL_ALR),
                  pl.BlockSpec((2, LANES, hk), lambda i: (0, 0, 0)),
                  pl.BlockSpec((2, 1, hk), lambda i: (0, 0, 0)),
                  pl.BlockSpec((1, GLA_DV), lambda i: (0, 0))],
        out_specs=pl.BlockSpec((None, s, hv), lambda i: (i, 0, 0)),
        out_shape=jax.ShapeDtypeStruct((b, s, hv), BF16),
        scratch_shapes=[pltpu.VMEM((2, s, hk), F32),
                        pltpu.VMEM((2, s, hk), BF16),
                        pltpu.VMEM((2, s, hk), BF16),
                        pltpu.VMEM((2, s, hk), F32),
                        pltpu.VMEM((s, hv), F32),
                        pltpu.VMEM((2, hv, hk), F32)],
        compiler_params=_cparams(("parallel",)),
        name="gla",
    )(proj, proj, proj, proj, gup, gbias, gain)


J_BELOW = (-MAX_DISTANCE - ATT_KC + 1) // ATT_KC
J_ABOVE = -(-(MAX_DISTANCE + ATT_TQ - 1) // ATT_KC)
N_BAND = J_ABOVE - J_BELOW + 1
assert ATT_KC * J_BELOW + ATT_KC - 1 <= -MAX_DISTANCE
assert ATT_KC * J_ABOVE - (ATT_TQ - 1) >= MAX_DISTANCE


def _band_buckets():
    half = N_BUCKETS // 2
    max_exact = half // 2
    j = (jnp.arange(N_BAND, dtype=jnp.int32) + J_BELOW)[:, None, None]
    rel = (j * ATT_KC + jnp.arange(ATT_KC, dtype=jnp.int32)[None, :, None]
           - jnp.arange(ATT_TQ, dtype=jnp.int32)[None, None, :])
    bucket = jnp.where(rel > 0, half, 0).astype(jnp.int32)
    n = jnp.abs(rel)
    n_large = max_exact + (jnp.log(jnp.maximum(n, max_exact).astype(F32) / max_exact)
                           / math.log(MAX_DISTANCE / max_exact) * (half - max_exact)).astype(jnp.int32)
    n_large = jnp.minimum(n_large, half - 1)
    return bucket + jnp.where(n < max_exact, n, n_large)


def _band_kernel(tab_ref, bucket_ref, o_ref, *, n_heads):
    h = pl.program_id(0)
    for t in range(N_BAND):
        bucket = bucket_ref[t]
        val = jnp.zeros(bucket.shape, F32)
        for bkt in range(N_BUCKETS):
            val = jnp.where(bucket == bkt, tab_ref[bkt * n_heads + h], val)
        o_ref[t] = val * LOG2E


def _bias_band(table):
    n_heads = table.shape[1]
    bucket = jnp.bitwise_and(_band_buckets(), N_BUCKETS - 1)
    return pl.pallas_call(
        functools.partial(_band_kernel, n_heads=n_heads),
        grid=(n_heads,),
        in_specs=[pl.BlockSpec(memory_space=pltpu.SMEM),
                  pl.BlockSpec((N_BAND, ATT_KC, ATT_TQ), lambda h: (0, 0, 0))],
        out_specs=pl.BlockSpec((None, N_BAND, ATT_KC, ATT_TQ), lambda h: (h, 0, 0, 0)),
        out_shape=jax.ShapeDtypeStruct((n_heads, N_BAND, ATT_KC, ATT_TQ), F32),
        compiler_params=_cparams(("parallel",)),
        name="bias_band",
    )(table.reshape(-1), bucket)


def _diff_kernel(lam_ref, q_ref, k_ref, vt_ref, band_ref, gain_ref, o_ref, *, out_scale):
    seq = q_ref.shape[0]
    n_chunks = seq // ATT_KC
    lane = lax.broadcasted_iota(jnp.int32, (ATT_TQ, 2 * DIFF_DK), 1)
    scale = DIFF_DK ** -0.5 * LOG2E
    items = [(r, mp) for r in range(seq // ATT_TQ) for mp in range(2)]
    first_map = {}

    def scores(item):
        r, mp = item
        q = (q_ref[r * ATT_TQ:(r + 1) * ATT_TQ, :].astype(F32) * scale).astype(BF16)
        q = jnp.where((lane >= DIFF_DK) == (mp == 1), q, jnp.zeros_like(q))
        tiles = [band_ref[min(max(c - (ATT_TQ // ATT_KC) * r - J_BELOW, 0), N_BAND - 1)] for c in range(n_chunks)]
        return _dot_nt(k_ref[...], q) + jnp.concatenate(tiles, axis=0)

    def finish(item, s_t):
        r, mp = item
        o_t = _softmax_pv_t(s_t, vt_ref[...])
        if mp == 0:
            first_map[r] = o_t
            return
        o = first_map.pop(r) - lam_ref[0] * o_t
        y = o * lax.rsqrt(jnp.mean(o * o, axis=0, keepdims=True) + RMS_EPS) * gain_ref[...]
        o_ref[r] = (y * out_scale).astype(o_ref.dtype)

    _pipelined(items, scores, finish)


def _diff_attention(proj, vt, band, lam, gain_col, lam_init):
    b, s, _ = proj.shape
    h = DIFF_HEADS
    w = 2 * DIFF_DK
    assert w == LANES
    return pl.pallas_call(
        functools.partial(_diff_kernel, out_scale=1.0 - lam_init),
        grid=(h, b),
        in_specs=[pl.BlockSpec(memory_space=pltpu.SMEM),
                  pl.BlockSpec((None, s, w), lambda hh, bb: (bb, 0, COL_Q_D // w + hh)),
                  pl.BlockSpec((None, s, w), lambda hh, bb: (bb, 0, COL_K_D // w + hh)),
                  pl.BlockSpec((None, DIFF_DV, s), lambda hh, bb: (bb, hh, 0)),
                  pl.BlockSpec((None, N_BAND, ATT_KC, ATT_TQ), lambda hh, bb: (hh, 0, 0, 0)),
                  pl.BlockSpec((DIFF_DV, 1), lambda hh, bb: (0, 0))],
        out_specs=pl.BlockSpec((None, s // ATT_TQ, DIFF_DV, ATT_TQ), lambda hh, bb: (bb, 0, hh, 0)),
        out_shape=jax.ShapeDtypeStruct((b, s // ATT_TQ, h * DIFF_DV, ATT_TQ), BF16),
        compiler_params=_cparams(("parallel", "parallel")),
        name="diff_attn",
    )(lam, proj, proj, vt, band, gain_col)


def _residual_ln(x_ref, o_ref, g_ref, b_ref, row_pairs, col_pairs):
    for j in range(x_ref.shape[0] // ATT_TQ):
        rows = slice(j * ATT_TQ, (j + 1) * ATT_TQ)
        h = ALPHA * x_ref[rows, :]
        for a_ref, w_ref in row_pairs:
            h = h + _dot(a_ref[rows, :], w_ref[...])
        for a_ref, w_ref in col_pairs:
            h = h + _dot_tn(a_ref[j], w_ref[...])
        o_ref[rows, :] = _layer_norm(h, g_ref[...], b_ref[...])


def _tile_specs(n, d, seq, tm):
    per_seq = seq // tm
    rows = lambda width: pl.BlockSpec((tm, width), lambda i: (i, 0))
    cols = lambda k: pl.BlockSpec((None, tm // ATT_TQ, k, ATT_TQ), lambda i: (i // per_seq, i % per_seq, 0, 0))
    full = lambda a: pl.BlockSpec(a.shape, lambda i: (0,) * a.ndim)
    return rows, cols, full


def _outproj_route_kernel(at_ref, w_ref, x_ref, g_ref, b_ref, rw_ref, o_ref, info_ref, cnt_ref, carry_ref):
    i = pl.program_id(0)
    tm = x_ref.shape[0]

    @pl.when(i == 0)
    def _():
        carry_ref[...] = jnp.zeros_like(carry_ref)

    _residual_ln(x_ref, o_ref, g_ref, b_ref, [], [(at_ref, w_ref)])
    xn = o_ref[...]

    x_hi = xn.astype(BF16)
    x_lo = (xn - x_hi.astype(F32)).astype(BF16)
    logits = _dot(x_hi, rw_ref[0]) + (_dot(x_hi, rw_ref[1]) + _dot(x_lo, rw_ref[0]))
    lane = lax.broadcasted_iota(jnp.int32, logits.shape, 1).astype(F32)
    neg = jnp.float32(-jnp.inf)
    logits = jnp.where(lane < N_EXPERTS, logits, neg)
    v1 = jnp.max(logits, axis=-1, keepdims=True)
    e1 = jnp.min(jnp.where(logits == v1, lane, float(LANES)), axis=-1, keepdims=True)
    rest = jnp.where(lane == e1, neg, logits)
    v2 = jnp.max(rest, axis=-1, keepdims=True)
    e2 = jnp.min(jnp.where(rest == v2, lane, float(LANES)), axis=-1, keepdims=True)
    t = jnp.exp(v2 - v1)
    w1 = 1.0 / (1.0 + t)
    w2 = t / (1.0 + t)

    onehot = jnp.where((lane == e1) | (lane == e2), 1.0, 0.0)
    row = lax.broadcasted_iota(jnp.int32, (tm, tm), 0)
    col = lax.broadcasted_iota(jnp.int32, (tm, tm), 1)
    before = jnp.where(row > col, 1.0, 0.0).astype(BF16)
    prior = carry_ref[...] + _dot(before, onehot.astype(BF16))
    r1 = jnp.sum(jnp.where(lane == e1, prior, 0.0), axis=-1, keepdims=True)
    r2 = jnp.sum(jnp.where(lane == e2, prior, 0.0), axis=-1, keepdims=True)
    carry_ref[...] += jnp.sum(onehot, axis=0, keepdims=True)
    cnt_ref[...] = jnp.broadcast_to(carry_ref[...], cnt_ref.shape)

    info = jnp.zeros(logits.shape, F32)
    for idx, val in enumerate((e1, e2, w1, w2, r1, r2)):
        info = jnp.where(lane == idx, val, info)
    info_ref[...] = info


def _outproj_route(at, w, x, g, b, router_w2, seq, tm):
    n, d = x.shape
    rows, cols, full = _tile_specs(n, d, seq, tm)
    return pl.pallas_call(
        _outproj_route_kernel,
        grid=(n // tm,),
        in_specs=[cols(at.shape[2]), full(w), rows(d), full(g), full(b), full(router_w2)],
        out_specs=[rows(d), rows(LANES),
                   pl.BlockSpec((8, LANES), lambda i: (0, 0))],
        out_shape=[jax.ShapeDtypeStruct((n, d), F32),
                   jax.ShapeDtypeStruct((n, LANES), F32),
                   jax.ShapeDtypeStruct((8, LANES), F32)],
        scratch_shapes=[pltpu.VMEM((1, LANES), F32)],
        compiler_params=_cparams(("arbitrary",)),
        name="outproj_route",
    )(at, w, x, g, b, router_w2)


def _swiglu_acc(xb, wg_ref, wu_ref, wd_ref, acc_ref):
    n_chunks = wg_ref.shape[-1] // FF_CHUNK
    for c in range(n_chunks):
        cols = slice(c * FF_CHUNK, (c + 1) * FF_CHUNK)
        gate = _dot(xb, wg_ref[:, cols])
        up = _dot(xb, wu_ref[:, cols])
        hidden = (gate * jax.nn.sigmoid(gate) * up).astype(BF16)
        part = _dot(hidden, wd_ref[cols, :])
        if c == 0:
            acc_ref[...] = part
        else:
            acc_ref[...] += part


def _mix_ffn_kernel(a_ref, at_ref, wa_ref, wt_ref, x_ref, g1_ref, b1_ref, wg_ref, wu_ref, wd_ref,
                    g2_ref, b2_ref, o_ref, h_ref):
    _residual_ln(x_ref, h_ref, g1_ref, b1_ref, [(a_ref, wa_ref)], [(at_ref, wt_ref)])
    h = h_ref[...]
    _swiglu_acc(h.astype(BF16), wg_ref, wu_ref, wd_ref, o_ref)
    o_ref[...] = _layer_norm(ALPHA * h + o_ref[...], g2_ref[...], b2_ref[...])


def _mix_ffn(a, at, wa, wt, x, g1, b1, wg, wu, wd, g2, b2, seq, tm):
    n, d = x.shape
    rows, cols, _ = _tile_specs(n, d, seq, tm)
    once = lambda arr: pl.BlockSpec(arr.shape, lambda i: (0,) * arr.ndim, pipeline_mode=pl.Buffered(1))
    return pl.pallas_call(
        _mix_ffn_kernel,
        grid=(n // tm,),
        in_specs=[rows(a.shape[1]), cols(at.shape[2]), once(wa), once(wt), rows(d), once(g1), once(b1),
                  once(wg), once(wu), once(wd), once(g2), once(b2)],
        out_specs=rows(d),
        out_shape=jax.ShapeDtypeStruct((n, d), F32),
        scratch_shapes=[pltpu.VMEM((tm, d), F32)],
        compiler_params=_cparams(("parallel",)),
        name="mix_ffn",
    )(a, at, wa, wt, x, g1, b1, wg, wu, wd, g2, b2)


def _moe_kernel(te_ref, nu_ref, x_ref, wg_ref, wu_ref, wd_ref, o_ref):
    t = pl.program_id(0)

    @pl.when(t < nu_ref[0])
    def _():
        _swiglu_acc(x_ref[...].astype(BF16), wg_ref, wu_ref, wd_ref, o_ref)

    @pl.when(t >= nu_ref[0])
    def _():
        o_ref[...] = jnp.zeros_like(o_ref)


def _moe_ffn(x_sorted, tile_expert, n_used, wg, wu, wd):
    p, d = x_sorted.shape
    f = wg.shape[2]
    tm = MOE_TILE
    grid_spec = pltpu.PrefetchScalarGridSpec(
        num_scalar_prefetch=2,
        grid=(p // tm,),
        in_specs=[pl.BlockSpec((tm, d), lambda t, te, nu: (jnp.minimum(t, nu[0] - 1), 0)),
                  pl.BlockSpec((None, d, f), lambda t, te, nu: (te[t], 0, 0)),
                  pl.BlockSpec((None, d, f), lambda t, te, nu: (te[t], 0, 0)),
                  pl.BlockSpec((None, f, d), lambda t, te, nu: (te[t], 0, 0))],
        out_specs=pl.BlockSpec((tm, d), lambda t, te, nu: (t, 0)),
    )
    return pl.pallas_call(
        _moe_kernel,
        grid_spec=grid_spec,
        out_shape=jax.ShapeDtypeStruct((p, d), F32),
        compiler_params=_cparams(("arbitrary",)),
        name="moe_ffn",
    )(tile_expert, n_used, x_sorted, wg, wu, wd)


def _scatter_kernel(pos_ref, pad_ref, x_ref, o_hbm, zero_ref, sem, zsem):
    i = pl.program_id(0)
    tm = x_ref.shape[0]
    n_tiles = o_hbm.shape[0] // MOE_TILE

    @pl.when(i == 0)
    def _():
        zero_ref[...] = jnp.zeros_like(zero_ref)
        zero_row = zero_ref.at[pl.ds(0, 1)]
        for e in range(N_EXPERTS):
            start, count = pad_ref[e], pad_ref[N_EXPERTS + e]

            def fill(r, carry):
                pltpu.make_async_copy(zero_row, o_hbm.at[pl.ds(start + r, 1)], zsem).start()
                return carry

            lax.fori_loop(0, count, fill, 0)

            def drain(r, carry):
                pltpu.make_async_copy(zero_row, o_hbm.at[pl.ds(start + r, 1)], zsem).wait()
                return carry

            lax.fori_loop(0, count, drain, 0)

        def fill_tile(t, carry):
            dst = o_hbm.at[pl.ds(pl.multiple_of(t * MOE_TILE, MOE_TILE), MOE_TILE)]
            cp = pltpu.make_async_copy(zero_ref, dst, zsem)
            cp.start()
            cp.wait()
            return carry

        lax.fori_loop(pad_ref[2 * N_EXPERTS], n_tiles, fill_tile, 0)

    def issue(r, carry):
        for kk in range(TOP_K):
            dst = pos_ref[0, kk, r]
            pltpu.make_async_copy(x_ref.at[pl.ds(r, 1)], o_hbm.at[pl.ds(dst, 1)], sem).start()
        return carry

    lax.fori_loop(0, tm, issue, 0, unroll=DMA_UNROLL)

    for kk in range(TOP_K):
        pltpu.make_async_copy(x_ref, o_hbm.at[pl.ds(0, tm)], sem).wait()


def _scatter_rows(x, pos, pad_info, n_slots, tm):
    n, d = x.shape
    return pl.pallas_call(
        _scatter_kernel,
        grid=(n // tm,),
        in_specs=[pl.BlockSpec((1, TOP_K, tm), lambda i: (i, 0, 0), memory_space=pltpu.SMEM),
                  pl.BlockSpec(memory_space=pltpu.SMEM),
                  pl.BlockSpec((tm, d), lambda i: (i, 0))],
        out_specs=pl.BlockSpec(memory_space=pl.ANY),
        out_shape=jax.ShapeDtypeStruct((n_slots, d), F32),
        scratch_shapes=[pltpu.VMEM((MOE_TILE, d), F32),
                        pltpu.SemaphoreType.DMA(()),
                        pltpu.SemaphoreType.DMA(())],
        compiler_params=_cparams(("arbitrary",)),
        name="scatter_rows",
    )(pos, pad_info, x)


def _combine_kernel(pos_ref, next_ref, x_ref, info_ref, g_ref, b_ref, y_hbm, o_ref, buf_ref, sem):
    i = pl.program_id(0)
    tm = x_ref.shape[0]
    slot = i % 2

    def gather(idx_ref, s):
        def issue(r, carry):
            for kk in range(TOP_K):
                src = idx_ref[0, kk, r]
                pltpu.make_async_copy(y_hbm.at[pl.ds(src, 1)], buf_ref.at[s, kk, pl.ds(r, 1)], sem.at[s]).start()
            return carry
        lax.fori_loop(0, tm, issue, 0, unroll=DMA_UNROLL)

    @pl.when(i == 0)
    def _():
        gather(pos_ref, slot)

    @pl.when(i + 1 < pl.num_programs(0))
    def _():
        gather(next_ref, 1 - slot)

    for kk in range(TOP_K):
        pltpu.make_async_copy(y_hbm.at[pl.ds(0, tm)], buf_ref.at[slot, kk], sem.at[slot]).wait()

    info = info_ref[...]
    w1 = info[:, 2:3]
    w2 = info[:, 3:4]
    f = w1 * buf_ref[slot, 0] + w2 * buf_ref[slot, 1]
    o_ref[...] = _layer_norm(ALPHA * x_ref[...] + f, g_ref[...], b_ref[...])


def _combine_ln(x, info, pos, y_sorted, g, b, tm):
    n, d = x.shape
    last = n // tm - 1
    return pl.pallas_call(
        _combine_kernel,
        grid=(n // tm,),
        in_specs=[pl.BlockSpec((1, TOP_K, tm), lambda i: (i, 0, 0), memory_space=pltpu.SMEM),
                  pl.BlockSpec((1, TOP_K, tm), lambda i: (jnp.minimum(i + 1, last), 0, 0), memory_space=pltpu.SMEM),
                  pl.BlockSpec((tm, d), lambda i: (i, 0)),
                  pl.BlockSpec((tm, LANES), lambda i: (i, 0)),
                  pl.BlockSpec((1, d), lambda i: (0, 0)),
                  pl.BlockSpec((1, d), lambda i: (0, 0)),
                  pl.BlockSpec(memory_space=pl.ANY)],
        out_specs=pl.BlockSpec((tm, d), lambda i: (i, 0)),
        out_shape=jax.ShapeDtypeStruct((n, d), F32),
        scratch_shapes=[pltpu.VMEM((2, TOP_K, tm, d), F32),
                        pltpu.SemaphoreType.DMA((2,))],
        compiler_params=_cparams(("arbitrary",)),
        name="combine_ln",
    )(pos, pos, x, info, g, b, y_sorted)


def _mla_proj_kernel(x_ref, win_ref, qg_ref, kvg_ref, wq1_ref, wq2_ref, wk_ref, wvt_ref,
                     cq_ref, sq_ref, ck_ref, sk_ref, q_ref, k_ref, vt_ref):
    c = _dot(x_ref[...].astype(BF16), win_ref[...])
    cq = c[:, :MLA_Q_LORA]
    ckv = c[:, MLA_Q_LORA:MLA_Q_LORA + MLA_KV_LORA]
    off = MLA_Q_LORA + MLA_KV_LORA
    k_rope = c[:, off:off + LANES] * ck_ref[...] + c[:, off + LANES:off + 2 * LANES] * sk_ref[...]
    cq = (cq * lax.rsqrt(jnp.mean(cq * cq, axis=-1, keepdims=True) + RMS_EPS) * qg_ref[...]).astype(BF16)
    ckv = (ckv * lax.rsqrt(jnp.mean(ckv * ckv, axis=-1, keepdims=True) + RMS_EPS) * kvg_ref[...]).astype(BF16)
    q1 = _dot(cq, wq1_ref[...])
    q2 = _dot(cq, wq2_ref[...])
    k1 = _dot(ckv, wk_ref[...])
    vt_ref[...] = _dot_nt(wvt_ref[...], ckv).astype(vt_ref.dtype)
    cos_q, sin_q = cq_ref[...], sq_ref[...]
    for h in range(MLA_HEADS):
        sl = slice(h * LANES, (h + 1) * LANES)
        q_ref[:, sl] = (q1[:, sl] * cos_q + q2[:, sl] * sin_q).astype(q_ref.dtype)
        k_ref[:, sl] = (k1[:, sl] + k_rope).astype(k_ref.dtype)


def _mla_proj(x, win, qg, kvg, wq1, wq2, wk, wvt, tabs, seq, tm):
    n, d = x.shape
    per_seq = seq // tm
    full = lambda a: pl.BlockSpec(a.shape, lambda i: (0,) * a.ndim)
    tab = pl.BlockSpec((tm, LANES), lambda i: (i % per_seq, 0))
    hw = MLA_HEADS * LANES
    dvt = wvt.shape[0]
    return pl.pallas_call(
        _mla_proj_kernel,
        grid=(n // tm,),
        in_specs=[pl.BlockSpec((tm, d), lambda i: (i, 0)), full(win), full(qg), full(kvg),
                  full(wq1), full(wq2), full(wk), full(wvt), tab, tab, tab, tab],
        out_specs=[pl.BlockSpec((tm, hw), lambda i: (i, 0)),
                   pl.BlockSpec((tm, hw), lambda i: (i, 0)),
                   pl.BlockSpec((None, dvt, tm), lambda i: (i // per_seq, 0, i % per_seq))],
        out_shape=[jax.ShapeDtypeStruct((n, hw), BF16),
                   jax.ShapeDtypeStruct((n, hw), BF16),
                   jax.ShapeDtypeStruct((n // seq, dvt, seq), BF16)],
        compiler_params=_cparams(("parallel",)),
        name="mla_proj",
    )(x, win, qg, kvg, wq1, wq2, wk, wvt, *tabs)


def _softmax_pv_t(s_t, vt):
    p = jnp.exp2(s_t - _col_reduce(s_t, jnp.max))
    l = _col_reduce(p, jnp.sum)
    return _dot(vt, p.astype(BF16)) * (1.0 / l)


def _col_reduce(x, op):
    rows, n = x.shape
    slab = COL_REDUCE_SLAB if rows % COL_REDUCE_SLAB == 0 else rows
    return op(op(x.reshape(rows // slab, slab, n), axis=0), axis=0, keepdims=True)


def _pipelined(items, scores, finish):
    s_next = scores(items[0])
    for idx, item in enumerate(items):
        s = s_next
        if idx + 1 < len(items):
            s_next = scores(items[idx + 1])
        finish(item, s)


def _mla_attn_kernel(q_ref, k_ref, vt_ref, o_ref):
    n_sub = q_ref.shape[0] // ATT_TQ
    items = [(r, hh) for r in range(n_sub) for hh in range(2)]
    cols = lambda hh: slice(hh * LANES, (hh + 1) * LANES)

    def scores(item):
        r, hh = item
        return _dot_nt(k_ref[:, cols(hh)], q_ref[r * ATT_TQ:(r + 1) * ATT_TQ, cols(hh)])

    def finish(item, s_t):
        r, hh = item
        vrows = slice(hh * MLA_DV, (hh + 1) * MLA_DV)
        o_ref[r, vrows, :] = _softmax_pv_t(s_t, vt_ref[vrows, :]).astype(o_ref.dtype)

    _pipelined(items, scores, finish)


def _mla_attention(q, k, vt, tq):
    b, s, _ = q.shape
    pairs = MLA_HEADS // 2
    pair_dv = 2 * MLA_DV
    return pl.pallas_call(
        _mla_attn_kernel,
        grid=(b, pairs, s // tq),
        in_specs=[pl.BlockSpec((None, tq, 2 * LANES), lambda bb, p, i: (bb, i, p)),
                  pl.BlockSpec((None, s, 2 * LANES), lambda bb, p, i: (bb, 0, p)),
                  pl.BlockSpec((None, pair_dv, s), lambda bb, p, i: (bb, p, 0))],
        out_specs=pl.BlockSpec((None, tq // ATT_TQ, pair_dv, ATT_TQ), lambda bb, p, i: (bb, i, p, 0)),
        out_shape=jax.ShapeDtypeStruct((b, s // ATT_TQ, MLA_HEADS * MLA_DV, ATT_TQ), BF16),
        compiler_params=_cparams(("parallel", "parallel", "parallel")),
        name="mla_attn",
    )(q, k, vt)


def _even_in_weight(w):
    hk, hv = GLA_HEADS * GLA_DK, GLA_HEADS * GLA_DV
    widths = (hk, hk, hv, hv, 2 * GLA_RANK, DIFF_HEADS * 2 * DIFF_DK, DIFF_HEADS * 2 * DIFF_DK,
              DIFF_HEADS * DIFF_DV)
    offs = [0]
    for wd_ in widths:
        offs.append(offs[-1] + wd_)
    piece = lambda j: w[:, offs[j]:offs[j + 1]]
    pad = jnp.zeros((w.shape[0], EVEN_COLS - COL_ALR - 2 * GLA_RANK), w.dtype)
    main = jnp.concatenate([piece(0), piece(1), piece(2), piece(3), piece(5), piece(6), piece(4), pad], axis=1)
    return main.astype(BF16), piece(7).T.astype(BF16)


def _rot_half_cols(w):
    half = MLA_ROPE // 2
    shp = w.shape
    g = w.reshape(shp[0], -1, MLA_ROPE)
    return jnp.concatenate([-g[..., half:], g[..., :half]], axis=-1).reshape(shp)


def _mla_weights(w_in, w_uq, w_ukv):
    d = w_in.shape[0]
    dq = MLA_NOPE + MLA_ROPE
    z = lambda rows, cols: jnp.zeros((rows, cols), F32)
    w_kr = w_in[:, MLA_Q_LORA + MLA_KV_LORA:]
    kr_blk = lambda m: jnp.concatenate([z(d, MLA_NOPE), m, z(d, LANES - dq)], axis=1)
    win = jnp.concatenate([w_in[:, :MLA_Q_LORA + MLA_KV_LORA], kr_blk(w_kr), kr_blk(_rot_half_cols(w_kr))],
                          axis=1).astype(BF16)
    uq = w_uq.reshape(MLA_Q_LORA, MLA_HEADS, dq)
    pad_q = jnp.zeros((MLA_Q_LORA, MLA_HEADS, LANES - dq), F32)
    wq1 = jnp.concatenate([uq, pad_q], axis=-1).reshape(MLA_Q_LORA, -1).astype(BF16)
    rot = _rot_half_cols(uq[..., MLA_NOPE:].reshape(MLA_Q_LORA, -1)).reshape(MLA_Q_LORA, MLA_HEADS, MLA_ROPE)
    wq2 = jnp.concatenate([jnp.zeros((MLA_Q_LORA, MLA_HEADS, MLA_NOPE), F32), rot, pad_q],
                          axis=-1).reshape(MLA_Q_LORA, -1).astype(BF16)
    ukv = w_ukv.reshape(MLA_KV_LORA, MLA_HEADS, MLA_NOPE + MLA_DV)
    wk = jnp.concatenate([ukv[..., :MLA_NOPE], jnp.zeros((MLA_KV_LORA, MLA_HEADS, LANES - MLA_NOPE), F32)],
                         axis=-1).reshape(MLA_KV_LORA, -1).astype(BF16)
    wvt = ukv[..., MLA_NOPE:].reshape(MLA_KV_LORA, -1).T.astype(BF16)
    return win, wq1, wq2, wk, wvt


def _rope_tables(seq):
    half = MLA_ROPE // 2
    inv = ROPE_THETA ** (-jnp.arange(half, dtype=F32) / half)
    ang = jnp.arange(seq, dtype=F32)[:, None] * inv[None, :]
    cos = jnp.concatenate([jnp.cos(ang), jnp.cos(ang)], axis=1)
    sin = jnp.concatenate([jnp.sin(ang), jnp.sin(ang)], axis=1)
    dq = MLA_NOPE + MLA_ROPE
    scale = dq ** -0.5 * LOG2E
    lay = lambda a, fill: jnp.concatenate([jnp.full((seq, MLA_NOPE), fill, F32), a,
                                           jnp.zeros((seq, LANES - dq), F32)], axis=1)
    return lay(cos, 1.0) * scale, lay(sin, 0.0) * scale, lay(cos, 0.0), lay(sin, 0.0)


def _routing_plan(info, counts, n_tokens):
    tm = MOE_TILE
    cnt = counts[0, :N_EXPERTS].astype(jnp.int32)
    padded = ((cnt + tm - 1) // tm) * tm
    ends = jnp.cumsum(padded)
    starts = ends - padded
    e = info[:, 0:TOP_K].astype(jnp.int32)
    rank = info[:, 4:4 + TOP_K].astype(jnp.int32)
    pos = starts[e] + rank
    n_tiles = (n_tokens * TOP_K) // tm + N_EXPERTS
    tile_start = jnp.arange(n_tiles, dtype=jnp.int32) * tm
    tile_expert = jnp.minimum(jnp.sum(tile_start[:, None] >= ends[None, :], axis=1), N_EXPERTS - 1)
    n_used = (ends[-1] // tm).reshape(1)
    pad_info = jnp.concatenate([starts + cnt, padded - cnt, n_used])
    return pos, tile_expert.astype(jnp.int32), n_used.astype(jnp.int32), pad_info.astype(jnp.int32), n_tiles * tm


def kernel(x, rel_bias_table, even_w_in, gla_gate_up, gla_gate_bias, gla_norm_gain, diff_lambda, diff_norm_gain, even_w_out, ffn_w_gate, ffn_w_up, ffn_w_down, odd_w_in, mla_q_norm_gain, mla_kv_norm_gain, mla_w_uq, mla_w_ukv, odd_w_out, router_w, moe_w_gate, moe_w_up, moe_w_down, ln_gain, ln_bias):
    b, s, d = x.shape
    n = b * s
    tm = min(512, s)
    assert s % tm == 0 and tm % ATT_TQ == 0 and s % GLA_GROUP == 0
    x2 = x.reshape(n, d)
    row = lambda v: v.reshape(1, -1)

    w_main, w_vdt = _even_in_weight(even_w_in[0])
    proj, vd_t = _in_proj(x2, w_main, w_vdt, s, tm)
    proj = proj.reshape(b, s, EVEN_COLS)
    gup = jnp.zeros((2, LANES, GLA_HEADS * GLA_DK), F32)
    for dd in range(2):
        gup = gup.at[dd, dd * GLA_RANK:(dd + 1) * GLA_RANK].set(gla_gate_up[0, dd])
    o_gla = _gla(proj, gup.astype(BF16), gla_gate_bias[0][:, None, :], row(gla_norm_gain[0]))
    lam_init = 0.8 - 0.6 * math.exp(-0.3 * 0)
    lf = diff_lambda[0]
    lam = (jnp.exp(jnp.sum(lf[0] * lf[1])) - jnp.exp(jnp.sum(lf[2] * lf[3])) + lam_init).reshape(1)
    o_diff_t = _diff_attention(proj, vd_t, _bias_band(rel_bias_table), lam,
                               diff_norm_gain[0].reshape(-1, 1), lam_init)
    hv = GLA_HEADS * GLA_DV
    w_out = even_w_out[0].astype(BF16)
    x2 = _mix_ffn(o_gla.reshape(n, hv), o_diff_t, w_out[:hv], w_out[hv:], x2,
                  row(ln_gain[0, 0]), row(ln_bias[0, 0]),
                  ffn_w_gate[0].astype(BF16), ffn_w_up[0].astype(BF16), ffn_w_down[0].astype(BF16),
                  row(ln_gain[0, 1]), row(ln_bias[0, 1]), s, tm)

    win, wq1, wq2, wk, wvt = _mla_weights(odd_w_in[0], mla_w_uq[0], mla_w_ukv[0])
    q, k, v_t = _mla_proj(x2, win, row(mla_q_norm_gain[0]), row(mla_kv_norm_gain[0]), wq1, wq2, wk, wvt,
                          _rope_tables(s), s, tm)
    o_t = _mla_attention(q.reshape(b, s, -1), k.reshape(b, s, -1), v_t, min(MLA_TQ, s))
    rw = jnp.concatenate([router_w[0], jnp.zeros((d, LANES - N_EXPERTS), F32)], axis=1)
    rw_hi = rw.astype(BF16)
    rw2 = jnp.stack([rw_hi, (rw - rw_hi.astype(F32)).astype(BF16)])
    xn, info, counts = _outproj_route(o_t, odd_w_out[0].astype(BF16), x2,
                                      row(ln_gain[1, 0]), row(ln_bias[1, 0]), rw2, s, tm)
    pos, tile_expert, n_used, pad_info, n_slots = _routing_plan(info, counts, n)
    pos_t = pos.reshape(n // tm, tm, TOP_K).transpose(0, 2, 1)
    x_sorted = _scatter_rows(xn, pos_t, pad_info, n_slots, tm)
    y_sorted = _moe_ffn(x_sorted, tile_expert, n_used, moe_w_gate[0].astype(BF16),
                        moe_w_up[0].astype(BF16), moe_w_down[0].astype(BF16))
    out = _combine_ln(xn, info, pos_t, y_sorted, row(ln_gain[1, 1]), row(ln_bias[1, 1]), tm)
    return out.reshape(b, s, d)
```

```python
import functools
import math

import jax
import jax.numpy as jnp
from jax import lax
from jax.experimental import pallas as pl
from jax.experimental.pallas import tpu as pltpu

F32 = jnp.float32
BF16 = jnp.bfloat16

LANES = 128
V7X_VMEM_BYTES = 64 * 1024 * 1024
VMEM_LIMIT = V7X_VMEM_BYTES - 8 * 1024 * 1024

D_MODEL = 1024
GLA_HEADS, GLA_DK, GLA_DV = 4, 64, 128
GLA_RANK, GLA_TEMP, GLA_CHUNK = 16, 16.0, 64
DIFF_HEADS, DIFF_DK, DIFF_DV = 4, 64, 128
N_BUCKETS, MAX_DISTANCE = 32, 128
MLA_HEADS, MLA_Q_LORA, MLA_KV_LORA = 16, 256, 128
MLA_NOPE, MLA_ROPE, MLA_DV = 64, 32, 64
ROPE_THETA = 10000.0
D_FF, N_EXPERTS, TOP_K = 2816, 8, 2
DEPTH = 2
ALPHA = (2 * DEPTH) ** 0.25
LN_EPS, RMS_EPS = 1e-5, 1e-6

COL_QK_G, COL_V_G, COL_G_G = 0, 512, 1024
COL_Q_D, COL_K_D, COL_ALR = 1536, 2048, 2560
EVEN_COLS = 2688

ATT_KC = 128
ATT_TQ = 256
MLA_TQ = 2048
COL_REDUCE_SLAB = 128
LOG2E = math.log2(math.e)

GLA_GROUP = 256
FF_CHUNK = 256
MOE_TILE = 512
DMA_UNROLL = 8


def _cparams(sem):
    return pltpu.CompilerParams(dimension_semantics=sem, vmem_limit_bytes=VMEM_LIMIT)


def _layer_norm(y, g, b):
    mu = jnp.mean(y, axis=-1, keepdims=True)
    d = y - mu
    var = jnp.mean(d * d, axis=-1, keepdims=True)
    return d * lax.rsqrt(var + LN_EPS) * g + b


def _split3(x):
    h1 = x.astype(BF16)
    r1 = x - h1.astype(F32)
    h2 = r1.astype(BF16)
    h3 = (r1 - h2.astype(F32)).astype(BF16)
    return h1, h2, h3


def _dot(a, b):
    return jnp.dot(a, b, preferred_element_type=F32)


def _dot_nt(a, b):
    return lax.dot_general(a, b, (((1,), (1,)), ((), ())), preferred_element_type=F32)


def _dot_tn(a, b):
    return lax.dot_general(a, b, (((0,), (0,)), ((), ())), preferred_element_type=F32)


def _in_proj_kernel(x_ref, w_ref, wvt_ref, o_ref, vt_ref):
    xb = x_ref[...].astype(BF16)
    o_ref[...] = _dot(xb, w_ref[...]).astype(o_ref.dtype)
    vt_ref[...] = _dot_nt(wvt_ref[...], xb).astype(vt_ref.dtype)


def _in_proj(x, w, wvt, seq, tm):
    n, k = x.shape
    m = w.shape[1]
    dvt = wvt.shape[0]
    per_seq = seq // tm
    return pl.pallas_call(
        _in_proj_kernel,
        grid=(n // tm,),
        in_specs=[pl.BlockSpec((tm, k), lambda i: (i, 0)),
                  pl.BlockSpec((k, m), lambda i: (0, 0)),
                  pl.BlockSpec((dvt, k), lambda i: (0, 0))],
        out_specs=[pl.BlockSpec((tm, m), lambda i: (i, 0)),
                   pl.BlockSpec((None, dvt, tm), lambda i: (i // per_seq, 0, i % per_seq))],
        out_shape=[jax.ShapeDtypeStruct((n, m), BF16),
                   jax.ShapeDtypeStruct((n // seq, dvt, seq), BF16)],
        compiler_params=_cparams(("parallel",)),
        name="in_proj",
    )(x, w, wvt)


def _gla_kernel(qk_ref, v_ref, g_ref, alr_ref, gup_ref, gbias_ref, gain_ref, o_ref,
                la_ref, qdec_ref, kdec_ref, decay_ref, acc_ref, st_ref):
    seq = qk_ref.shape[0]
    hk = GLA_HEADS * GLA_DK
    hv = GLA_HEADS * GLA_DV
    grp = GLA_GROUP
    c = GLA_CHUNK

    alr = alr_ref[...]
    for d in range(2):
        logits = _dot(alr, gup_ref[d]) + gbias_ref[d]
        log_sig = jnp.minimum(logits, 0.0) - jnp.log(1.0 + jnp.exp(-jnp.abs(logits)))
        la_ref[d] = log_sig / GLA_TEMP

    row = lax.broadcasted_iota(jnp.int32, (grp, grp), 0)
    col = lax.broadcasted_iota(jnp.int32, (grp, grp), 1)
    same = (row // c) == (col // c)
    tri = [same & (row >= col), same & (row <= col)]
    cum_inc = [jnp.where(t, 1.0, 0.0).astype(BF16) for t in tri]
    cum_rem = [jnp.where(same & (row < col), 1.0, 0.0).astype(BF16),
               jnp.where(same & (row > col), 1.0, 0.0).astype(BF16)]
    lane_k = lax.broadcasted_iota(jnp.int32, (grp, hk), 1) // GLA_DK
    scale = GLA_DK ** -0.5

    def group_body(r, carry):
        rows = pl.ds(pl.multiple_of(r * grp, grp), grp)
        q = qk_ref[rows, 0:hk].astype(F32)
        k = qk_ref[rows, hk:2 * hk].astype(F32)
        v = v_ref[rows, :]
        o_heads = [jnp.zeros((grp, GLA_DV), F32) for _ in range(GLA_HEADS)]
        for d in range(2):
            parts = _split3(la_ref[d, rows, :])
            bcum = sum(_dot(cum_inc[d], p) for p in parts)
            brem = sum(_dot(cum_rem[d], p) for p in parts)
            q_dec = q * jnp.exp(bcum) * scale
            k_inv = (k * jnp.exp(-bcum)).astype(BF16)
            qdec_ref[d, rows, :] = q_dec.astype(BF16)
            kdec_ref[d, rows, :] = (k * jnp.exp(brem)).astype(BF16)
            decay_ref[d, rows, :] = jnp.exp(bcum + brem)
            for h in range(GLA_HEADS):
                q_h = jnp.where(lane_k == h, q_dec, 0.0).astype(BF16)
                attn = jnp.where(tri[d], _dot_nt(q_h, k_inv), 0.0)
                o_heads[h] = o_heads[h] + _dot(attn.astype(BF16), v[:, h * GLA_DV:(h + 1) * GLA_DV])
        for h in range(GLA_HEADS):
            acc_ref[rows, h * GLA_DV:(h + 1) * GLA_DV] = o_heads[h]
        return carry

    lax.fori_loop(0, seq // grp, group_body, 0)

    srow = lax.broadcasted_iota(jnp.int32, (hv, hk), 0) // GLA_DV
    scol = lax.broadcasted_iota(jnp.int32, (hv, hk), 1) // GLA_DK
    head_mask = srow == scol
    n_chunks = seq // c
    st_ref[...] = jnp.zeros_like(st_ref)

    def chunk_body(i, carry):
        for d in range(2):
            ci = i if d == 0 else n_chunks - 1 - i
            rows = pl.ds(pl.multiple_of(ci * c, c), c)
            state = st_ref[d]
            acc_ref[rows, :] += _dot_nt(qdec_ref[d, rows, :], state.astype(BF16))
            kv = _dot_tn(v_ref[rows, :], kdec_ref[d, rows, :])
            decay = decay_ref[d, pl.ds(pl.multiple_of(ci * c, c), 1), :]
            st_ref[d] = state * decay + jnp.where(head_mask, kv, 0.0)
        return carry

    lax.fori_loop(0, n_chunks, chunk_body, 0, unroll=2)

    gain = gain_ref[...]
    for h in range(GLA_HEADS):
        sl = slice(h * GLA_DV, (h + 1) * GLA_DV)
        o = acc_ref[:, sl]
        y = o * lax.rsqrt(jnp.mean(o * o, axis=-1, keepdims=True) + RMS_EPS) * gain
        gate = g_ref[:, sl].astype(F32)
        o_ref[:, sl] = (y * (gate * jax.nn.sigmoid(gate))).astype(o_ref.dtype)


def _gla(proj, gup, gbias, gain):
    b, s, _ = proj.shape
    hk, hv = GLA_HEADS * GLA_DK, GLA_HEADS * GLA_DV
    blk = lambda width, col: pl.BlockSpec((None, s, width), lambda i: (i, 0, col // width))
    return pl.pallas_call(
        _gla_kernel,
        grid=(b,),
        in_specs=[blk(2 * hk, COL_QK_G), blk(hv, COL_V_G), blk(hv, COL_G_G), blk(LANES, COL_ALR),
                  pl.BlockSpec((2, LANES, hk), lambda i: (0, 0, 0)),
                  pl.BlockSpec((2, 1, hk), lambda i: (0, 0, 0)),
                  pl.BlockSpec((1, GLA_DV), lambda i: (0, 0))],
        out_specs=pl.BlockSpec((None, s, hv), lambda i: (i, 0, 0)),
        out_shape=jax.ShapeDtypeStruct((b, s, hv), BF16),
        scratch_shapes=[pltpu.VMEM((2, s, hk), F32),
                        pltpu.VMEM((2, s, hk), BF16),
                        pltpu.VMEM((2, s, hk), BF16),
                        pltpu.VMEM((2, s, hk), F32),
                        pltpu.VMEM((s, hv), F32),
                        pltpu.VMEM((2, hv, hk), F32)],
        compiler_params=_cparams(("parallel",)),
        name="gla",
    )(proj, proj, proj, proj, gup, gbias, gain)


J_BELOW = (-MAX_DISTANCE - ATT_KC + 1) // ATT_KC
J_ABOVE = -(-(MAX_DISTANCE + ATT_TQ - 1) // ATT_KC)
N_BAND = J_ABOVE - J_BELOW + 1
assert ATT_KC * J_BELOW + ATT_KC - 1 <= -MAX_DISTANCE
assert ATT_KC * J_ABOVE - (ATT_TQ - 1) >= MAX_DISTANCE


def _band_buckets():
    half = N_BUCKETS // 2
    max_exact = half // 2
    j = (jnp.arange(N_BAND, dtype=jnp.int32) + J_BELOW)[:, None, None]
    rel = (j * ATT_KC + jnp.arange(ATT_KC, dtype=jnp.int32)[None, :, None]
           - jnp.arange(ATT_TQ, dtype=jnp.int32)[None, None, :])
    bucket = jnp.where(rel > 0, half, 0).astype(jnp.int32)
    n = jnp.abs(rel)
    n_large = max_exact + (jnp.log(jnp.maximum(n, max_exact).astype(F32) / max_exact)
                           / math.log(MAX_DISTANCE / max_exact) * (half - max_exact)).astype(jnp.int32)
    n_large = jnp.minimum(n_large, half - 1)
    return bucket + jnp.where(n < max_exact, n, n_large)


def _band_kernel(tab_ref, bucket_ref, o_ref, *, n_heads):
    h = pl.program_id(0)
    for t in range(N_BAND):
        bucket = bucket_ref[t]
        val = jnp.zeros(bucket.shape, F32)
        for bkt in range(N_BUCKETS):
            val = jnp.where(bucket == bkt, tab_ref[bkt * n_heads + h], val)
        o_ref[t] = val * LOG2E


def _bias_band(table):
    n_heads = table.shape[1]
    bucket = jnp.bitwise_and(_band_buckets(), N_BUCKETS - 1)
    return pl.pallas_call(
        functools.partial(_band_kernel, n_heads=n_heads),
        grid=(n_heads,),
        in_specs=[pl.BlockSpec(memory_space=pltpu.SMEM),
                  pl.BlockSpec((N_BAND, ATT_KC, ATT_TQ), lambda h: (0, 0, 0))],
        out_specs=pl.BlockSpec((None, N_BAND, ATT_KC, ATT_TQ), lambda h: (h, 0, 0, 0)),
        out_shape=jax.ShapeDtypeStruct((n_heads, N_BAND, ATT_KC, ATT_TQ), F32),
        compiler_params=_cparams(("parallel",)),
        name="bias_band",
    )(table.reshape(-1), bucket)


def _diff_kernel(lam_ref, q_ref, k_ref, vt_ref, band_ref, gain_ref, o_ref, *, out_scale):
    seq = q_ref.shape[0]
    n_chunks = seq // ATT_KC
    lane = lax.broadcasted_iota(jnp.int32, (ATT_TQ, 2 * DIFF_DK), 1)
    scale = DIFF_DK ** -0.5 * LOG2E
    items = [(r, mp) for r in range(seq // ATT_TQ) for mp in range(2)]
    first_map = {}

    def scores(item):
        r, mp = item
        q = (q_ref[r * ATT_TQ:(r + 1) * ATT_TQ, :].astype(F32) * scale).astype(BF16)
        q = jnp.where((lane >= DIFF_DK) == (mp == 1), q, jnp.zeros_like(q))
        tiles = [band_ref[min(max(c - (ATT_TQ // ATT_KC) * r - J_BELOW, 0), N_BAND - 1)] for c in range(n_chunks)]
        return _dot_nt(k_ref[...], q) + jnp.concatenate(tiles, axis=0)

    def finish(item, s_t):
        r, mp = item
        o_t = _softmax_pv_t(s_t, vt_ref[...])
        if mp == 0:
            first_map[r] = o_t
            return
        o = first_map.pop(r) - lam_ref[0] * o_t
        y = o * lax.rsqrt(jnp.mean(o * o, axis=0, keepdims=True) + RMS_EPS) * gain_ref[...]
        o_ref[r] = (y * out_scale).astype(o_ref.dtype)

    _pipelined(items, scores, finish)


def _diff_attention(proj, vt, band, lam, gain_col, lam_init):
    b, s, _ = proj.shape
    h = DIFF_HEADS
    w = 2 * DIFF_DK
    assert w == LANES
    return pl.pallas_call(
        functools.partial(_diff_kernel, out_scale=1.0 - lam_init),
        grid=(h, b),
        in_specs=[pl.BlockSpec(memory_space=pltpu.SMEM),
                  pl.BlockSpec((None, s, w), lambda hh, bb: (bb, 0, COL_Q_D // w + hh)),
                  pl.BlockSpec((None, s, w), lambda hh, bb: (bb, 0, COL_K_D // w + hh)),
                  pl.BlockSpec((None, DIFF_DV, s), lambda hh, bb: (bb, hh, 0)),
                  pl.BlockSpec((None, N_BAND, ATT_KC, ATT_TQ), lambda hh, bb: (hh, 0, 0, 0)),
                  pl.BlockSpec((DIFF_DV, 1), lambda hh, bb: (0, 0))],
        out_specs=pl.BlockSpec((None, s // ATT_TQ, DIFF_DV, ATT_TQ), lambda hh, bb: (bb, 0, hh, 0)),
        out_shape=jax.ShapeDtypeStruct((b, s // ATT_TQ, h * DIFF_DV, ATT_TQ), BF16),
        compiler_params=_cparams(("parallel", "parallel")),
        name="diff_attn",
    )(lam, proj, proj, vt, band, gain_col)


def _residual_ln(x_ref, o_ref, g_ref, b_ref, row_pairs, col_pairs):
    for j in range(x_ref.shape[0] // ATT_TQ):
        rows = slice(j * ATT_TQ, (j + 1) * ATT_TQ)
        h = ALPHA * x_ref[rows, :]
        for a_ref, w_ref in row_pairs:
            h = h + _dot(a_ref[rows, :], w_ref[...])
        for a_ref, w_ref in col_pairs:
            h = h + _dot_tn(a_ref[j], w_ref[...])
        o_ref[rows, :] = _layer_norm(h, g_ref[...], b_ref[...])


def _tile_specs(n, d, seq, tm):
    per_seq = seq // tm
    rows = lambda width: pl.BlockSpec((tm, width), lambda i: (i, 0))
    cols = lambda k: pl.BlockSpec((None, tm // ATT_TQ, k, ATT_TQ), lambda i: (i // per_seq, i % per_seq, 0, 0))
    full = lambda a: pl.BlockSpec(a.shape, lambda i: (0,) * a.ndim)
    return rows, cols, full


def _outproj_route_kernel(at_ref, w_ref, x_ref, g_ref, b_ref, rw_ref, o_ref, info_ref, cnt_ref, carry_ref):
    i = pl.program_id(0)
    tm = x_ref.shape[0]

    @pl.when(i == 0)
    def _():
        carry_ref[...] = jnp.zeros_like(carry_ref)

    _residual_ln(x_ref, o_ref, g_ref, b_ref, [], [(at_ref, w_ref)])
    xn = o_ref[...]

    x_hi = xn.astype(BF16)
    x_lo = (xn - x_hi.astype(F32)).astype(BF16)
    logits = _dot(x_hi, rw_ref[0]) + (_dot(x_hi, rw_ref[1]) + _dot(x_lo, rw_ref[0]))
    lane = lax.broadcasted_iota(jnp.int32, logits.shape, 1).astype(F32)
    neg = jnp.float32(-jnp.inf)
    logits = jnp.where(lane < N_EXPERTS, logits, neg)
    v1 = jnp.max(logits, axis=-1, keepdims=True)
    e1 = jnp.min(jnp.where(logits == v1, lane, float(LANES)), axis=-1, keepdims=True)
    rest = jnp.where(lane == e1, neg, logits)
    v2 = jnp.max(rest, axis=-1, keepdims=True)
    e2 = jnp.min(jnp.where(rest == v2, lane, float(LANES)), axis=-1, keepdims=True)
    t = jnp.exp(v2 - v1)
    w1 = 1.0 / (1.0 + t)
    w2 = t / (1.0 + t)

    onehot = jnp.where((lane == e1) | (lane == e2), 1.0, 0.0)
    row = lax.broadcasted_iota(jnp.int32, (tm, tm), 0)
    col = lax.broadcasted_iota(jnp.int32, (tm, tm), 1)
    before = jnp.where(row > col, 1.0, 0.0).astype(BF16)
    prior = carry_ref[...] + _dot(before, onehot.astype(BF16))
    r1 = jnp.sum(jnp.where(lane == e1, prior, 0.0), axis=-1, keepdims=True)
    r2 = jnp.sum(jnp.where(lane == e2, prior, 0.0), axis=-1, keepdims=True)
    carry_ref[...] += jnp.sum(onehot, axis=0, keepdims=True)
    cnt_ref[...] = jnp.broadcast_to(carry_ref[...], cnt_ref.shape)

    info = jnp.zeros(logits.shape, F32)
    for idx, val in enumerate((e1, e2, w1, w2, r1, r2)):
        info = jnp.where(lane == idx, val, info)
    info_ref[...] = info


def _outproj_route(at, w, x, g, b, router_w2, seq, tm):
    n, d = x.shape
    rows, cols, full = _tile_specs(n, d, seq, tm)
    return pl.pallas_call(
        _outproj_route_kernel,
        grid=(n // tm,),
        in_specs=[cols(at.shape[2]), full(w), rows(d), full(g), full(b), full(router_w2)],
        out_specs=[rows(d), rows(LANES),
                   pl.BlockSpec((8, LANES), lambda i: (0, 0))],
        out_shape=[jax.ShapeDtypeStruct((n, d), F32),
                   jax.ShapeDtypeStruct((n, LANES), F32),
                   jax.ShapeDtypeStruct((8, LANES), F32)],
        scratch_shapes=[pltpu.VMEM((1, LANES), F32)],
        compiler_params=_cparams(("arbitrary",)),
        name="outproj_route",
    )(at, w, x, g, b, router_w2)


def _swiglu_acc(xb, wg_ref, wu_ref, wd_ref, acc_ref):
    n_chunks = wg_ref.shape[-1] // FF_CHUNK
    for c in range(n_chunks):
        cols = slice(c * FF_CHUNK, (c + 1) * FF_CHUNK)
        gate = _dot(xb, wg_ref[:, cols])
        up = _dot(xb, wu_ref[:, cols])
        hidden = (gate * jax.nn.sigmoid(gate) * up).astype(BF16)
        part = _dot(hidden, wd_ref[cols, :])
        if c == 0:
            acc_ref[...] = part
        else:
            acc_ref[...] += part


def _mix_ffn_kernel(a_ref, at_ref, wa_ref, wt_ref, x_ref, g1_ref, b1_ref, wg_ref, wu_ref, wd_ref,
                    g2_ref, b2_ref, o_ref, h_ref):
    _residual_ln(x_ref, h_ref, g1_ref, b1_ref, [(a_ref, wa_ref)], [(at_ref, wt_ref)])
    h = h_ref[...]
    _swiglu_acc(h.astype(BF16), wg_ref, wu_ref, wd_ref, o_ref)
    o_ref[...] = _layer_norm(ALPHA * h + o_ref[...], g2_ref[...], b2_ref[...])


def _mix_ffn(a, at, wa, wt, x, g1, b1, wg, wu, wd, g2, b2, seq, tm):
    n, d = x.shape
    rows, cols, _ = _tile_specs(n, d, seq, tm)
    once = lambda arr: pl.BlockSpec(arr.shape, lambda i: (0,) * arr.ndim, pipeline_mode=pl.Buffered(1))
    return pl.pallas_call(
        _mix_ffn_kernel,
        grid=(n // tm,),
        in_specs=[rows(a.shape[1]), cols(at.shape[2]), once(wa), once(wt), rows(d), once(g1), once(b1),
                  once(wg), once(wu), once(wd), once(g2), once(b2)],
        out_specs=rows(d),
        out_shape=jax.ShapeDtypeStruct((n, d), F32),
        scratch_shapes=[pltpu.VMEM((tm, d), F32)],
        compiler_params=_cparams(("parallel",)),
        name="mix_ffn",
    )(a, at, wa, wt, x, g1, b1, wg, wu, wd, g2, b2)


def _moe_kernel(te_ref, nu_ref, x_ref, wg_ref, wu_ref, wd_ref, o_ref):
    t = pl.program_id(0)

    @pl.when(t < nu_ref[0])
    def _():
        _swiglu_acc(x_ref[...].astype(BF16), wg_ref, wu_ref, wd_ref, o_ref)

    @pl.when(t >= nu_ref[0])
    def _():
        o_ref[...] = jnp.zeros_like(o_ref)


def _moe_ffn(x_sorted, tile_expert, n_used, wg, wu, wd):
    p, d = x_sorted.shape
    f = wg.shape[2]
    tm = MOE_TILE
    grid_spec = pltpu.PrefetchScalarGridSpec(
        num_scalar_prefetch=2,
        grid=(p // tm,),
        in_specs=[pl.BlockSpec((tm, d), lambda t, te, nu: (jnp.minimum(t, nu[0] - 1), 0)),
                  pl.BlockSpec((None, d, f), lambda t, te, nu: (te[t], 0, 0)),
                  pl.BlockSpec((None, d, f), lambda t, te, nu: (te[t], 0, 0)),
                  pl.BlockSpec((None, f, d), lambda t, te, nu: (te[t], 0, 0))],
        out_specs=pl.BlockSpec((tm, d), lambda t, te, nu: (t, 0)),
    )
    return pl.pallas_call(
        _moe_kernel,
        grid_spec=grid_spec,
        out_shape=jax.ShapeDtypeStruct((p, d), F32),
        compiler_params=_cparams(("arbitrary",)),
        name="moe_ffn",
    )(tile_expert, n_used, x_sorted, wg, wu, wd)


def _scatter_kernel(pos_ref, pad_ref, x_ref, o_hbm, zero_ref, sem, zsem):
    i = pl.program_id(0)
    tm = x_ref.shape[0]
    n_tiles = o_hbm.shape[0] // MOE_TILE

    @pl.when(i == 0)
    def _():
        zero_ref[...] = jnp.zeros_like(zero_ref)
        zero_row = zero_ref.at[pl.ds(0, 1)]
        for e in range(N_EXPERTS):
            start, count = pad_ref[e], pad_ref[N_EXPERTS + e]

            def fill(r, carry):
                pltpu.make_async_copy(zero_row, o_hbm.at[pl.ds(start + r, 1)], zsem).start()
                return carry

            lax.fori_loop(0, count, fill, 0)

            def drain(r, carry):
                pltpu.make_async_copy(zero_row, o_hbm.at[pl.ds(start + r, 1)], zsem).wait()
                return carry

            lax.fori_loop(0, count, drain, 0)

        def fill_tile(t, carry):
            dst = o_hbm.at[pl.ds(pl.multiple_of(t * MOE_TILE, MOE_TILE), MOE_TILE)]
            cp = pltpu.make_async_copy(zero_ref, dst, zsem)
            cp.start()
            cp.wait()
            return carry

        lax.fori_loop(pad_ref[2 * N_EXPERTS], n_tiles, fill_tile, 0)

    def issue(r, carry):
        for kk in range(TOP_K):
            dst = pos_ref[0, kk, r]
            pltpu.make_async_copy(x_ref.at[pl.ds(r, 1)], o_hbm.at[pl.ds(dst, 1)], sem).start(priority=kk)
        return carry

    lax.fori_loop(0, tm, issue, 0, unroll=DMA_UNROLL)

    for kk in range(TOP_K):
        pltpu.make_async_copy(x_ref, o_hbm.at[pl.ds(0, tm)], sem).wait()


def _scatter_rows(x, pos, pad_info, n_slots, tm):
    n, d = x.shape
    return pl.pallas_call(
        _scatter_kernel,
        grid=(n // tm,),
        in_specs=[pl.BlockSpec((1, TOP_K, tm), lambda i: (i, 0, 0), memory_space=pltpu.SMEM),
                  pl.BlockSpec(memory_space=pltpu.SMEM),
                  pl.BlockSpec((tm, d), lambda i: (i, 0))],
        out_specs=pl.BlockSpec(memory_space=pl.ANY),
        out_shape=jax.ShapeDtypeStruct((n_slots, d), F32),
        scratch_shapes=[pltpu.VMEM((MOE_TILE, d), F32),
                        pltpu.SemaphoreType.DMA(()),
                        pltpu.SemaphoreType.DMA(())],
        compiler_params=_cparams(("arbitrary",)),
        name="scatter_rows",
    )(pos, pad_info, x)


def _combine_kernel(pos_ref, next_ref, x_ref, info_ref, g_ref, b_ref, y_hbm, o_ref, buf_ref, sem):
    i = pl.program_id(0)
    tm = x_ref.shape[0]
    slot = i % 2

    def gather(idx_ref, s):
        def issue(r, carry):
            for kk in range(TOP_K):
                src = idx_ref[0, kk, r]
                pltpu.make_async_copy(y_hbm.at[pl.ds(src, 1)], buf_ref.at[s, kk, pl.ds(r, 1)],
                                      sem.at[s]).start(priority=kk)
            return carry
        lax.fori_loop(0, tm, issue, 0, unroll=DMA_UNROLL)

    @pl.when(i == 0)
    def _():
        gather(pos_ref, slot)

    @pl.when(i + 1 < pl.num_programs(0))
    def _():
        gather(next_ref, 1 - slot)

    for kk in range(TOP_K):
        pltpu.make_async_copy(y_hbm.at[pl.ds(0, tm)], buf_ref.at[slot, kk], sem.at[slot]).wait()

    info = info_ref[...]
    w1 = info[:, 2:3]
    w2 = info[:, 3:4]
    f = w1 * buf_ref[slot, 0] + w2 * buf_ref[slot, 1]
    o_ref[...] = _layer_norm(ALPHA * x_ref[...] + f, g_ref[...], b_ref[...])


def _combine_ln(x, info, pos, y_sorted, g, b, tm):
    n, d = x.shape
    last = n // tm - 1
    return pl.pallas_call(
        _combine_kernel,
        grid=(n // tm,),
        in_specs=[pl.BlockSpec((1, TOP_K, tm), lambda i: (i, 0, 0), memory_space=pltpu.SMEM),
                  pl.BlockSpec((1, TOP_K, tm), lambda i: (jnp.minimum(i + 1, last), 0, 0), memory_space=pltpu.SMEM),
                  pl.BlockSpec((tm, d), lambda i: (i, 0)),
                  pl.BlockSpec((tm, LANES), lambda i: (i, 0)),
                  pl.BlockSpec((1, d), lambda i: (0, 0)),
                  pl.BlockSpec((1, d), lambda i: (0, 0)),
                  pl.BlockSpec(memory_space=pl.ANY)],
        out_specs=pl.BlockSpec((tm, d), lambda i: (i, 0)),
        out_shape=jax.ShapeDtypeStruct((n, d), F32),
        scratch_shapes=[pltpu.VMEM((2, TOP_K, tm, d), F32),
                        pltpu.SemaphoreType.DMA((2,))],
        compiler_params=_cparams(("arbitrary",)),
        name="combine_ln",
    )(pos, pos, x, info, g, b, y_sorted)


def _mla_proj_kernel(x_ref, win_ref, qg_ref, kvg_ref, wq1_ref, wq2_ref, wk_ref, wvt_ref,
                     cq_ref, sq_ref, ck_ref, sk_ref, q_ref, k_ref, vt_ref):
    c = _dot(x_ref[...].astype(BF16), win_ref[...])
    cq = c[:, :MLA_Q_LORA]
    ckv = c[:, MLA_Q_LORA:MLA_Q_LORA + MLA_KV_LORA]
    off = MLA_Q_LORA + MLA_KV_LORA
    k_rope = c[:, off:off + LANES] * ck_ref[...] + c[:, off + LANES:off + 2 * LANES] * sk_ref[...]
    cq = (cq * lax.rsqrt(jnp.mean(cq * cq, axis=-1, keepdims=True) + RMS_EPS) * qg_ref[...]).astype(BF16)
    ckv = (ckv * lax.rsqrt(jnp.mean(ckv * ckv, axis=-1, keepdims=True) + RMS_EPS) * kvg_ref[...]).astype(BF16)
    q1 = _dot(cq, wq1_ref[...])
    q2 = _dot(cq, wq2_ref[...])
    k1 = _dot(ckv, wk_ref[...])
    vt_ref[...] = _dot_nt(wvt_ref[...], ckv).astype(vt_ref.dtype)
    cos_q, sin_q = cq_ref[...], sq_ref[...]
    for h in range(MLA_HEADS):
        sl = slice(h * LANES, (h + 1) * LANES)
        q_ref[:, sl] = (q1[:, sl] * cos_q + q2[:, sl] * sin_q).astype(q_ref.dtype)
        k_ref[:, sl] = (k1[:, sl] + k_rope).astype(k_ref.dtype)


def _mla_proj(x, win, qg, kvg, wq1, wq2, wk, wvt, tabs, seq, tm):
    n, d = x.shape
    per_seq = seq // tm
    full = lambda a: pl.BlockSpec(a.shape, lambda i: (0,) * a.ndim)
    tab = pl.BlockSpec((tm, LANES), lambda i: (i % per_seq, 0))
    hw = MLA_HEADS * LANES
    dvt = wvt.shape[0]
    return pl.pallas_call(
        _mla_proj_kernel,
        grid=(n // tm,),
        in_specs=[pl.BlockSpec((tm, d), lambda i: (i, 0)), full(win), full(qg), full(kvg),
                  full(wq1), full(wq2), full(wk), full(wvt), tab, tab, tab, tab],
        out_specs=[pl.BlockSpec((tm, hw), lambda i: (i, 0)),
                   pl.BlockSpec((tm, hw), lambda i: (i, 0)),
                   pl.BlockSpec((None, dvt, tm), lambda i: (i // per_seq, 0, i % per_seq))],
        out_shape=[jax.ShapeDtypeStruct((n, hw), BF16),
                   jax.ShapeDtypeStruct((n, hw), BF16),
                   jax.ShapeDtypeStruct((n // seq, dvt, seq), BF16)],
        compiler_params=_cparams(("parallel",)),
        name="mla_proj",
    )(x, win, qg, kvg, wq1, wq2, wk, wvt, *tabs)


def _softmax_pv_t(s_t, vt):
    p = jnp.exp2(s_t - _col_reduce(s_t, jnp.max))
    l = _col_reduce(p, jnp.sum)
    return _dot(vt, p.astype(BF16)) * (1.0 / l)


def _col_reduce(x, op):
    rows, n = x.shape
    slab = COL_REDUCE_SLAB if rows % COL_REDUCE_SLAB == 0 else rows
    return op(op(x.reshape(rows // slab, slab, n), axis=0), axis=0, keepdims=True)


def _pipelined(items, scores, finish):
    s_next = scores(items[0])
    for idx, item in enumerate(items):
        s = s_next
        if idx + 1 < len(items):
            s_next = scores(items[idx + 1])
        finish(item, s)


def _mla_attn_kernel(q_ref, k_ref, vt_ref, *refs):
    n_side = (len(refs) - 1) // 2
    o_ref = refs[n_side]
    for w_ref, wb_ref in zip(refs[:n_side], refs[n_side + 1:]):
        wb_ref[...] = w_ref[...].astype(wb_ref.dtype)
    n_sub = q_ref.shape[0] // ATT_TQ
    items = [(r, hh) for r in range(n_sub) for hh in range(2)]
    cols = lambda hh: slice(hh * LANES, (hh + 1) * LANES)

    def scores(item):
        r, hh = item
        return _dot_nt(k_ref[:, cols(hh)], q_ref[r * ATT_TQ:(r + 1) * ATT_TQ, cols(hh)])

    def finish(item, s_t):
        r, hh = item
        vrows = slice(hh * MLA_DV, (hh + 1) * MLA_DV)
        o_ref[r, vrows, :] = _softmax_pv_t(s_t, vt_ref[vrows, :]).astype(o_ref.dtype)

    _pipelined(items, scores, finish)


def _mla_attention(q, k, vt, tq, side_f32):
    b, s, _ = q.shape
    pairs = MLA_HEADS // 2
    pair_dv = 2 * MLA_DV
    n_q = s // tq
    n_steps = b * pairs * n_q
    step = lambda bb, p, i: (bb * pairs + p) * n_q + i
    side_specs = []
    for w in side_f32:
        assert w.shape[0] % (n_steps * 16) == 0, w.shape
        side_specs.append(pl.BlockSpec((w.shape[0] // n_steps, w.shape[1]), lambda bb, p, i: (step(bb, p, i), 0)))
    outs = pl.pallas_call(
        _mla_attn_kernel,
        grid=(b, pairs, n_q),
        in_specs=[pl.BlockSpec((None, tq, 2 * LANES), lambda bb, p, i: (bb, i, p)),
                  pl.BlockSpec((None, s, 2 * LANES), lambda bb, p, i: (bb, 0, p)),
                  pl.BlockSpec((None, pair_dv, s), lambda bb, p, i: (bb, p, 0))] + side_specs,
        out_specs=[pl.BlockSpec((None, tq // ATT_TQ, pair_dv, ATT_TQ), lambda bb, p, i: (bb, i, p, 0))] + side_specs,
        out_shape=[jax.ShapeDtypeStruct((b, s // ATT_TQ, MLA_HEADS * MLA_DV, ATT_TQ), BF16)]
                  + [jax.ShapeDtypeStruct(w.shape, BF16) for w in side_f32],
        compiler_params=_cparams(("parallel", "parallel", "parallel")),
        name="mla_attn",
    )(q, k, vt, *side_f32)
    return outs[0], outs[1:]


def _even_in_weight(w):
    hk, hv = GLA_HEADS * GLA_DK, GLA_HEADS * GLA_DV
    widths = (hk, hk, hv, hv, 2 * GLA_RANK, DIFF_HEADS * 2 * DIFF_DK, DIFF_HEADS * 2 * DIFF_DK,
              DIFF_HEADS * DIFF_DV)
    offs = [0]
    for wd_ in widths:
        offs.append(offs[-1] + wd_)
    piece = lambda j: w[:, offs[j]:offs[j + 1]]
    pad = jnp.zeros((w.shape[0], EVEN_COLS - COL_ALR - 2 * GLA_RANK), w.dtype)
    main = jnp.concatenate([piece(0), piece(1), piece(2), piece(3), piece(5), piece(6), piece(4), pad], axis=1)
    return main.astype(BF16), piece(7).T.astype(BF16)


def _rot_half_cols(w):
    half = MLA_ROPE // 2
    shp = w.shape
    g = w.reshape(shp[0], -1, MLA_ROPE)
    return jnp.concatenate([-g[..., half:], g[..., :half]], axis=-1).reshape(shp)


def _mla_weights(w_in, w_uq, w_ukv):
    d = w_in.shape[0]
    dq = MLA_NOPE + MLA_ROPE
    z = lambda rows, cols: jnp.zeros((rows, cols), F32)
    w_kr = w_in[:, MLA_Q_LORA + MLA_KV_LORA:]
    kr_blk = lambda m: jnp.concatenate([z(d, MLA_NOPE), m, z(d, LANES - dq)], axis=1)
    win = jnp.concatenate([w_in[:, :MLA_Q_LORA + MLA_KV_LORA], kr_blk(w_kr), kr_blk(_rot_half_cols(w_kr))],
                          axis=1).astype(BF16)
    uq = w_uq.reshape(MLA_Q_LORA, MLA_HEADS, dq)
    pad_q = jnp.zeros((MLA_Q_LORA, MLA_HEADS, LANES - dq), F32)
    wq1 = jnp.concatenate([uq, pad_q], axis=-1).reshape(MLA_Q_LORA, -1).astype(BF16)
    rot = _rot_half_cols(uq[..., MLA_NOPE:].reshape(MLA_Q_LORA, -1)).reshape(MLA_Q_LORA, MLA_HEADS, MLA_ROPE)
    wq2 = jnp.concatenate([jnp.zeros((MLA_Q_LORA, MLA_HEADS, MLA_NOPE), F32), rot, pad_q],
                          axis=-1).reshape(MLA_Q_LORA, -1).astype(BF16)
    ukv = w_ukv.reshape(MLA_KV_LORA, MLA_HEADS, MLA_NOPE + MLA_DV)
    wk = jnp.concatenate([ukv[..., :MLA_NOPE], jnp.zeros((MLA_KV_LORA, MLA_HEADS, LANES - MLA_NOPE), F32)],
                         axis=-1).reshape(MLA_KV_LORA, -1).astype(BF16)
    wvt = ukv[..., MLA_NOPE:].reshape(MLA_KV_LORA, -1).T.astype(BF16)
    return win, wq1, wq2, wk, wvt


def _rope_tables(seq):
    half = MLA_ROPE // 2
    inv = ROPE_THETA ** (-jnp.arange(half, dtype=F32) / half)
    ang = jnp.arange(seq, dtype=F32)[:, None] * inv[None, :]
    cos = jnp.concatenate([jnp.cos(ang), jnp.cos(ang)], axis=1)
    sin = jnp.concatenate([jnp.sin(ang), jnp.sin(ang)], axis=1)
    dq = MLA_NOPE + MLA_ROPE
    scale = dq ** -0.5 * LOG2E
    lay = lambda a, fill: jnp.concatenate([jnp.full((seq, MLA_NOPE), fill, F32), a,
                                           jnp.zeros((seq, LANES - dq), F32)], axis=1)
    return lay(cos, 1.0) * scale, lay(sin, 0.0) * scale, lay(cos, 0.0), lay(sin, 0.0)


def _routing_plan(info, counts, n_tokens):
    tm = MOE_TILE
    cnt = counts[0, :N_EXPERTS].astype(jnp.int32)
    padded = ((cnt + tm - 1) // tm) * tm
    ends = jnp.cumsum(padded)
    starts = ends - padded
    e = info[:, 0:TOP_K].astype(jnp.int32)
    rank = info[:, 4:4 + TOP_K].astype(jnp.int32)
    pos = starts[e] + rank
    n_tiles = (n_tokens * TOP_K) // tm + N_EXPERTS
    tile_start = jnp.arange(n_tiles, dtype=jnp.int32) * tm
    tile_expert = jnp.minimum(jnp.sum(tile_start[:, None] >= ends[None, :], axis=1), N_EXPERTS - 1)
    n_used = (ends[-1] // tm).reshape(1)
    pad_info = jnp.concatenate([starts + cnt, padded - cnt, n_used])
    return pos, tile_expert.astype(jnp.int32), n_used.astype(jnp.int32), pad_info.astype(jnp.int32), n_tiles * tm


def kernel(x, rel_bias_table, even_w_in, gla_gate_up, gla_gate_bias, gla_norm_gain, diff_lambda, diff_norm_gain, even_w_out, ffn_w_gate, ffn_w_up, ffn_w_down, odd_w_in, mla_q_norm_gain, mla_kv_norm_gain, mla_w_uq, mla_w_ukv, odd_w_out, router_w, moe_w_gate, moe_w_up, moe_w_down, ln_gain, ln_bias):
    b, s, d = x.shape
    n = b * s
    tm = min(512, s)
    assert s % tm == 0 and tm % ATT_TQ == 0 and s % GLA_GROUP == 0
    x2 = x.reshape(n, d)
    row = lambda v: v.reshape(1, -1)

    w_main, w_vdt = _even_in_weight(even_w_in[0])
    proj, vd_t = _in_proj(x2, w_main, w_vdt, s, tm)
    proj = proj.reshape(b, s, EVEN_COLS)
    gup = jnp.zeros((2, LANES, GLA_HEADS * GLA_DK), F32)
    for dd in range(2):
        gup = gup.at[dd, dd * GLA_RANK:(dd + 1) * GLA_RANK].set(gla_gate_up[0, dd])
    o_gla = _gla(proj, gup.astype(BF16), gla_gate_bias[0][:, None, :], row(gla_norm_gain[0]))
    lam_init = 0.8 - 0.6 * math.exp(-0.3 * 0)
    lf = diff_lambda[0]
    lam = (jnp.exp(jnp.sum(lf[0] * lf[1])) - jnp.exp(jnp.sum(lf[2] * lf[3])) + lam_init).reshape(1)
    o_diff_t = _diff_attention(proj, vd_t, _bias_band(rel_bias_table), lam,
                               diff_norm_gain[0].reshape(-1, 1), lam_init)
    hv = GLA_HEADS * GLA_DV
    w_out = even_w_out[0].astype(BF16)
    x2 = _mix_ffn(o_gla.reshape(n, hv), o_diff_t, w_out[:hv], w_out[hv:], x2,
                  row(ln_gain[0, 0]), row(ln_bias[0, 0]),
                  ffn_w_gate[0].astype(BF16), ffn_w_up[0].astype(BF16), ffn_w_down[0].astype(BF16),
                  row(ln_gain[0, 1]), row(ln_bias[0, 1]), s, tm)

    win, wq1, wq2, wk, wvt = _mla_weights(odd_w_in[0], mla_w_uq[0], mla_w_ukv[0])
    q, k, v_t = _mla_proj(x2, win, row(mla_q_norm_gain[0]), row(mla_kv_norm_gain[0]), wq1, wq2, wk, wvt,
                          _rope_tables(s), s, tm)
    moe_w = [moe_w_gate[0], moe_w_up[0], moe_w_down[0]]
    o_t, moe_wb = _mla_attention(q.reshape(b, s, -1), k.reshape(b, s, -1), v_t, min(MLA_TQ, s),
                                 [w.reshape(-1, w.shape[-1]) for w in moe_w])
    wg_b, wu_b, wd_b = [wb.reshape(w.shape) for wb, w in zip(moe_wb, moe_w)]
    rw = jnp.concatenate([router_w[0], jnp.zeros((d, LANES - N_EXPERTS), F32)], axis=1)
    rw_hi = rw.astype(BF16)
    rw2 = jnp.stack([rw_hi, (rw - rw_hi.astype(F32)).astype(BF16)])
    xn, info, counts = _outproj_route(o_t, odd_w_out[0].astype(BF16), x2,
                                      row(ln_gain[1, 0]), row(ln_bias[1, 0]), rw2, s, tm)
    pos, tile_expert, n_used, pad_info, n_slots = _routing_plan(info, counts, n)
    pos_t = pos.reshape(n // tm, tm, TOP_K).transpose(0, 2, 1)
    x_sorted = _scatter_rows(xn, pos_t, pad_info, n_slots, tm)
    y_sorted = _moe_ffn(x_sorted, tile_expert, n_used, wg_b, wu_b, wd_b)
    out = _combine_ln(xn, info, pos_t, y_sorted, row(ln_gain[1, 1]), row(ln_bias[1, 1]), tm)
    return out.reshape(b, s, d)
```

```python
import functools
import math

import jax
import jax.numpy as jnp
from jax import lax
from jax.experimental import pallas as pl
from jax.experimental.pallas import tpu as pltpu

F32 = jnp.float32
BF16 = jnp.bfloat16

LANES = 128
V7X_VMEM_BYTES = 64 * 1024 * 1024
VMEM_LIMIT = V7X_VMEM_BYTES - 8 * 1024 * 1024

D_MODEL = 1024
GLA_HEADS, GLA_DK, GLA_DV = 4, 64, 128
GLA_RANK, GLA_TEMP, GLA_CHUNK = 16, 16.0, 64
DIFF_HEADS, DIFF_DK, DIFF_DV = 4, 64, 128
N_BUCKETS, MAX_DISTANCE = 32, 128
MLA_HEADS, MLA_Q_LORA, MLA_KV_LORA = 16, 256, 128
MLA_NOPE, MLA_ROPE, MLA_DV = 64, 32, 64
ROPE_THETA = 10000.0
D_FF, N_EXPERTS, TOP_K = 2816, 8, 2
DEPTH = 2
ALPHA = (2 * DEPTH) ** 0.25
LN_EPS, RMS_EPS = 1e-5, 1e-6

COL_QK_G, COL_V_G, COL_G_G = 0, 512, 1024
COL_Q_D, COL_K_D, COL_ALR = 1536, 2048, 2560
EVEN_COLS = 2688

ATT_KC = 128
ATT_TQ = 256
MLA_TQ = 2048
COL_REDUCE_SLAB = 128
LOG2E = math.log2(math.e)

GLA_GROUP = 256
FF_CHUNK = 256
MOE_TILE = 512
DMA_UNROLL = 8


def _cparams(sem):
    return pltpu.CompilerParams(dimension_semantics=sem, vmem_limit_bytes=VMEM_LIMIT)


def _layer_norm(y, g, b):
    mu = jnp.mean(y, axis=-1, keepdims=True)
    d = y - mu
    var = jnp.mean(d * d, axis=-1, keepdims=True)
    return d * lax.rsqrt(var + LN_EPS) * g + b


def _split3(x):
    h1 = x.astype(BF16)
    r1 = x - h1.astype(F32)
    h2 = r1.astype(BF16)
    h3 = (r1 - h2.astype(F32)).astype(BF16)
    return h1, h2, h3


def _dot(a, b):
    return jnp.dot(a, b, preferred_element_type=F32)


def _dot_nt(a, b):
    return lax.dot_general(a, b, (((1,), (1,)), ((), ())), preferred_element_type=F32)


def _dot_tn(a, b):
    return lax.dot_general(a, b, (((0,), (0,)), ((), ())), preferred_element_type=F32)


def _in_proj_kernel(x_ref, w_ref, wvt_ref, o_ref, vt_ref):
    xb = x_ref[...].astype(BF16)
    o_ref[...] = _dot(xb, w_ref[...]).astype(o_ref.dtype)
    vt_ref[...] = _dot_nt(wvt_ref[...], xb).astype(vt_ref.dtype)


def _in_proj(x, w, wvt, seq, tm):
    n, k = x.shape
    m = w.shape[1]
    dvt = wvt.shape[0]
    per_seq = seq // tm
    return pl.pallas_call(
        _in_proj_kernel,
        grid=(n // tm,),
        in_specs=[pl.BlockSpec((tm, k), lambda i: (i, 0)),
                  pl.BlockSpec((k, m), lambda i: (0, 0)),
                  pl.BlockSpec((dvt, k), lambda i: (0, 0))],
        out_specs=[pl.BlockSpec((tm, m), lambda i: (i, 0)),
                   pl.BlockSpec((None, dvt, tm), lambda i: (i // per_seq, 0, i % per_seq))],
        out_shape=[jax.ShapeDtypeStruct((n, m), BF16),
                   jax.ShapeDtypeStruct((n // seq, dvt, seq), BF16)],
        compiler_params=_cparams(("parallel",)),
        name="in_proj",
    )(x, w, wvt)


def _side_cast(in_refs, out_refs):
    for w_ref, wb_ref in zip(in_refs, out_refs):
        wb_ref[...] = w_ref[...].astype(wb_ref.dtype)


def _side_specs(side_f32, n_steps, step_of):
    specs = []
    for w in side_f32:
        assert w.shape[0] % (n_steps * 16) == 0, w.shape
        specs.append(pl.BlockSpec((w.shape[0] // n_steps, w.shape[1]), lambda *ids: (step_of(*ids), 0)))
    return specs


def _gla_kernel(qk_ref, v_ref, g_ref, alr_ref, gup_ref, gbias_ref, gain_ref, *refs):
    n_side = (len(refs) - 7) // 2
    _side_cast(refs[:n_side], refs[n_side + 1:2 * n_side + 1])
    o_ref = refs[n_side]
    la_ref, qdec_ref, kdec_ref, decay_ref, acc_ref, st_ref = refs[2 * n_side + 1:]
    seq = qk_ref.shape[0]
    hk = GLA_HEADS * GLA_DK
    hv = GLA_HEADS * GLA_DV
    grp = GLA_GROUP
    c = GLA_CHUNK

    alr = alr_ref[...]
    for d in range(2):
        logits = _dot(alr, gup_ref[d]) + gbias_ref[d]
        log_sig = jnp.minimum(logits, 0.0) - jnp.log(1.0 + jnp.exp(-jnp.abs(logits)))
        la_ref[d] = log_sig / GLA_TEMP

    row = lax.broadcasted_iota(jnp.int32, (grp, grp), 0)
    col = lax.broadcasted_iota(jnp.int32, (grp, grp), 1)
    same = (row // c) == (col // c)
    tri = [same & (row >= col), same & (row <= col)]
    cum_inc = [jnp.where(t, 1.0, 0.0).astype(BF16) for t in tri]
    cum_rem = [jnp.where(same & (row < col), 1.0, 0.0).astype(BF16),
               jnp.where(same & (row > col), 1.0, 0.0).astype(BF16)]
    lane_k = lax.broadcasted_iota(jnp.int32, (grp, hk), 1) // GLA_DK
    scale = GLA_DK ** -0.5

    def group_body(r, carry):
        rows = pl.ds(pl.multiple_of(r * grp, grp), grp)
        q = qk_ref[rows, 0:hk].astype(F32)
        k = qk_ref[rows, hk:2 * hk].astype(F32)
        v = v_ref[rows, :]
        o_heads = [jnp.zeros((grp, GLA_DV), F32) for _ in range(GLA_HEADS)]
        for d in range(2):
            parts = _split3(la_ref[d, rows, :])
            bcum = sum(_dot(cum_inc[d], p) for p in parts)
            brem = sum(_dot(cum_rem[d], p) for p in parts)
            q_dec = q * jnp.exp(bcum) * scale
            k_inv = (k * jnp.exp(-bcum)).astype(BF16)
            qdec_ref[d, rows, :] = q_dec.astype(BF16)
            kdec_ref[d, rows, :] = (k * jnp.exp(brem)).astype(BF16)
            decay_ref[d, rows, :] = jnp.exp(bcum + brem)
            for h in range(GLA_HEADS):
                q_h = jnp.where(lane_k == h, q_dec, 0.0).astype(BF16)
                attn = jnp.where(tri[d], _dot_nt(q_h, k_inv), 0.0)
                o_heads[h] = o_heads[h] + _dot(attn.astype(BF16), v[:, h * GLA_DV:(h + 1) * GLA_DV])
        for h in range(GLA_HEADS):
            acc_ref[rows, h * GLA_DV:(h + 1) * GLA_DV] = o_heads[h]
        return carry

    lax.fori_loop(0, seq // grp, group_body, 0)

    srow = lax.broadcasted_iota(jnp.int32, (hv, hk), 0) // GLA_DV
    scol = lax.broadcasted_iota(jnp.int32, (hv, hk), 1) // GLA_DK
    head_mask = srow == scol
    n_chunks = seq // c
    st_ref[...] = jnp.zeros_like(st_ref)

    def chunk_body(i, carry):
        for d in range(2):
            ci = i if d == 0 else n_chunks - 1 - i
            rows = pl.ds(pl.multiple_of(ci * c, c), c)
            state = st_ref[d]
            acc_ref[rows, :] += _dot_nt(qdec_ref[d, rows, :], state.astype(BF16))
            kv = _dot_tn(v_ref[rows, :], kdec_ref[d, rows, :])
            decay = decay_ref[d, pl.ds(pl.multiple_of(ci * c, c), 1), :]
            st_ref[d] = state * decay + jnp.where(head_mask, kv, 0.0)
        return carry

    lax.fori_loop(0, n_chunks, chunk_body, 0, unroll=2)

    gain = gain_ref[...]
    for h in range(GLA_HEADS):
        sl = slice(h * GLA_DV, (h + 1) * GLA_DV)
        o = acc_ref[:, sl]
        y = o * lax.rsqrt(jnp.mean(o * o, axis=-1, keepdims=True) + RMS_EPS) * gain
        gate = g_ref[:, sl].astype(F32)
        o_ref[:, sl] = (y * (gate * jax.nn.sigmoid(gate))).astype(o_ref.dtype)


def _gla(proj, gup, gbias, gain, side_f32):
    b, s, _ = proj.shape
    hk, hv = GLA_HEADS * GLA_DK, GLA_HEADS * GLA_DV
    blk = lambda width, col: pl.BlockSpec((None, s, width), lambda i: (i, 0, col // width))
    side_specs = _side_specs(side_f32, b, lambda i: i)
    outs = pl.pallas_call(
        _gla_kernel,
        grid=(b,),
        in_specs=[blk(2 * hk, COL_QK_G), blk(hv, COL_V_G), blk(hv, COL_G_G), blk(LANES, COL_ALR),
                  pl.BlockSpec((2, LANES, hk), lambda i: (0, 0, 0)),
                  pl.BlockSpec((2, 1, hk), lambda i: (0, 0, 0)),
                  pl.BlockSpec((1, GLA_DV), lambda i: (0, 0))] + side_specs,
        out_specs=[pl.BlockSpec((None, s, hv), lambda i: (i, 0, 0))] + side_specs,
        out_shape=[jax.ShapeDtypeStruct((b, s, hv), BF16)]
                  + [jax.ShapeDtypeStruct(w.shape, BF16) for w in side_f32],
        scratch_shapes=[pltpu.VMEM((2, s, hk), F32),
                        pltpu.VMEM((2, s, hk), BF16),
                        pltpu.VMEM((2, s, hk), BF16),
                        pltpu.VMEM((2, s, hk), F32),
                        pltpu.VMEM((s, hv), F32),
                        pltpu.VMEM((2, hv, hk), F32)],
        compiler_params=_cparams(("parallel",)),
        name="gla",
    )(proj, proj, proj, proj, gup, gbias, gain, *side_f32)
    return outs[0], outs[1:]


J_BELOW = (-MAX_DISTANCE - ATT_KC + 1) // ATT_KC
J_ABOVE = -(-(MAX_DISTANCE + ATT_TQ - 1) // ATT_KC)
N_BAND = J_ABOVE - J_BELOW + 1
assert ATT_KC * J_BELOW + ATT_KC - 1 <= -MAX_DISTANCE
assert ATT_KC * J_ABOVE - (ATT_TQ - 1) >= MAX_DISTANCE


def _band_buckets():
    half = N_BUCKETS // 2
    max_exact = half // 2
    j = (jnp.arange(N_BAND, dtype=jnp.int32) + J_BELOW)[:, None, None]
    rel = (j * ATT_KC + jnp.arange(ATT_KC, dtype=jnp.int32)[None, :, None]
           - jnp.arange(ATT_TQ, dtype=jnp.int32)[None, None, :])
    bucket = jnp.where(rel > 0, half, 0).astype(jnp.int32)
    n = jnp.abs(rel)
    n_large = max_exact + (jnp.log(jnp.maximum(n, max_exact).astype(F32) / max_exact)
                           / math.log(MAX_DISTANCE / max_exact) * (half - max_exact)).astype(jnp.int32)
    n_large = jnp.minimum(n_large, half - 1)
    return bucket + jnp.where(n < max_exact, n, n_large)


def _band_kernel(tab_ref, bucket_ref, o_ref, *, n_heads):
    h = pl.program_id(0)
    for t in range(N_BAND):
        bucket = bucket_ref[t]
        val = jnp.zeros(bucket.shape, F32)
        for bkt in range(N_BUCKETS):
            val = jnp.where(bucket == bkt, tab_ref[bkt * n_heads + h], val)
        o_ref[t] = val * LOG2E


def _bias_band(table):
    n_heads = table.shape[1]
    bucket = jnp.bitwise_and(_band_buckets(), N_BUCKETS - 1)
    return pl.pallas_call(
        functools.partial(_band_kernel, n_heads=n_heads),
        grid=(n_heads,),
        in_specs=[pl.BlockSpec(memory_space=pltpu.SMEM),
                  pl.BlockSpec((N_BAND, ATT_KC, ATT_TQ), lambda h: (0, 0, 0))],
        out_specs=pl.BlockSpec((None, N_BAND, ATT_KC, ATT_TQ), lambda h: (h, 0, 0, 0)),
        out_shape=jax.ShapeDtypeStruct((n_heads, N_BAND, ATT_KC, ATT_TQ), F32),
        compiler_params=_cparams(("parallel",)),
        name="bias_band",
    )(table.reshape(-1), bucket)


def _diff_kernel(lam_ref, q_ref, k_ref, vt_ref, band_ref, gain_ref, o_ref, *, out_scale):
    seq = q_ref.shape[0]
    n_chunks = seq // ATT_KC
    lane = lax.broadcasted_iota(jnp.int32, (ATT_TQ, 2 * DIFF_DK), 1)
    scale = DIFF_DK ** -0.5 * LOG2E
    items = [(r, mp) for r in range(seq // ATT_TQ) for mp in range(2)]
    first_map = {}

    def scores(item):
        r, mp = item
        q = (q_ref[r * ATT_TQ:(r + 1) * ATT_TQ, :].astype(F32) * scale).astype(BF16)
        q = jnp.where((lane >= DIFF_DK) == (mp == 1), q, jnp.zeros_like(q))
        tiles = [band_ref[min(max(c - (ATT_TQ // ATT_KC) * r - J_BELOW, 0), N_BAND - 1)] for c in range(n_chunks)]
        return _dot_nt(k_ref[...], q) + jnp.concatenate(tiles, axis=0)

    def finish(item, s_t):
        r, mp = item
        o_t = _softmax_pv_t(s_t, vt_ref[...])
        if mp == 0:
            first_map[r] = o_t
            return
        o = first_map.pop(r) - lam_ref[0] * o_t
        y = o * lax.rsqrt(jnp.mean(o * o, axis=0, keepdims=True) + RMS_EPS) * gain_ref[...]
        o_ref[r] = (y * out_scale).astype(o_ref.dtype)

    _pipelined(items, scores, finish)


def _diff_attention(proj, vt, band, lam, gain_col, lam_init):
    b, s, _ = proj.shape
    h = DIFF_HEADS
    w = 2 * DIFF_DK
    assert w == LANES
    return pl.pallas_call(
        functools.partial(_diff_kernel, out_scale=1.0 - lam_init),
        grid=(h, b),
        in_specs=[pl.BlockSpec(memory_space=pltpu.SMEM),
                  pl.BlockSpec((None, s, w), lambda hh, bb: (bb, 0, COL_Q_D // w + hh)),
                  pl.BlockSpec((None, s, w), lambda hh, bb: (bb, 0, COL_K_D // w + hh)),
                  pl.BlockSpec((None, DIFF_DV, s), lambda hh, bb: (bb, hh, 0)),
                  pl.BlockSpec((None, N_BAND, ATT_KC, ATT_TQ), lambda hh, bb: (hh, 0, 0, 0)),
                  pl.BlockSpec((DIFF_DV, 1), lambda hh, bb: (0, 0))],
        out_specs=pl.BlockSpec((None, s // ATT_TQ, DIFF_DV, ATT_TQ), lambda hh, bb: (bb, 0, hh, 0)),
        out_shape=jax.ShapeDtypeStruct((b, s // ATT_TQ, h * DIFF_DV, ATT_TQ), BF16),
        compiler_params=_cparams(("parallel", "parallel")),
        name="diff_attn",
    )(lam, proj, proj, vt, band, gain_col)


def _residual_ln(x_ref, o_ref, g_ref, b_ref, row_pairs, col_pairs):
    for j in range(x_ref.shape[0] // ATT_TQ):
        rows = slice(j * ATT_TQ, (j + 1) * ATT_TQ)
        h = ALPHA * x_ref[rows, :]
        for a_ref, w_ref in row_pairs:
            h = h + _dot(a_ref[rows, :], w_ref[...])
        for a_ref, w_ref in col_pairs:
            h = h + _dot_tn(a_ref[j], w_ref[...])
        o_ref[rows, :] = _layer_norm(h, g_ref[...], b_ref[...])


def _tile_specs(n, d, seq, tm):
    per_seq = seq // tm
    rows = lambda width: pl.BlockSpec((tm, width), lambda i: (i, 0))
    cols = lambda k: pl.BlockSpec((None, tm // ATT_TQ, k, ATT_TQ), lambda i: (i // per_seq, i % per_seq, 0, 0))
    full = lambda a: pl.BlockSpec(a.shape, lambda i: (0,) * a.ndim)
    return rows, cols, full


def _outproj_route_kernel(at_ref, w_ref, x_ref, g_ref, b_ref, rw_ref, o_ref, info_ref, cnt_ref, carry_ref):
    i = pl.program_id(0)
    tm = x_ref.shape[0]

    @pl.when(i == 0)
    def _():
        carry_ref[...] = jnp.zeros_like(carry_ref)

    _residual_ln(x_ref, o_ref, g_ref, b_ref, [], [(at_ref, w_ref)])
    xn = o_ref[...]

    x_hi = xn.astype(BF16)
    x_lo = (xn - x_hi.astype(F32)).astype(BF16)
    logits = _dot(x_hi, rw_ref[0]) + (_dot(x_hi, rw_ref[1]) + _dot(x_lo, rw_ref[0]))
    lane = lax.broadcasted_iota(jnp.int32, logits.shape, 1).astype(F32)
    neg = jnp.float32(-jnp.inf)
    logits = jnp.where(lane < N_EXPERTS, logits, neg)
    v1 = jnp.max(logits, axis=-1, keepdims=True)
    e1 = jnp.min(jnp.where(logits == v1, lane, float(LANES)), axis=-1, keepdims=True)
    rest = jnp.where(lane == e1, neg, logits)
    v2 = jnp.max(rest, axis=-1, keepdims=True)
    e2 = jnp.min(jnp.where(rest == v2, lane, float(LANES)), axis=-1, keepdims=True)
    t = jnp.exp(v2 - v1)
    w1 = 1.0 / (1.0 + t)
    w2 = t / (1.0 + t)

    onehot = jnp.where((lane == e1) | (lane == e2), 1.0, 0.0)
    row = lax.broadcasted_iota(jnp.int32, (tm, tm), 0)
    col = lax.broadcasted_iota(jnp.int32, (tm, tm), 1)
    before = jnp.where(row > col, 1.0, 0.0).astype(BF16)
    prior = carry_ref[...] + _dot(before, onehot.astype(BF16))
    r1 = jnp.sum(jnp.where(lane == e1, prior, 0.0), axis=-1, keepdims=True)
    r2 = jnp.sum(jnp.where(lane == e2, prior, 0.0), axis=-1, keepdims=True)
    carry_ref[...] += jnp.sum(onehot, axis=0, keepdims=True)
    cnt_ref[...] = jnp.broadcast_to(carry_ref[...], cnt_ref.shape)

    info = jnp.zeros(logits.shape, F32)
    for idx, val in enumerate((e1, e2, w1, w2, r1, r2)):
        info = jnp.where(lane == idx, val, info)
    info_ref[...] = info


def _outproj_route(at, w, x, g, b, router_w2, seq, tm):
    n, d = x.shape
    rows, cols, full = _tile_specs(n, d, seq, tm)
    return pl.pallas_call(
        _outproj_route_kernel,
        grid=(n // tm,),
        in_specs=[cols(at.shape[2]), full(w), rows(d), full(g), full(b), full(router_w2)],
        out_specs=[rows(d), rows(LANES),
                   pl.BlockSpec((8, LANES), lambda i: (0, 0))],
        out_shape=[jax.ShapeDtypeStruct((n, d), F32),
                   jax.ShapeDtypeStruct((n, LANES), F32),
                   jax.ShapeDtypeStruct((8, LANES), F32)],
        scratch_shapes=[pltpu.VMEM((1, LANES), F32)],
        compiler_params=_cparams(("arbitrary",)),
        name="outproj_route",
    )(at, w, x, g, b, router_w2)


def _swiglu_acc(xb, wg_ref, wu_ref, wd_ref, acc_ref):
    n_chunks = wg_ref.shape[-1] // FF_CHUNK
    for c in range(n_chunks):
        cols = slice(c * FF_CHUNK, (c + 1) * FF_CHUNK)
        gate = _dot(xb, wg_ref[:, cols])
        up = _dot(xb, wu_ref[:, cols])
        hidden = (gate * jax.nn.sigmoid(gate) * up).astype(BF16)
        part = _dot(hidden, wd_ref[cols, :])
        if c == 0:
            acc_ref[...] = part
        else:
            acc_ref[...] += part


def _mix_ffn_kernel(a_ref, at_ref, wa_ref, wt_ref, x_ref, g1_ref, b1_ref, wg_ref, wu_ref, wd_ref,
                    g2_ref, b2_ref, o_ref, h_ref):
    _residual_ln(x_ref, h_ref, g1_ref, b1_ref, [(a_ref, wa_ref)], [(at_ref, wt_ref)])
    h = h_ref[...]
    _swiglu_acc(h.astype(BF16), wg_ref, wu_ref, wd_ref, o_ref)
    o_ref[...] = _layer_norm(ALPHA * h + o_ref[...], g2_ref[...], b2_ref[...])


def _mix_ffn(a, at, wa, wt, x, g1, b1, wg, wu, wd, g2, b2, seq, tm):
    n, d = x.shape
    rows, cols, _ = _tile_specs(n, d, seq, tm)
    once = lambda arr: pl.BlockSpec(arr.shape, lambda i: (0,) * arr.ndim, pipeline_mode=pl.Buffered(1))
    return pl.pallas_call(
        _mix_ffn_kernel,
        grid=(n // tm,),
        in_specs=[rows(a.shape[1]), cols(at.shape[2]), once(wa), once(wt), rows(d), once(g1), once(b1),
                  once(wg), once(wu), once(wd), once(g2), once(b2)],
        out_specs=rows(d),
        out_shape=jax.ShapeDtypeStruct((n, d), F32),
        scratch_shapes=[pltpu.VMEM((tm, d), F32)],
        compiler_params=_cparams(("parallel",)),
        name="mix_ffn",
    )(a, at, wa, wt, x, g1, b1, wg, wu, wd, g2, b2)


def _moe_kernel(te_ref, nu_ref, x_ref, wg_ref, wu_ref, wd_ref, o_ref):
    t = pl.program_id(0)

    @pl.when(t < nu_ref[0])
    def _():
        _swiglu_acc(x_ref[...].astype(BF16), wg_ref, wu_ref, wd_ref, o_ref)

    @pl.when(t >= nu_ref[0])
    def _():
        o_ref[...] = jnp.zeros_like(o_ref)


def _moe_ffn(x_sorted, tile_expert, n_used, wg, wu, wd):
    p, d = x_sorted.shape
    f = wg.shape[2]
    tm = MOE_TILE
    grid_spec = pltpu.PrefetchScalarGridSpec(
        num_scalar_prefetch=2,
        grid=(p // tm,),
        in_specs=[pl.BlockSpec((tm, d), lambda t, te, nu: (jnp.minimum(t, nu[0] - 1), 0)),
                  pl.BlockSpec((None, d, f), lambda t, te, nu: (te[t], 0, 0)),
                  pl.BlockSpec((None, d, f), lambda t, te, nu: (te[t], 0, 0)),
                  pl.BlockSpec((None, f, d), lambda t, te, nu: (te[t], 0, 0))],
        out_specs=pl.BlockSpec((tm, d), lambda t, te, nu: (t, 0)),
    )
    return pl.pallas_call(
        _moe_kernel,
        grid_spec=grid_spec,
        out_shape=jax.ShapeDtypeStruct((p, d), F32),
        compiler_params=_cparams(("arbitrary",)),
        name="moe_ffn",
    )(tile_expert, n_used, x_sorted, wg, wu, wd)


def _scatter_kernel(pos_ref, pad_ref, x_ref, o_hbm, zero_ref, sem, zsem):
    i = pl.program_id(0)
    tm = x_ref.shape[0]
    n_tiles = o_hbm.shape[0] // MOE_TILE

    @pl.when(i == 0)
    def _():
        zero_ref[...] = jnp.zeros_like(zero_ref)
        zero_row = zero_ref.at[pl.ds(0, 1)]
        for e in range(N_EXPERTS):
            start, count = pad_ref[e], pad_ref[N_EXPERTS + e]

            def fill(r, carry):
                pltpu.make_async_copy(zero_row, o_hbm.at[pl.ds(start + r, 1)], zsem).start()
                return carry

            lax.fori_loop(0, count, fill, 0)

            def drain(r, carry):
                pltpu.make_async_copy(zero_row, o_hbm.at[pl.ds(start + r, 1)], zsem).wait()
                return carry

            lax.fori_loop(0, count, drain, 0)

        def fill_tile(t, carry):
            dst = o_hbm.at[pl.ds(pl.multiple_of(t * MOE_TILE, MOE_TILE), MOE_TILE)]
            cp = pltpu.make_async_copy(zero_ref, dst, zsem)
            cp.start()
            cp.wait()
            return carry

        lax.fori_loop(pad_ref[2 * N_EXPERTS], n_tiles, fill_tile, 0)

    def issue(r, carry):
        for kk in range(TOP_K):
            dst = pos_ref[kk * tm + r]
            pltpu.make_async_copy(x_ref.at[pl.ds(r, 1)], o_hbm.at[pl.ds(dst, 1)], sem).start(priority=kk)
        return carry

    lax.fori_loop(0, tm, issue, 0, unroll=DMA_UNROLL)

    for kk in range(TOP_K):
        pltpu.make_async_copy(x_ref, o_hbm.at[pl.ds(0, tm)], sem).wait()


def _scatter_rows(x, pos, pad_info, n_slots, tm):
    n, d = x.shape
    return pl.pallas_call(
        _scatter_kernel,
        grid=(n // tm,),
        in_specs=[pl.BlockSpec((TOP_K * tm,), lambda i: (i,), memory_space=pltpu.SMEM),
                  pl.BlockSpec(memory_space=pltpu.SMEM),
                  pl.BlockSpec((tm, d), lambda i: (i, 0))],
        out_specs=pl.BlockSpec(memory_space=pl.ANY),
        out_shape=jax.ShapeDtypeStruct((n_slots, d), F32),
        scratch_shapes=[pltpu.VMEM((MOE_TILE, d), F32),
                        pltpu.SemaphoreType.DMA(()),
                        pltpu.SemaphoreType.DMA(())],
        compiler_params=_cparams(("arbitrary",)),
        name="scatter_rows",
    )(pos, pad_info, x)


def _combine_kernel(pos_ref, next_ref, x_ref, info_ref, g_ref, b_ref, y_hbm, o_ref, buf_ref, sem):
    i = pl.program_id(0)
    tm = x_ref.shape[0]
    slot = i % 2

    def gather(idx_ref, s):
        def issue(r, carry):
            for kk in range(TOP_K):
                src = idx_ref[kk * tm + r]
                pltpu.make_async_copy(y_hbm.at[pl.ds(src, 1)], buf_ref.at[s, kk, pl.ds(r, 1)],
                                      sem.at[s]).start(priority=kk)
            return carry
        lax.fori_loop(0, tm, issue, 0, unroll=DMA_UNROLL)

    @pl.when(i == 0)
    def _():
        gather(pos_ref, slot)

    @pl.when(i + 1 < pl.num_programs(0))
    def _():
        gather(next_ref, 1 - slot)

    for kk in range(TOP_K):
        pltpu.make_async_copy(y_hbm.at[pl.ds(0, tm)], buf_ref.at[slot, kk], sem.at[slot]).wait()

    info = info_ref[...]
    w1 = info[:, 2:3]
    w2 = info[:, 3:4]
    f = w1 * buf_ref[slot, 0] + w2 * buf_ref[slot, 1]
    o_ref[...] = _layer_norm(ALPHA * x_ref[...] + f, g_ref[...], b_ref[...])


def _combine_ln(x, info, pos, y_sorted, g, b, tm):
    n, d = x.shape
    last = n // tm - 1
    return pl.pallas_call(
        _combine_kernel,
        grid=(n // tm,),
        in_specs=[pl.BlockSpec((TOP_K * tm,), lambda i: (i,), memory_space=pltpu.SMEM),
                  pl.BlockSpec((TOP_K * tm,), lambda i: (jnp.minimum(i + 1, last),), memory_space=pltpu.SMEM),
                  pl.BlockSpec((tm, d), lambda i: (i, 0)),
                  pl.BlockSpec((tm, LANES), lambda i: (i, 0)),
                  pl.BlockSpec((1, d), lambda i: (0, 0)),
                  pl.BlockSpec((1, d), lambda i: (0, 0)),
                  pl.BlockSpec(memory_space=pl.ANY)],
        out_specs=pl.BlockSpec((tm, d), lambda i: (i, 0)),
        out_shape=jax.ShapeDtypeStruct((n, d), F32),
        scratch_shapes=[pltpu.VMEM((2, TOP_K, tm, d), F32),
                        pltpu.SemaphoreType.DMA((2,))],
        compiler_params=_cparams(("arbitrary",)),
        name="combine_ln",
    )(pos, pos, x, info, g, b, y_sorted)


def _mla_proj_kernel(x_ref, win_ref, qg_ref, kvg_ref, wq1_ref, wq2_ref, wk_ref, wvt_ref,
                     cq_ref, sq_ref, ck_ref, sk_ref, q_ref, k_ref, vt_ref):
    c = _dot(x_ref[...].astype(BF16), win_ref[...])
    cq = c[:, :MLA_Q_LORA]
    ckv = c[:, MLA_Q_LORA:MLA_Q_LORA + MLA_KV_LORA]
    off = MLA_Q_LORA + MLA_KV_LORA
    k_rope = c[:, off:off + LANES] * ck_ref[...] + c[:, off + LANES:off + 2 * LANES] * sk_ref[...]
    cq = (cq * lax.rsqrt(jnp.mean(cq * cq, axis=-1, keepdims=True) + RMS_EPS) * qg_ref[...]).astype(BF16)
    ckv = (ckv * lax.rsqrt(jnp.mean(ckv * ckv, axis=-1, keepdims=True) + RMS_EPS) * kvg_ref[...]).astype(BF16)
    q1 = _dot(cq, wq1_ref[...])
    q2 = _dot(cq, wq2_ref[...])
    k1 = _dot(ckv, wk_ref[...])
    vt_ref[...] = _dot_nt(wvt_ref[...], ckv).astype(vt_ref.dtype)
    cos_q, sin_q = cq_ref[...], sq_ref[...]
    for h in range(MLA_HEADS):
        sl = slice(h * LANES, (h + 1) * LANES)
        q_ref[:, sl] = (q1[:, sl] * cos_q + q2[:, sl] * sin_q).astype(q_ref.dtype)
        k_ref[:, sl] = (k1[:, sl] + k_rope).astype(k_ref.dtype)


def _mla_proj(x, win, qg, kvg, wq1, wq2, wk, wvt, tabs, seq, tm):
    n, d = x.shape
    per_seq = seq // tm
    full = lambda a: pl.BlockSpec(a.shape, lambda i: (0,) * a.ndim)
    tab = pl.BlockSpec((tm, LANES), lambda i: (i % per_seq, 0))
    hw = MLA_HEADS * LANES
    dvt = wvt.shape[0]
    return pl.pallas_call(
        _mla_proj_kernel,
        grid=(n // tm,),
        in_specs=[pl.BlockSpec((tm, d), lambda i: (i, 0)), full(win), full(qg), full(kvg),
                  full(wq1), full(wq2), full(wk), full(wvt), tab, tab, tab, tab],
        out_specs=[pl.BlockSpec((tm, hw), lambda i: (i, 0)),
                   pl.BlockSpec((tm, hw), lambda i: (i, 0)),
                   pl.BlockSpec((None, dvt, tm), lambda i: (i // per_seq, 0, i % per_seq))],
        out_shape=[jax.ShapeDtypeStruct((n, hw), BF16),
                   jax.ShapeDtypeStruct((n, hw), BF16),
                   jax.ShapeDtypeStruct((n // seq, dvt, seq), BF16)],
        compiler_params=_cparams(("parallel",)),
        name="mla_proj",
    )(x, win, qg, kvg, wq1, wq2, wk, wvt, *tabs)


def _softmax_pv_t(s_t, vt):
    p = jnp.exp2(s_t - _col_reduce(s_t, jnp.max))
    l = _col_reduce(p, jnp.sum)
    return _dot(vt, p.astype(BF16)) * (1.0 / l)


def _col_reduce(x, op):
    rows, n = x.shape
    slab = COL_REDUCE_SLAB if rows % COL_REDUCE_SLAB == 0 else rows
    return op(op(x.reshape(rows // slab, slab, n), axis=0), axis=0, keepdims=True)


def _pipelined(items, scores, finish):
    s_next = scores(items[0])
    for idx, item in enumerate(items):
        s = s_next
        if idx + 1 < len(items):
            s_next = scores(items[idx + 1])
        finish(item, s)


def _mla_attn_kernel(q_ref, k_ref, vt_ref, *refs):
    n_side = (len(refs) - 1) // 2
    o_ref = refs[n_side]
    _side_cast(refs[:n_side], refs[n_side + 1:])
    n_sub = q_ref.shape[0] // ATT_TQ
    items = [(r, hh) for r in range(n_sub) for hh in range(2)]
    cols = lambda hh: slice(hh * LANES, (hh + 1) * LANES)

    def scores(item):
        r, hh = item
        return _dot_nt(k_ref[:, cols(hh)], q_ref[r * ATT_TQ:(r + 1) * ATT_TQ, cols(hh)])

    def finish(item, s_t):
        r, hh = item
        vrows = slice(hh * MLA_DV, (hh + 1) * MLA_DV)
        o_ref[r, vrows, :] = _softmax_pv_t(s_t, vt_ref[vrows, :]).astype(o_ref.dtype)

    _pipelined(items, scores, finish)


def _mla_attention(q, k, vt, tq, side_f32):
    b, s, _ = q.shape
    pairs = MLA_HEADS // 2
    pair_dv = 2 * MLA_DV
    n_q = s // tq
    side_specs = _side_specs(side_f32, b * pairs * n_q, lambda bb, p, i: (bb * pairs + p) * n_q + i)
    outs = pl.pallas_call(
        _mla_attn_kernel,
        grid=(b, pairs, n_q),
        in_specs=[pl.BlockSpec((None, tq, 2 * LANES), lambda bb, p, i: (bb, i, p)),
                  pl.BlockSpec((None, s, 2 * LANES), lambda bb, p, i: (bb, 0, p)),
                  pl.BlockSpec((None, pair_dv, s), lambda bb, p, i: (bb, p, 0))] + side_specs,
        out_specs=[pl.BlockSpec((None, tq // ATT_TQ, pair_dv, ATT_TQ), lambda bb, p, i: (bb, i, p, 0))] + side_specs,
        out_shape=[jax.ShapeDtypeStruct((b, s // ATT_TQ, MLA_HEADS * MLA_DV, ATT_TQ), BF16)]
                  + [jax.ShapeDtypeStruct(w.shape, BF16) for w in side_f32],
        compiler_params=_cparams(("parallel", "parallel", "parallel")),
        name="mla_attn",
    )(q, k, vt, *side_f32)
    return outs[0], outs[1:]


def _even_in_weight(w):
    hk, hv = GLA_HEADS * GLA_DK, GLA_HEADS * GLA_DV
    widths = (hk, hk, hv, hv, 2 * GLA_RANK, DIFF_HEADS * 2 * DIFF_DK, DIFF_HEADS * 2 * DIFF_DK,
              DIFF_HEADS * DIFF_DV)
    offs = [0]
    for wd_ in widths:
        offs.append(offs[-1] + wd_)
    piece = lambda j: w[:, offs[j]:offs[j + 1]]
    pad = jnp.zeros((w.shape[0], EVEN_COLS - COL_ALR - 2 * GLA_RANK), w.dtype)
    main = jnp.concatenate([piece(0), piece(1), piece(2), piece(3), piece(5), piece(6), piece(4), pad], axis=1)
    return main.astype(BF16), piece(7).T.astype(BF16)


def _rot_half_cols(w):
    half = MLA_ROPE // 2
    shp = w.shape
    g = w.reshape(shp[0], -1, MLA_ROPE)
    return jnp.concatenate([-g[..., half:], g[..., :half]], axis=-1).reshape(shp)


def _mla_weights(w_in, w_uq, w_ukv):
    d = w_in.shape[0]
    dq = MLA_NOPE + MLA_ROPE
    z = lambda rows, cols: jnp.zeros((rows, cols), F32)
    w_kr = w_in[:, MLA_Q_LORA + MLA_KV_LORA:]
    kr_blk = lambda m: jnp.concatenate([z(d, MLA_NOPE), m, z(d, LANES - dq)], axis=1)
    win = jnp.concatenate([w_in[:, :MLA_Q_LORA + MLA_KV_LORA], kr_blk(w_kr), kr_blk(_rot_half_cols(w_kr))],
                          axis=1).astype(BF16)
    uq = w_uq.reshape(MLA_Q_LORA, MLA_HEADS, dq)
    pad_q = jnp.zeros((MLA_Q_LORA, MLA_HEADS, LANES - dq), F32)
    wq1 = jnp.concatenate([uq, pad_q], axis=-1).reshape(MLA_Q_LORA, -1).astype(BF16)
    rot = _rot_half_cols(uq[..., MLA_NOPE:].reshape(MLA_Q_LORA, -1)).reshape(MLA_Q_LORA, MLA_HEADS, MLA_ROPE)
    wq2 = jnp.concatenate([jnp.zeros((MLA_Q_LORA, MLA_HEADS, MLA_NOPE), F32), rot, pad_q],
                          axis=-1).reshape(MLA_Q_LORA, -1).astype(BF16)
    ukv = w_ukv.reshape(MLA_KV_LORA, MLA_HEADS, MLA_NOPE + MLA_DV)
    wk = jnp.concatenate([ukv[..., :MLA_NOPE], jnp.zeros((MLA_KV_LORA, MLA_HEADS, LANES - MLA_NOPE), F32)],
                         axis=-1).reshape(MLA_KV_LORA, -1).astype(BF16)
    wvt = ukv[..., MLA_NOPE:].reshape(MLA_KV_LORA, -1).T.astype(BF16)
    return win, wq1, wq2, wk, wvt


def _rope_tables(seq):
    half = MLA_ROPE // 2
    inv = ROPE_THETA ** (-jnp.arange(half, dtype=F32) / half)
    ang = jnp.arange(seq, dtype=F32)[:, None] * inv[None, :]
    cos = jnp.concatenate([jnp.cos(ang), jnp.cos(ang)], axis=1)
    sin = jnp.concatenate([jnp.sin(ang), jnp.sin(ang)], axis=1)
    dq = MLA_NOPE + MLA_ROPE
    scale = dq ** -0.5 * LOG2E
    lay = lambda a, fill: jnp.concatenate([jnp.full((seq, MLA_NOPE), fill, F32), a,
                                           jnp.zeros((seq, LANES - dq), F32)], axis=1)
    return lay(cos, 1.0) * scale, lay(sin, 0.0) * scale, lay(cos, 0.0), lay(sin, 0.0)


def _routing_plan(info, counts, n_tokens):
    tm = MOE_TILE
    cnt = counts[0, :N_EXPERTS].astype(jnp.int32)
    padded = ((cnt + tm - 1) // tm) * tm
    ends = jnp.cumsum(padded)
    starts = ends - padded
    e = info[:, 0:TOP_K].astype(jnp.int32)
    rank = info[:, 4:4 + TOP_K].astype(jnp.int32)
    pos = starts[e] + rank
    n_tiles = (n_tokens * TOP_K) // tm + N_EXPERTS
    tile_start = jnp.arange(n_tiles, dtype=jnp.int32) * tm
    tile_expert = jnp.minimum(jnp.sum(tile_start[:, None] >= ends[None, :], axis=1), N_EXPERTS - 1)
    n_used = (ends[-1] // tm).reshape(1)
    pad_info = jnp.concatenate([starts + cnt, padded - cnt, n_used])
    return pos, tile_expert.astype(jnp.int32), n_used.astype(jnp.int32), pad_info.astype(jnp.int32), n_tiles * tm


def kernel(x, rel_bias_table, even_w_in, gla_gate_up, gla_gate_bias, gla_norm_gain, diff_lambda, diff_norm_gain, even_w_out, ffn_w_gate, ffn_w_up, ffn_w_down, odd_w_in, mla_q_norm_gain, mla_kv_norm_gain, mla_w_uq, mla_w_ukv, odd_w_out, router_w, moe_w_gate, moe_w_up, moe_w_down, ln_gain, ln_bias):
    b, s, d = x.shape
    n = b * s
    tm = min(512, s)
    assert s % tm == 0 and tm % ATT_TQ == 0 and s % GLA_GROUP == 0
    x2 = x.reshape(n, d)
    row = lambda v: v.reshape(1, -1)

    w_main, w_vdt = _even_in_weight(even_w_in[0])
    proj, vd_t = _in_proj(x2, w_main, w_vdt, s, tm)
    proj = proj.reshape(b, s, EVEN_COLS)
    gup = jnp.zeros((2, LANES, GLA_HEADS * GLA_DK), F32)
    for dd in range(2):
        gup = gup.at[dd, dd * GLA_RANK:(dd + 1) * GLA_RANK].set(gla_gate_up[0, dd])
    o_gla, (ffn_wg, ffn_wu, ffn_wd, w_out) = _gla(
        proj, gup.astype(BF16), gla_gate_bias[0][:, None, :], row(gla_norm_gain[0]),
        [ffn_w_gate[0], ffn_w_up[0], ffn_w_down[0], even_w_out[0]])
    lam_init = 0.8 - 0.6 * math.exp(-0.3 * 0)
    lf = diff_lambda[0]
    lam = (jnp.exp(jnp.sum(lf[0] * lf[1])) - jnp.exp(jnp.sum(lf[2] * lf[3])) + lam_init).reshape(1)
    o_diff_t = _diff_attention(proj, vd_t, _bias_band(rel_bias_table), lam,
                               diff_norm_gain[0].reshape(-1, 1), lam_init)
    hv = GLA_HEADS * GLA_DV
    x2 = _mix_ffn(o_gla.reshape(n, hv), o_diff_t, w_out[:hv], w_out[hv:], x2,
                  row(ln_gain[0, 0]), row(ln_bias[0, 0]), ffn_wg, ffn_wu, ffn_wd,
                  row(ln_gain[0, 1]), row(ln_bias[0, 1]), s, tm)

    win, wq1, wq2, wk, wvt = _mla_weights(odd_w_in[0], mla_w_uq[0], mla_w_ukv[0])
    q, k, v_t = _mla_proj(x2, win, row(mla_q_norm_gain[0]), row(mla_kv_norm_gain[0]), wq1, wq2, wk, wvt,
                          _rope_tables(s), s, tm)
    moe_w = [moe_w_gate[0], moe_w_up[0], moe_w_down[0]]
    o_t, moe_wb = _mla_attention(q.reshape(b, s, -1), k.reshape(b, s, -1), v_t, min(MLA_TQ, s),
                                 [w.reshape(-1, w.shape[-1]) for w in moe_w])
    wg_b, wu_b, wd_b = [wb.reshape(w.shape) for wb, w in zip(moe_wb, moe_w)]
    rw = jnp.concatenate([router_w[0], jnp.zeros((d, LANES - N_EXPERTS), F32)], axis=1)
    rw_hi = rw.astype(BF16)
    rw2 = jnp.stack([rw_hi, (rw - rw_hi.astype(F32)).astype(BF16)])
    xn, info, counts = _outproj_route(o_t, odd_w_out[0].astype(BF16), x2,
                                      row(ln_gain[1, 0]), row(ln_bias[1, 0]), rw2, s, tm)
    pos, tile_expert, n_used, pad_info, n_slots = _routing_plan(info, counts, n)
    pos_t = pos.reshape(n // tm, tm, TOP_K).transpose(0, 2, 1).reshape(-1)
    x_sorted = _scatter_rows(xn, pos_t, pad_info, n_slots, tm)
    y_sorted = _moe_ffn(x_sorted, tile_expert, n_used, wg_b, wu_b, wd_b)
    out = _combine_ln(xn, info, pos_t, y_sorted, row(ln_gain[1, 1]), row(ln_bias[1, 1]), tm)
    return out.reshape(b, s, d)
```

```python
import functools
import math

import jax
import jax.numpy as jnp
from jax import lax
from jax.experimental import pallas as pl
from jax.experimental.pallas import tpu as pltpu

F32 = jnp.float32
BF16 = jnp.bfloat16

LANES = 128
V7X_VMEM_BYTES = 64 * 1024 * 1024
VMEM_LIMIT = V7X_VMEM_BYTES - 8 * 1024 * 1024

D_MODEL = 1024
GLA_HEADS, GLA_DK, GLA_DV = 4, 64, 128
GLA_RANK, GLA_TEMP, GLA_CHUNK = 16, 16.0, 64
DIFF_HEADS, DIFF_DK, DIFF_DV = 4, 64, 128
N_BUCKETS, MAX_DISTANCE = 32, 128
MLA_HEADS, MLA_Q_LORA, MLA_KV_LORA = 16, 256, 128
MLA_NOPE, MLA_ROPE, MLA_DV = 64, 32, 64
ROPE_THETA = 10000.0
D_FF, N_EXPERTS, TOP_K = 2816, 8, 2
DEPTH = 2
ALPHA = (2 * DEPTH) ** 0.25
LN_EPS, RMS_EPS = 1e-5, 1e-6

COL_QK_G, COL_V_G, COL_G_G = 0, 512, 1024
COL_Q_D, COL_K_D, COL_ALR = 1536, 2048, 2560
EVEN_COLS = 2688

ATT_KC = 128
ATT_TQ = 256
MLA_TQ = 2048
COL_REDUCE_SLAB = 128
LOG2E = math.log2(math.e)

GLA_GROUP = 256
FF_CHUNK = 256
MOE_TILE = 512
DMA_UNROLL = 8


def _cparams(sem):
    return pltpu.CompilerParams(dimension_semantics=sem, vmem_limit_bytes=VMEM_LIMIT)


def _layer_norm(y, g, b):
    mu = jnp.mean(y, axis=-1, keepdims=True)
    d = y - mu
    var = jnp.mean(d * d, axis=-1, keepdims=True)
    return d * lax.rsqrt(var + LN_EPS) * g + b


def _split3(x):
    h1 = x.astype(BF16)
    r1 = x - h1.astype(F32)
    h2 = r1.astype(BF16)
    h3 = (r1 - h2.astype(F32)).astype(BF16)
    return h1, h2, h3


def _dot(a, b):
    return jnp.dot(a, b, preferred_element_type=F32)


def _dot_nt(a, b):
    return lax.dot_general(a, b, (((1,), (1,)), ((), ())), preferred_element_type=F32)


def _dot_tn(a, b):
    return lax.dot_general(a, b, (((0,), (0,)), ((), ())), preferred_element_type=F32)


def _in_proj_kernel(x_ref, w_ref, wvt_ref, o_ref, vt_ref):
    xb = x_ref[...].astype(BF16)
    o_ref[...] = _dot(xb, w_ref[...]).astype(o_ref.dtype)
    vt_ref[...] = _dot_nt(wvt_ref[...], xb).astype(vt_ref.dtype)


def _in_proj(x, w, wvt, seq, tm):
    n, k = x.shape
    m = w.shape[1]
    dvt = wvt.shape[0]
    per_seq = seq // tm
    return pl.pallas_call(
        _in_proj_kernel,
        grid=(n // tm,),
        in_specs=[pl.BlockSpec((tm, k), lambda i: (i, 0)),
                  pl.BlockSpec((k, m), lambda i: (0, 0)),
                  pl.BlockSpec((dvt, k), lambda i: (0, 0))],
        out_specs=[pl.BlockSpec((tm, m), lambda i: (i, 0)),
                   pl.BlockSpec((None, dvt, tm), lambda i: (i // per_seq, 0, i % per_seq))],
        out_shape=[jax.ShapeDtypeStruct((n, m), BF16),
                   jax.ShapeDtypeStruct((n // seq, dvt, seq), BF16)],
        compiler_params=_cparams(("parallel",)),
        name="in_proj",
    )(x, w, wvt)


def _side_cast(in_refs, out_refs):
    for w_ref, wb_ref in zip(in_refs, out_refs):
        wb_ref[...] = w_ref[...].astype(wb_ref.dtype)


def _side_specs(side_f32, n_steps, step_of):
    specs = []
    for w in side_f32:
        assert w.shape[0] % (n_steps * 16) == 0, w.shape
        specs.append(pl.BlockSpec((w.shape[0] // n_steps, w.shape[1]), lambda *ids: (step_of(*ids), 0)))
    return specs


def _gla_kernel(qk_ref, v_ref, g_ref, alr_ref, gup_ref, gbias_ref, gain_ref, *refs):
    n_side = (len(refs) - 7) // 2
    _side_cast(refs[:n_side], refs[n_side + 1:2 * n_side + 1])
    o_ref = refs[n_side]
    la_ref, qdec_ref, kdec_ref, decay_ref, acc_ref, st_ref = refs[2 * n_side + 1:]
    seq = qk_ref.shape[0]
    hk = GLA_HEADS * GLA_DK
    hv = GLA_HEADS * GLA_DV
    grp = GLA_GROUP
    c = GLA_CHUNK

    alr = alr_ref[...]
    for d in range(2):
        logits = _dot(alr, gup_ref[d]) + gbias_ref[d]
        log_sig = jnp.minimum(logits, 0.0) - jnp.log(1.0 + jnp.exp(-jnp.abs(logits)))
        la_ref[d] = log_sig / GLA_TEMP

    row = lax.broadcasted_iota(jnp.int32, (grp, grp), 0)
    col = lax.broadcasted_iota(jnp.int32, (grp, grp), 1)
    same = (row // c) == (col // c)
    tri = [same & (row >= col), same & (row <= col)]
    cum_inc = [jnp.where(t, 1.0, 0.0).astype(BF16) for t in tri]
    cum_rem = [jnp.where(same & (row < col), 1.0, 0.0).astype(BF16),
               jnp.where(same & (row > col), 1.0, 0.0).astype(BF16)]
    lane_k = lax.broadcasted_iota(jnp.int32, (grp, hk), 1) // GLA_DK
    scale = GLA_DK ** -0.5

    def group_body(r, carry):
        rows = pl.ds(pl.multiple_of(r * grp, grp), grp)
        q = qk_ref[rows, 0:hk].astype(F32)
        k = qk_ref[rows, hk:2 * hk].astype(F32)
        v = v_ref[rows, :]
        o_heads = [jnp.zeros((grp, GLA_DV), F32) for _ in range(GLA_HEADS)]
        for d in range(2):
            parts = _split3(la_ref[d, rows, :])
            bcum = sum(_dot(cum_inc[d], p) for p in parts)
            brem = sum(_dot(cum_rem[d], p) for p in parts)
            q_dec = q * jnp.exp(bcum) * scale
            k_inv = (k * jnp.exp(-bcum)).astype(BF16)
            qdec_ref[d, rows, :] = q_dec.astype(BF16)
            kdec_ref[d, rows, :] = (k * jnp.exp(brem)).astype(BF16)
            decay_ref[d, rows, :] = jnp.exp(bcum + brem)
            for h in range(GLA_HEADS):
                q_h = jnp.where(lane_k == h, q_dec, 0.0).astype(BF16)
                attn = jnp.where(tri[d], _dot_nt(q_h, k_inv), 0.0)
                o_heads[h] = o_heads[h] + _dot(attn.astype(BF16), v[:, h * GLA_DV:(h + 1) * GLA_DV])
        for h in range(GLA_HEADS):
            acc_ref[rows, h * GLA_DV:(h + 1) * GLA_DV] = o_heads[h]
        return carry

    lax.fori_loop(0, seq // grp, group_body, 0)

    srow = lax.broadcasted_iota(jnp.int32, (hv, hk), 0) // GLA_DV
    scol = lax.broadcasted_iota(jnp.int32, (hv, hk), 1) // GLA_DK
    head_mask = srow == scol
    n_chunks = seq // c
    st_ref[...] = jnp.zeros_like(st_ref)

    def chunk_body(i, carry):
        for d in range(2):
            ci = i if d == 0 else n_chunks - 1 - i
            rows = pl.ds(pl.multiple_of(ci * c, c), c)
            state = st_ref[d]
            acc_ref[rows, :] += _dot_nt(qdec_ref[d, rows, :], state.astype(BF16))
            kv = _dot_tn(v_ref[rows, :], kdec_ref[d, rows, :])
            decay = decay_ref[d, pl.ds(pl.multiple_of(ci * c, c), 1), :]
            st_ref[d] = state * decay + jnp.where(head_mask, kv, 0.0)
        return carry

    lax.fori_loop(0, n_chunks, chunk_body, 0, unroll=2)

    gain = gain_ref[...]
    for h in range(GLA_HEADS):
        sl = slice(h * GLA_DV, (h + 1) * GLA_DV)
        o = acc_ref[:, sl]
        y = o * lax.rsqrt(jnp.mean(o * o, axis=-1, keepdims=True) + RMS_EPS) * gain
        gate = g_ref[:, sl].astype(F32)
        o_ref[:, sl] = (y * (gate * jax.nn.sigmoid(gate))).astype(o_ref.dtype)


def _gla(proj, gup, gbias, gain, side_f32):
    b, s, _ = proj.shape
    hk, hv = GLA_HEADS * GLA_DK, GLA_HEADS * GLA_DV
    blk = lambda width, col: pl.BlockSpec((None, s, width), lambda i: (i, 0, col // width))
    side_specs = _side_specs(side_f32, b, lambda i: i)
    outs = pl.pallas_call(
        _gla_kernel,
        grid=(b,),
        in_specs=[blk(2 * hk, COL_QK_G), blk(hv, COL_V_G), blk(hv, COL_G_G), blk(LANES, COL_ALR),
                  pl.BlockSpec((2, LANES, hk), lambda i: (0, 0, 0)),
                  pl.BlockSpec((2, 1, hk), lambda i: (0, 0, 0)),
                  pl.BlockSpec((1, GLA_DV), lambda i: (0, 0))] + side_specs,
        out_specs=[pl.BlockSpec((None, s, hv), lambda i: (i, 0, 0))] + side_specs,
        out_shape=[jax.ShapeDtypeStruct((b, s, hv), BF16)]
                  + [jax.ShapeDtypeStruct(w.shape, BF16) for w in side_f32],
        scratch_shapes=[pltpu.VMEM((2, s, hk), F32),
                        pltpu.VMEM((2, s, hk), BF16),
                        pltpu.VMEM((2, s, hk), BF16),
                        pltpu.VMEM((2, s, hk), F32),
                        pltpu.VMEM((s, hv), F32),
                        pltpu.VMEM((2, hv, hk), F32)],
        compiler_params=_cparams(("parallel",)),
        name="gla",
    )(proj, proj, proj, proj, gup, gbias, gain, *side_f32)
    return outs[0], outs[1:]


J_BELOW = (-MAX_DISTANCE - ATT_KC + 1) // ATT_KC
J_ABOVE = -(-(MAX_DISTANCE + ATT_TQ - 1) // ATT_KC)
N_BAND = J_ABOVE - J_BELOW + 1
assert ATT_KC * J_BELOW + ATT_KC - 1 <= -MAX_DISTANCE
assert ATT_KC * J_ABOVE - (ATT_TQ - 1) >= MAX_DISTANCE


def _band_buckets():
    half = N_BUCKETS // 2
    max_exact = half // 2
    j = (jnp.arange(N_BAND, dtype=jnp.int32) + J_BELOW)[:, None, None]
    rel = (j * ATT_KC + jnp.arange(ATT_KC, dtype=jnp.int32)[None, :, None]
           - jnp.arange(ATT_TQ, dtype=jnp.int32)[None, None, :])
    bucket = jnp.where(rel > 0, half, 0).astype(jnp.int32)
    n = jnp.abs(rel)
    n_large = max_exact + (jnp.log(jnp.maximum(n, max_exact).astype(F32) / max_exact)
                           / math.log(MAX_DISTANCE / max_exact) * (half - max_exact)).astype(jnp.int32)
    n_large = jnp.minimum(n_large, half - 1)
    return bucket + jnp.where(n < max_exact, n, n_large)


def _band_kernel(tab_ref, bucket_ref, o_ref, *, n_heads):
    h = pl.program_id(0)
    for t in range(N_BAND):
        bucket = bucket_ref[t]
        val = jnp.zeros(bucket.shape, F32)
        for bkt in range(N_BUCKETS):
            val = jnp.where(bucket == bkt, tab_ref[bkt * n_heads + h], val)
        o_ref[t] = val * LOG2E


def _bias_band(table):
    n_heads = table.shape[1]
    bucket = jnp.bitwise_and(_band_buckets(), N_BUCKETS - 1)
    return pl.pallas_call(
        functools.partial(_band_kernel, n_heads=n_heads),
        grid=(n_heads,),
        in_specs=[pl.BlockSpec(memory_space=pltpu.SMEM),
                  pl.BlockSpec((N_BAND, ATT_KC, ATT_TQ), lambda h: (0, 0, 0))],
        out_specs=pl.BlockSpec((None, N_BAND, ATT_KC, ATT_TQ), lambda h: (h, 0, 0, 0)),
        out_shape=jax.ShapeDtypeStruct((n_heads, N_BAND, ATT_KC, ATT_TQ), F32),
        compiler_params=_cparams(("parallel",)),
        name="bias_band",
    )(table.reshape(-1), bucket)


def _diff_kernel(lam_ref, q_ref, k_ref, vt_ref, band_ref, gain_ref, o_ref, *, out_scale):
    seq = q_ref.shape[0]
    n_chunks = seq // ATT_KC
    lane = lax.broadcasted_iota(jnp.int32, (ATT_TQ, 2 * DIFF_DK), 1)
    scale = DIFF_DK ** -0.5 * LOG2E
    items = [(r, mp) for r in range(seq // ATT_TQ) for mp in range(2)]
    first_map = {}

    def scores(item):
        r, mp = item
        q = (q_ref[r * ATT_TQ:(r + 1) * ATT_TQ, :].astype(F32) * scale).astype(BF16)
        q = jnp.where((lane >= DIFF_DK) == (mp == 1), q, jnp.zeros_like(q))
        tiles = [band_ref[min(max(c - (ATT_TQ // ATT_KC) * r - J_BELOW, 0), N_BAND - 1)] for c in range(n_chunks)]
        return _dot_nt(k_ref[...], q) + jnp.concatenate(tiles, axis=0)

    def finish(item, s_t):
        r, mp = item
        o_t = _softmax_pv_t(s_t, vt_ref[...])
        if mp == 0:
            first_map[r] = o_t
            return
        o = first_map.pop(r) - lam_ref[0] * o_t
        y = o * lax.rsqrt(jnp.mean(o * o, axis=0, keepdims=True) + RMS_EPS) * gain_ref[...]
        o_ref[r] = (y * out_scale).astype(o_ref.dtype)

    _pipelined(items, scores, finish)


def _diff_attention(proj, vt, band, lam, gain_col, lam_init):
    b, s, _ = proj.shape
    h = DIFF_HEADS
    w = 2 * DIFF_DK
    assert w == LANES
    return pl.pallas_call(
        functools.partial(_diff_kernel, out_scale=1.0 - lam_init),
        grid=(h, b),
        in_specs=[pl.BlockSpec(memory_space=pltpu.SMEM),
                  pl.BlockSpec((None, s, w), lambda hh, bb: (bb, 0, COL_Q_D // w + hh)),
                  pl.BlockSpec((None, s, w), lambda hh, bb: (bb, 0, COL_K_D // w + hh)),
                  pl.BlockSpec((None, DIFF_DV, s), lambda hh, bb: (bb, hh, 0)),
                  pl.BlockSpec((None, N_BAND, ATT_KC, ATT_TQ), lambda hh, bb: (hh, 0, 0, 0)),
                  pl.BlockSpec((DIFF_DV, 1), lambda hh, bb: (0, 0))],
        out_specs=pl.BlockSpec((None, s // ATT_TQ, DIFF_DV, ATT_TQ), lambda hh, bb: (bb, 0, hh, 0)),
        out_shape=jax.ShapeDtypeStruct((b, s // ATT_TQ, h * DIFF_DV, ATT_TQ), BF16),
        compiler_params=_cparams(("parallel", "parallel")),
        name="diff_attn",
    )(lam, proj, proj, vt, band, gain_col)


def _residual_ln(x_ref, o_ref, g_ref, b_ref, row_pairs, col_pairs):
    for j in range(x_ref.shape[0] // ATT_TQ):
        rows = slice(j * ATT_TQ, (j + 1) * ATT_TQ)
        h = ALPHA * x_ref[rows, :]
        for a_ref, w_ref in row_pairs:
            h = h + _dot(a_ref[rows, :], w_ref[...])
        for a_ref, w_ref in col_pairs:
            h = h + _dot_tn(a_ref[j], w_ref[...])
        o_ref[rows, :] = _layer_norm(h, g_ref[...], b_ref[...])


def _tile_specs(n, d, seq, tm):
    per_seq = seq // tm
    rows = lambda width: pl.BlockSpec((tm, width), lambda i: (i, 0))
    cols = lambda k: pl.BlockSpec((None, tm // ATT_TQ, k, ATT_TQ), lambda i: (i // per_seq, i % per_seq, 0, 0))
    full = lambda a: pl.BlockSpec(a.shape, lambda i: (0,) * a.ndim)
    return rows, cols, full


def _outproj_route_kernel(at_ref, w_ref, x_ref, g_ref, b_ref, rw_ref, o_ref, info_ref, info_t_ref, cnt_ref,
                          carry_ref):
    i = pl.program_id(0)
    tm = x_ref.shape[0]

    @pl.when(i == 0)
    def _():
        carry_ref[...] = jnp.zeros_like(carry_ref)

    _residual_ln(x_ref, o_ref, g_ref, b_ref, [], [(at_ref, w_ref)])
    xn = o_ref[...]

    x_hi = xn.astype(BF16)
    x_lo = (xn - x_hi.astype(F32)).astype(BF16)
    logits = _dot(x_hi, rw_ref[0]) + (_dot(x_hi, rw_ref[1]) + _dot(x_lo, rw_ref[0]))
    lane = lax.broadcasted_iota(jnp.int32, logits.shape, 1).astype(F32)
    neg = jnp.float32(-jnp.inf)
    logits = jnp.where(lane < N_EXPERTS, logits, neg)
    v1 = jnp.max(logits, axis=-1, keepdims=True)
    e1 = jnp.min(jnp.where(logits == v1, lane, float(LANES)), axis=-1, keepdims=True)
    rest = jnp.where(lane == e1, neg, logits)
    v2 = jnp.max(rest, axis=-1, keepdims=True)
    e2 = jnp.min(jnp.where(rest == v2, lane, float(LANES)), axis=-1, keepdims=True)
    t = jnp.exp(v2 - v1)
    w1 = 1.0 / (1.0 + t)
    w2 = t / (1.0 + t)

    onehot = jnp.where((lane == e1) | (lane == e2), 1.0, 0.0)
    row = lax.broadcasted_iota(jnp.int32, (tm, tm), 0)
    col = lax.broadcasted_iota(jnp.int32, (tm, tm), 1)
    before = jnp.where(row > col, 1.0, 0.0).astype(BF16)
    prior = carry_ref[...] + _dot(before, onehot.astype(BF16))
    r1 = jnp.sum(jnp.where(lane == e1, prior, 0.0), axis=-1, keepdims=True)
    r2 = jnp.sum(jnp.where(lane == e2, prior, 0.0), axis=-1, keepdims=True)
    carry_ref[...] += jnp.sum(onehot, axis=0, keepdims=True)
    cnt_ref[...] = jnp.broadcast_to(carry_ref[...], cnt_ref.shape)

    info = jnp.zeros(logits.shape, F32)
    for idx, val in enumerate((e1, e2, w1, w2, r1, r2)):
        info = jnp.where(lane == idx, val, info)
    info_ref[...] = info
    info_t_ref[...] = info.T[0:8, :]


def _outproj_route(at, w, x, g, b, router_w2, seq, tm):
    n, d = x.shape
    rows, cols, full = _tile_specs(n, d, seq, tm)
    return pl.pallas_call(
        _outproj_route_kernel,
        grid=(n // tm,),
        in_specs=[cols(at.shape[2]), full(w), rows(d), full(g), full(b), full(router_w2)],
        out_specs=[rows(d), rows(LANES),
                   pl.BlockSpec((8, tm), lambda i: (0, i)),
                   pl.BlockSpec((8, LANES), lambda i: (0, 0))],
        out_shape=[jax.ShapeDtypeStruct((n, d), F32),
                   jax.ShapeDtypeStruct((n, LANES), F32),
                   jax.ShapeDtypeStruct((8, n), F32),
                   jax.ShapeDtypeStruct((8, LANES), F32)],
        scratch_shapes=[pltpu.VMEM((1, LANES), F32)],
        compiler_params=_cparams(("arbitrary",)),
        name="outproj_route",
    )(at, w, x, g, b, router_w2)


def _swiglu_acc(xb, wg_ref, wu_ref, wd_ref, acc_ref):
    n_chunks = wg_ref.shape[-1] // FF_CHUNK
    for c in range(n_chunks):
        cols = slice(c * FF_CHUNK, (c + 1) * FF_CHUNK)
        gate = _dot(xb, wg_ref[:, cols])
        up = _dot(xb, wu_ref[:, cols])
        hidden = (gate * jax.nn.sigmoid(gate) * up).astype(BF16)
        part = _dot(hidden, wd_ref[cols, :])
        if c == 0:
            acc_ref[...] = part
        else:
            acc_ref[...] += part


def _mix_ffn_kernel(a_ref, at_ref, wa_ref, wt_ref, x_ref, g1_ref, b1_ref, wg_ref, wu_ref, wd_ref,
                    g2_ref, b2_ref, o_ref, h_ref):
    _residual_ln(x_ref, h_ref, g1_ref, b1_ref, [(a_ref, wa_ref)], [(at_ref, wt_ref)])
    h = h_ref[...]
    _swiglu_acc(h.astype(BF16), wg_ref, wu_ref, wd_ref, o_ref)
    o_ref[...] = _layer_norm(ALPHA * h + o_ref[...], g2_ref[...], b2_ref[...])


def _mix_ffn(a, at, wa, wt, x, g1, b1, wg, wu, wd, g2, b2, seq, tm):
    n, d = x.shape
    rows, cols, _ = _tile_specs(n, d, seq, tm)
    once = lambda arr: pl.BlockSpec(arr.shape, lambda i: (0,) * arr.ndim, pipeline_mode=pl.Buffered(1))
    return pl.pallas_call(
        _mix_ffn_kernel,
        grid=(n // tm,),
        in_specs=[rows(a.shape[1]), cols(at.shape[2]), once(wa), once(wt), rows(d), once(g1), once(b1),
                  once(wg), once(wu), once(wd), once(g2), once(b2)],
        out_specs=rows(d),
        out_shape=jax.ShapeDtypeStruct((n, d), F32),
        scratch_shapes=[pltpu.VMEM((tm, d), F32)],
        compiler_params=_cparams(("parallel",)),
        name="mix_ffn",
    )(a, at, wa, wt, x, g1, b1, wg, wu, wd, g2, b2)


def _moe_kernel(te_ref, nu_ref, x_ref, wg_ref, wu_ref, wd_ref, o_ref):
    t = pl.program_id(0)

    @pl.when(t < nu_ref[0])
    def _():
        _swiglu_acc(x_ref[...].astype(BF16), wg_ref, wu_ref, wd_ref, o_ref)

    @pl.when(t >= nu_ref[0])
    def _():
        o_ref[...] = jnp.zeros_like(o_ref)


def _moe_ffn(x_sorted, tile_expert, n_used, wg, wu, wd):
    p, d = x_sorted.shape
    f = wg.shape[2]
    tm = MOE_TILE
    grid_spec = pltpu.PrefetchScalarGridSpec(
        num_scalar_prefetch=2,
        grid=(p // tm,),
        in_specs=[pl.BlockSpec((tm, d), lambda t, te, nu: (jnp.minimum(t, nu[0] - 1), 0)),
                  pl.BlockSpec((None, d, f), lambda t, te, nu: (te[t], 0, 0)),
                  pl.BlockSpec((None, d, f), lambda t, te, nu: (te[t], 0, 0)),
                  pl.BlockSpec((None, f, d), lambda t, te, nu: (te[t], 0, 0))],
        out_specs=pl.BlockSpec((tm, d), lambda t, te, nu: (t, 0)),
    )
    return pl.pallas_call(
        _moe_kernel,
        grid_spec=grid_spec,
        out_shape=jax.ShapeDtypeStruct((p, d), F32),
        compiler_params=_cparams(("arbitrary",)),
        name="moe_ffn",
    )(tile_expert, n_used, x_sorted, wg, wu, wd)


def _scatter_kernel(pos_ref, pad_ref, x_ref, o_hbm, zero_ref, sem, zsem):
    i = pl.program_id(0)
    tm = x_ref.shape[0]
    n_tiles = o_hbm.shape[0] // MOE_TILE

    @pl.when(i == 0)
    def _():
        zero_ref[...] = jnp.zeros_like(zero_ref)
        zero_row = zero_ref.at[pl.ds(0, 1)]
        for e in range(N_EXPERTS):
            start, count = pad_ref[e], pad_ref[N_EXPERTS + e]

            def fill(r, carry):
                pltpu.make_async_copy(zero_row, o_hbm.at[pl.ds(start + r, 1)], zsem).start()
                return carry

            lax.fori_loop(0, count, fill, 0)

            def drain(r, carry):
                pltpu.make_async_copy(zero_row, o_hbm.at[pl.ds(start + r, 1)], zsem).wait()
                return carry

            lax.fori_loop(0, count, drain, 0)

        def fill_tile(t, carry):
            dst = o_hbm.at[pl.ds(pl.multiple_of(t * MOE_TILE, MOE_TILE), MOE_TILE)]
            cp = pltpu.make_async_copy(zero_ref, dst, zsem)
            cp.start()
            cp.wait()
            return carry

        lax.fori_loop(pad_ref[2 * N_EXPERTS], n_tiles, fill_tile, 0)

    def issue(r, carry):
        for kk in range(TOP_K):
            dst = pos_ref[kk * tm + r]
            pltpu.make_async_copy(x_ref.at[pl.ds(r, 1)], o_hbm.at[pl.ds(dst, 1)], sem).start(priority=kk)
        return carry

    lax.fori_loop(0, tm, issue, 0, unroll=DMA_UNROLL)

    for kk in range(TOP_K):
        pltpu.make_async_copy(x_ref, o_hbm.at[pl.ds(0, tm)], sem).wait()


def _scatter_rows(x, pos, pad_info, n_slots, tm):
    n, d = x.shape
    return pl.pallas_call(
        _scatter_kernel,
        grid=(n // tm,),
        in_specs=[pl.BlockSpec((TOP_K * tm,), lambda i: (i,), memory_space=pltpu.SMEM),
                  pl.BlockSpec(memory_space=pltpu.SMEM),
                  pl.BlockSpec((tm, d), lambda i: (i, 0))],
        out_specs=pl.BlockSpec(memory_space=pl.ANY),
        out_shape=jax.ShapeDtypeStruct((n_slots, d), F32),
        scratch_shapes=[pltpu.VMEM((MOE_TILE, d), F32),
                        pltpu.SemaphoreType.DMA(()),
                        pltpu.SemaphoreType.DMA(())],
        compiler_params=_cparams(("arbitrary",)),
        name="scatter_rows",
    )(pos, pad_info, x)


def _combine_kernel(pos_ref, next_ref, x_ref, info_ref, g_ref, b_ref, y_hbm, o_ref, buf_ref, sem):
    i = pl.program_id(0)
    tm = x_ref.shape[0]
    slot = i % 2

    def gather(idx_ref, s):
        def issue(r, carry):
            for kk in range(TOP_K):
                src = idx_ref[kk * tm + r]
                pltpu.make_async_copy(y_hbm.at[pl.ds(src, 1)], buf_ref.at[s, kk, pl.ds(r, 1)],
                                      sem.at[s]).start(priority=kk)
            return carry
        lax.fori_loop(0, tm, issue, 0, unroll=DMA_UNROLL)

    @pl.when(i == 0)
    def _():
        gather(pos_ref, slot)

    @pl.when(i + 1 < pl.num_programs(0))
    def _():
        gather(next_ref, 1 - slot)

    for kk in range(TOP_K):
        pltpu.make_async_copy(y_hbm.at[pl.ds(0, tm)], buf_ref.at[slot, kk], sem.at[slot]).wait()

    info = info_ref[...]
    w1 = info[:, 2:3]
    w2 = info[:, 3:4]
    f = w1 * buf_ref[slot, 0] + w2 * buf_ref[slot, 1]
    o_ref[...] = _layer_norm(ALPHA * x_ref[...] + f, g_ref[...], b_ref[...])


def _combine_ln(x, info, pos, y_sorted, g, b, tm):
    n, d = x.shape
    last = n // tm - 1
    return pl.pallas_call(
        _combine_kernel,
        grid=(n // tm,),
        in_specs=[pl.BlockSpec((TOP_K * tm,), lambda i: (i,), memory_space=pltpu.SMEM),
                  pl.BlockSpec((TOP_K * tm,), lambda i: (jnp.minimum(i + 1, last),), memory_space=pltpu.SMEM),
                  pl.BlockSpec((tm, d), lambda i: (i, 0)),
                  pl.BlockSpec((tm, LANES), lambda i: (i, 0)),
                  pl.BlockSpec((1, d), lambda i: (0, 0)),
                  pl.BlockSpec((1, d), lambda i: (0, 0)),
                  pl.BlockSpec(memory_space=pl.ANY)],
        out_specs=pl.BlockSpec((tm, d), lambda i: (i, 0)),
        out_shape=jax.ShapeDtypeStruct((n, d), F32),
        scratch_shapes=[pltpu.VMEM((2, TOP_K, tm, d), F32),
                        pltpu.SemaphoreType.DMA((2,))],
        compiler_params=_cparams(("arbitrary",)),
        name="combine_ln",
    )(pos, pos, x, info, g, b, y_sorted)


def _mla_proj_kernel(x_ref, win_ref, qg_ref, kvg_ref, wq1_ref, wq2_ref, wk_ref, wvt_ref,
                     cq_ref, sq_ref, ck_ref, sk_ref, q_ref, k_ref, vt_ref):
    c = _dot(x_ref[...].astype(BF16), win_ref[...])
    cq = c[:, :MLA_Q_LORA]
    ckv = c[:, MLA_Q_LORA:MLA_Q_LORA + MLA_KV_LORA]
    off = MLA_Q_LORA + MLA_KV_LORA
    k_rope = c[:, off:off + LANES] * ck_ref[...] + c[:, off + LANES:off + 2 * LANES] * sk_ref[...]
    cq = (cq * lax.rsqrt(jnp.mean(cq * cq, axis=-1, keepdims=True) + RMS_EPS) * qg_ref[...]).astype(BF16)
    ckv = (ckv * lax.rsqrt(jnp.mean(ckv * ckv, axis=-1, keepdims=True) + RMS_EPS) * kvg_ref[...]).astype(BF16)
    q1 = _dot(cq, wq1_ref[...])
    q2 = _dot(cq, wq2_ref[...])
    k1 = _dot(ckv, wk_ref[...])
    vt_ref[...] = _dot_nt(wvt_ref[...], ckv).astype(vt_ref.dtype)
    cos_q, sin_q = cq_ref[...], sq_ref[...]
    for h in range(MLA_HEADS):
        sl = slice(h * LANES, (h + 1) * LANES)
        q_ref[:, sl] = (q1[:, sl] * cos_q + q2[:, sl] * sin_q).astype(q_ref.dtype)
        k_ref[:, sl] = (k1[:, sl] + k_rope).astype(k_ref.dtype)


def _mla_proj(x, win, qg, kvg, wq1, wq2, wk, wvt, tabs, seq, tm):
    n, d = x.shape
    per_seq = seq // tm
    full = lambda a: pl.BlockSpec(a.shape, lambda i: (0,) * a.ndim)
    tab = pl.BlockSpec((tm, LANES), lambda i: (i % per_seq, 0))
    hw = MLA_HEADS * LANES
    dvt = wvt.shape[0]
    return pl.pallas_call(
        _mla_proj_kernel,
        grid=(n // tm,),
        in_specs=[pl.BlockSpec((tm, d), lambda i: (i, 0)), full(win), full(qg), full(kvg),
                  full(wq1), full(wq2), full(wk), full(wvt), tab, tab, tab, tab],
        out_specs=[pl.BlockSpec((tm, hw), lambda i: (i, 0)),
                   pl.BlockSpec((tm, hw), lambda i: (i, 0)),
                   pl.BlockSpec((None, dvt, tm), lambda i: (i // per_seq, 0, i % per_seq))],
        out_shape=[jax.ShapeDtypeStruct((n, hw), BF16),
                   jax.ShapeDtypeStruct((n, hw), BF16),
                   jax.ShapeDtypeStruct((n // seq, dvt, seq), BF16)],
        compiler_params=_cparams(("parallel",)),
        name="mla_proj",
    )(x, win, qg, kvg, wq1, wq2, wk, wvt, *tabs)


def _softmax_pv_t(s_t, vt):
    p = jnp.exp2(s_t - _col_reduce(s_t, jnp.max))
    l = _col_reduce(p, jnp.sum)
    return _dot(vt, p.astype(BF16)) * (1.0 / l)


def _col_reduce(x, op):
    rows, n = x.shape
    slab = COL_REDUCE_SLAB if rows % COL_REDUCE_SLAB == 0 else rows
    return op(op(x.reshape(rows // slab, slab, n), axis=0), axis=0, keepdims=True)


def _pipelined(items, scores, finish):
    s_next = scores(items[0])
    for idx, item in enumerate(items):
        s = s_next
        if idx + 1 < len(items):
            s_next = scores(items[idx + 1])
        finish(item, s)


def _mla_attn_kernel(q_ref, k_ref, vt_ref, *refs):
    n_side = (len(refs) - 1) // 2
    o_ref = refs[n_side]
    _side_cast(refs[:n_side], refs[n_side + 1:])
    n_sub = q_ref.shape[0] // ATT_TQ
    items = [(r, hh) for r in range(n_sub) for hh in range(2)]
    cols = lambda hh: slice(hh * LANES, (hh + 1) * LANES)

    def scores(item):
        r, hh = item
        return _dot_nt(k_ref[:, cols(hh)], q_ref[r * ATT_TQ:(r + 1) * ATT_TQ, cols(hh)])

    def finish(item, s_t):
        r, hh = item
        vrows = slice(hh * MLA_DV, (hh + 1) * MLA_DV)
        o_ref[r, vrows, :] = _softmax_pv_t(s_t, vt_ref[vrows, :]).astype(o_ref.dtype)

    _pipelined(items, scores, finish)


def _mla_attention(q, k, vt, tq, side_f32):
    b, s, _ = q.shape
    pairs = MLA_HEADS // 2
    pair_dv = 2 * MLA_DV
    n_q = s // tq
    side_specs = _side_specs(side_f32, b * pairs * n_q, lambda bb, p, i: (bb * pairs + p) * n_q + i)
    outs = pl.pallas_call(
        _mla_attn_kernel,
        grid=(b, pairs, n_q),
        in_specs=[pl.BlockSpec((None, tq, 2 * LANES), lambda bb, p, i: (bb, i, p)),
                  pl.BlockSpec((None, s, 2 * LANES), lambda bb, p, i: (bb, 0, p)),
                  pl.BlockSpec((None, pair_dv, s), lambda bb, p, i: (bb, p, 0))] + side_specs,
        out_specs=[pl.BlockSpec((None, tq // ATT_TQ, pair_dv, ATT_TQ), lambda bb, p, i: (bb, i, p, 0))] + side_specs,
        out_shape=[jax.ShapeDtypeStruct((b, s // ATT_TQ, MLA_HEADS * MLA_DV, ATT_TQ), BF16)]
                  + [jax.ShapeDtypeStruct(w.shape, BF16) for w in side_f32],
        compiler_params=_cparams(("parallel", "parallel", "parallel")),
        name="mla_attn",
    )(q, k, vt, *side_f32)
    return outs[0], outs[1:]


def _even_in_weight(w):
    hk, hv = GLA_HEADS * GLA_DK, GLA_HEADS * GLA_DV
    widths = (hk, hk, hv, hv, 2 * GLA_RANK, DIFF_HEADS * 2 * DIFF_DK, DIFF_HEADS * 2 * DIFF_DK,
              DIFF_HEADS * DIFF_DV)
    offs = [0]
    for wd_ in widths:
        offs.append(offs[-1] + wd_)
    piece = lambda j: w[:, offs[j]:offs[j + 1]]
    pad = jnp.zeros((w.shape[0], EVEN_COLS - COL_ALR - 2 * GLA_RANK), w.dtype)
    main = jnp.concatenate([piece(0), piece(1), piece(2), piece(3), piece(5), piece(6), piece(4), pad], axis=1)
    return main.astype(BF16), piece(7).T.astype(BF16)


def _rot_half_cols(w):
    half = MLA_ROPE // 2
    shp = w.shape
    g = w.reshape(shp[0], -1, MLA_ROPE)
    return jnp.concatenate([-g[..., half:], g[..., :half]], axis=-1).reshape(shp)


def _mla_weights(w_in, w_uq, w_ukv):
    d = w_in.shape[0]
    dq = MLA_NOPE + MLA_ROPE
    z = lambda rows, cols: jnp.zeros((rows, cols), F32)
    w_kr = w_in[:, MLA_Q_LORA + MLA_KV_LORA:]
    kr_blk = lambda m: jnp.concatenate([z(d, MLA_NOPE), m, z(d, LANES - dq)], axis=1)
    win = jnp.concatenate([w_in[:, :MLA_Q_LORA + MLA_KV_LORA], kr_blk(w_kr), kr_blk(_rot_half_cols(w_kr))],
                          axis=1).astype(BF16)
    uq = w_uq.reshape(MLA_Q_LORA, MLA_HEADS, dq)
    pad_q = jnp.zeros((MLA_Q_LORA, MLA_HEADS, LANES - dq), F32)
    wq1 = jnp.concatenate([uq, pad_q], axis=-1).reshape(MLA_Q_LORA, -1).astype(BF16)
    rot = _rot_half_cols(uq[..., MLA_NOPE:].reshape(MLA_Q_LORA, -1)).reshape(MLA_Q_LORA, MLA_HEADS, MLA_ROPE)
    wq2 = jnp.concatenate([jnp.zeros((MLA_Q_LORA, MLA_HEADS, MLA_NOPE), F32), rot, pad_q],
                          axis=-1).reshape(MLA_Q_LORA, -1).astype(BF16)
    ukv = w_ukv.reshape(MLA_KV_LORA, MLA_HEADS, MLA_NOPE + MLA_DV)
    wk = jnp.concatenate([ukv[..., :MLA_NOPE], jnp.zeros((MLA_KV_LORA, MLA_HEADS, LANES - MLA_NOPE), F32)],
                         axis=-1).reshape(MLA_KV_LORA, -1).astype(BF16)
    wvt = ukv[..., MLA_NOPE:].reshape(MLA_KV_LORA, -1).T.astype(BF16)
    return win, wq1, wq2, wk, wvt


def _rope_tables(seq):
    half = MLA_ROPE // 2
    inv = ROPE_THETA ** (-jnp.arange(half, dtype=F32) / half)
    ang = jnp.arange(seq, dtype=F32)[:, None] * inv[None, :]
    cos = jnp.concatenate([jnp.cos(ang), jnp.cos(ang)], axis=1)
    sin = jnp.concatenate([jnp.sin(ang), jnp.sin(ang)], axis=1)
    dq = MLA_NOPE + MLA_ROPE
    scale = dq ** -0.5 * LOG2E
    lay = lambda a, fill: jnp.concatenate([jnp.full((seq, MLA_NOPE), fill, F32), a,
                                           jnp.zeros((seq, LANES - dq), F32)], axis=1)
    return lay(cos, 1.0) * scale, lay(sin, 0.0) * scale, lay(cos, 0.0), lay(sin, 0.0)


def _routing_plan(info_t, counts, n_tokens):
    tm = MOE_TILE
    cnt = counts[0, :N_EXPERTS].astype(jnp.int32)
    padded = ((cnt + tm - 1) // tm) * tm
    ends = jnp.cumsum(padded)
    starts = ends - padded
    e = info_t[0:TOP_K].astype(jnp.int32)
    rank = info_t[4:4 + TOP_K].astype(jnp.int32)
    pos = rank + sum(jnp.where(e == k, starts[k], 0) for k in range(N_EXPERTS))
    n_tiles = (n_tokens * TOP_K) // tm + N_EXPERTS
    tile_start = jnp.arange(n_tiles, dtype=jnp.int32) * tm
    tile_expert = jnp.minimum(jnp.sum(tile_start[:, None] >= ends[None, :], axis=1), N_EXPERTS - 1)
    n_used = (ends[-1] // tm).reshape(1)
    pad_info = jnp.concatenate([starts + cnt, padded - cnt, n_used])
    return pos, tile_expert.astype(jnp.int32), n_used.astype(jnp.int32), pad_info.astype(jnp.int32), n_tiles * tm


def kernel(x, rel_bias_table, even_w_in, gla_gate_up, gla_gate_bias, gla_norm_gain, diff_lambda, diff_norm_gain, even_w_out, ffn_w_gate, ffn_w_up, ffn_w_down, odd_w_in, mla_q_norm_gain, mla_kv_norm_gain, mla_w_uq, mla_w_ukv, odd_w_out, router_w, moe_w_gate, moe_w_up, moe_w_down, ln_gain, ln_bias):
    b, s, d = x.shape
    n = b * s
    tm = min(512, s)
    assert s % tm == 0 and tm % ATT_TQ == 0 and s % GLA_GROUP == 0
    x2 = x.reshape(n, d)
    row = lambda v: v.reshape(1, -1)

    w_main, w_vdt = _even_in_weight(even_w_in[0])
    proj, vd_t = _in_proj(x2, w_main, w_vdt, s, tm)
    proj = proj.reshape(b, s, EVEN_COLS)
    gup = jnp.zeros((2, LANES, GLA_HEADS * GLA_DK), F32)
    for dd in range(2):
        gup = gup.at[dd, dd * GLA_RANK:(dd + 1) * GLA_RANK].set(gla_gate_up[0, dd])
    o_gla, (ffn_wg, ffn_wu, ffn_wd, w_out) = _gla(
        proj, gup.astype(BF16), gla_gate_bias[0][:, None, :], row(gla_norm_gain[0]),
        [ffn_w_gate[0], ffn_w_up[0], ffn_w_down[0], even_w_out[0]])
    lam_init = 0.8 - 0.6 * math.exp(-0.3 * 0)
    lf = diff_lambda[0]
    lam = (jnp.exp(jnp.sum(lf[0] * lf[1])) - jnp.exp(jnp.sum(lf[2] * lf[3])) + lam_init).reshape(1)
    o_diff_t = _diff_attention(proj, vd_t, _bias_band(rel_bias_table), lam,
                               diff_norm_gain[0].reshape(-1, 1), lam_init)
    hv = GLA_HEADS * GLA_DV
    x2 = _mix_ffn(o_gla.reshape(n, hv), o_diff_t, w_out[:hv], w_out[hv:], x2,
                  row(ln_gain[0, 0]), row(ln_bias[0, 0]), ffn_wg, ffn_wu, ffn_wd,
                  row(ln_gain[0, 1]), row(ln_bias[0, 1]), s, tm)

    win, wq1, wq2, wk, wvt = _mla_weights(odd_w_in[0], mla_w_uq[0], mla_w_ukv[0])
    q, k, v_t = _mla_proj(x2, win, row(mla_q_norm_gain[0]), row(mla_kv_norm_gain[0]), wq1, wq2, wk, wvt,
                          _rope_tables(s), s, tm)
    moe_w = [moe_w_gate[0], moe_w_up[0], moe_w_down[0]]
    o_t, moe_wb = _mla_attention(q.reshape(b, s, -1), k.reshape(b, s, -1), v_t, min(MLA_TQ, s),
                                 [w.reshape(-1, w.shape[-1]) for w in moe_w])
    wg_b, wu_b, wd_b = [wb.reshape(w.shape) for wb, w in zip(moe_wb, moe_w)]
    rw = jnp.concatenate([router_w[0], jnp.zeros((d, LANES - N_EXPERTS), F32)], axis=1)
    rw_hi = rw.astype(BF16)
    rw2 = jnp.stack([rw_hi, (rw - rw_hi.astype(F32)).astype(BF16)])
    xn, info, info_t, counts = _outproj_route(o_t, odd_w_out[0].astype(BF16), x2,
                                              row(ln_gain[1, 0]), row(ln_bias[1, 0]), rw2, s, tm)
    pos, tile_expert, n_used, pad_info, n_slots = _routing_plan(info_t, counts, n)
    pos_t = pos.reshape(TOP_K, n // tm, tm).transpose(1, 0, 2).reshape(-1)
    x_sorted = _scatter_rows(xn, pos_t, pad_info, n_slots, tm)
    y_sorted = _moe_ffn(x_sorted, tile_expert, n_used, wg_b, wu_b, wd_b)
    out = _combine_ln(xn, info, pos_t, y_sorted, row(ln_gain[1, 1]), row(ln_bias[1, 1]), tm)
    return out.reshape(b, s, d)
```

```python
import functools
import math

import jax
import jax.numpy as jnp
from jax import lax
from jax.experimental import pallas as pl
from jax.experimental.pallas import tpu as pltpu

F32 = jnp.float32
BF16 = jnp.bfloat16

LANES = 128
V7X_VMEM_BYTES = 64 * 1024 * 1024
VMEM_LIMIT = V7X_VMEM_BYTES - 8 * 1024 * 1024

D_MODEL = 1024
GLA_HEADS, GLA_DK, GLA_DV = 4, 64, 128
GLA_RANK, GLA_TEMP, GLA_CHUNK = 16, 16.0, 64
DIFF_HEADS, DIFF_DK, DIFF_DV = 4, 64, 128
N_BUCKETS, MAX_DISTANCE = 32, 128
MLA_HEADS, MLA_Q_LORA, MLA_KV_LORA = 16, 256, 128
MLA_NOPE, MLA_ROPE, MLA_DV = 64, 32, 64
ROPE_THETA = 10000.0
D_FF, N_EXPERTS, TOP_K = 2816, 8, 2
DEPTH = 2
ALPHA = (2 * DEPTH) ** 0.25
LN_EPS, RMS_EPS = 1e-5, 1e-6

COL_QK_G, COL_V_G, COL_G_G = 0, 512, 1024
COL_Q_D, COL_K_D, COL_ALR = 1536, 2048, 2560
EVEN_COLS = 2688

ATT_KC = 128
ATT_TQ = 512
MLA_TQ = 2048
COL_REDUCE_SLAB = 128
LOG2E = math.log2(math.e)

GLA_GROUP = 256
FF_CHUNK = 256
MOE_TILE = 512
DMA_UNROLL = 8


def _cparams(sem):
    return pltpu.CompilerParams(dimension_semantics=sem, vmem_limit_bytes=VMEM_LIMIT)


def _layer_norm(y, g, b):
    mu = jnp.mean(y, axis=-1, keepdims=True)
    d = y - mu
    var = jnp.mean(d * d, axis=-1, keepdims=True)
    return d * lax.rsqrt(var + LN_EPS) * g + b


def _split3(x):
    h1 = x.astype(BF16)
    r1 = x - h1.astype(F32)
    h2 = r1.astype(BF16)
    h3 = (r1 - h2.astype(F32)).astype(BF16)
    return h1, h2, h3


def _dot(a, b):
    return jnp.dot(a, b, preferred_element_type=F32)


def _dot_nt(a, b):
    return lax.dot_general(a, b, (((1,), (1,)), ((), ())), preferred_element_type=F32)


def _dot_tn(a, b):
    return lax.dot_general(a, b, (((0,), (0,)), ((), ())), preferred_element_type=F32)


def _in_proj_kernel(x_ref, w_ref, wvt_ref, o_ref, vt_ref):
    xb = x_ref[...].astype(BF16)
    o_ref[...] = _dot(xb, w_ref[...]).astype(o_ref.dtype)
    vt_ref[...] = _dot_nt(wvt_ref[...], xb).astype(vt_ref.dtype)


def _in_proj(x, w, wvt, seq, tm):
    n, k = x.shape
    m = w.shape[1]
    dvt = wvt.shape[0]
    per_seq = seq // tm
    return pl.pallas_call(
        _in_proj_kernel,
        grid=(n // tm,),
        in_specs=[pl.BlockSpec((tm, k), lambda i: (i, 0)),
                  pl.BlockSpec((k, m), lambda i: (0, 0)),
                  pl.BlockSpec((dvt, k), lambda i: (0, 0))],
        out_specs=[pl.BlockSpec((tm, m), lambda i: (i, 0)),
                   pl.BlockSpec((None, dvt, tm), lambda i: (i // per_seq, 0, i % per_seq))],
        out_shape=[jax.ShapeDtypeStruct((n, m), BF16),
                   jax.ShapeDtypeStruct((n // seq, dvt, seq), BF16)],
        compiler_params=_cparams(("parallel",)),
        name="in_proj",
    )(x, w, wvt)


def _side_cast(in_refs, out_refs):
    for w_ref, wb_ref in zip(in_refs, out_refs):
        wb_ref[...] = w_ref[...].astype(wb_ref.dtype)


def _side_specs(side_f32, n_steps, step_of):
    specs = []
    for w in side_f32:
        assert w.shape[0] % (n_steps * 16) == 0, w.shape
        specs.append(pl.BlockSpec((w.shape[0] // n_steps, w.shape[1]), lambda *ids: (step_of(*ids), 0)))
    return specs


def _gla_kernel(qk_ref, v_ref, g_ref, alr_ref, gup_ref, gbias_ref, gain_ref, *refs):
    n_side = (len(refs) - 7) // 2
    _side_cast(refs[:n_side], refs[n_side + 1:2 * n_side + 1])
    o_ref = refs[n_side]
    la_ref, qdec_ref, kdec_ref, decay_ref, acc_ref, st_ref = refs[2 * n_side + 1:]
    seq = qk_ref.shape[0]
    hk = GLA_HEADS * GLA_DK
    hv = GLA_HEADS * GLA_DV
    grp = GLA_GROUP
    c = GLA_CHUNK

    alr = alr_ref[...]
    for d in range(2):
        logits = _dot(alr, gup_ref[d]) + gbias_ref[d]
        log_sig = jnp.minimum(logits, 0.0) - jnp.log(1.0 + jnp.exp(-jnp.abs(logits)))
        la_ref[d] = log_sig / GLA_TEMP

    row = lax.broadcasted_iota(jnp.int32, (grp, grp), 0)
    col = lax.broadcasted_iota(jnp.int32, (grp, grp), 1)
    same = (row // c) == (col // c)
    tri = [same & (row >= col), same & (row <= col)]
    cum_inc = [jnp.where(t, 1.0, 0.0).astype(BF16) for t in tri]
    cum_rem = [jnp.where(same & (row < col), 1.0, 0.0).astype(BF16),
               jnp.where(same & (row > col), 1.0, 0.0).astype(BF16)]
    lane_k = lax.broadcasted_iota(jnp.int32, (grp, hk), 1) // GLA_DK
    scale = GLA_DK ** -0.5

    def group_body(r, carry):
        rows = pl.ds(pl.multiple_of(r * grp, grp), grp)
        q = qk_ref[rows, 0:hk].astype(F32)
        k = qk_ref[rows, hk:2 * hk].astype(F32)
        v = v_ref[rows, :]
        o_heads = [jnp.zeros((grp, GLA_DV), F32) for _ in range(GLA_HEADS)]
        for d in range(2):
            parts = _split3(la_ref[d, rows, :])
            bcum = sum(_dot(cum_inc[d], p) for p in parts)
            brem = sum(_dot(cum_rem[d], p) for p in parts)
            q_dec = q * jnp.exp(bcum) * scale
            k_inv = (k * jnp.exp(-bcum)).astype(BF16)
            qdec_ref[d, rows, :] = q_dec.astype(BF16)
            kdec_ref[d, rows, :] = (k * jnp.exp(brem)).astype(BF16)
            decay_ref[d, rows, :] = jnp.exp(bcum + brem)
            for h in range(GLA_HEADS):
                q_h = jnp.where(lane_k == h, q_dec, 0.0).astype(BF16)
                attn = jnp.where(tri[d], _dot_nt(q_h, k_inv), 0.0)
                o_heads[h] = o_heads[h] + _dot(attn.astype(BF16), v[:, h * GLA_DV:(h + 1) * GLA_DV])
        for h in range(GLA_HEADS):
            acc_ref[rows, h * GLA_DV:(h + 1) * GLA_DV] = o_heads[h]
        return carry

    lax.fori_loop(0, seq // grp, group_body, 0)

    srow = lax.broadcasted_iota(jnp.int32, (hv, hk), 0) // GLA_DV
    scol = lax.broadcasted_iota(jnp.int32, (hv, hk), 1) // GLA_DK
    head_mask = srow == scol
    n_chunks = seq // c
    st_ref[...] = jnp.zeros_like(st_ref)

    def chunk_body(i, carry):
        for d in range(2):
            ci = i if d == 0 else n_chunks - 1 - i
            rows = pl.ds(pl.multiple_of(ci * c, c), c)
            state = st_ref[d]
            acc_ref[rows, :] += _dot_nt(qdec_ref[d, rows, :], state.astype(BF16))
            kv = _dot_tn(v_ref[rows, :], kdec_ref[d, rows, :])
            decay = decay_ref[d, pl.ds(pl.multiple_of(ci * c, c), 1), :]
            st_ref[d] = state * decay + jnp.where(head_mask, kv, 0.0)
        return carry

    lax.fori_loop(0, n_chunks, chunk_body, 0, unroll=2)

    gain = gain_ref[...]
    for h in range(GLA_HEADS):
        sl = slice(h * GLA_DV, (h + 1) * GLA_DV)
        o = acc_ref[:, sl]
        y = o * lax.rsqrt(jnp.mean(o * o, axis=-1, keepdims=True) + RMS_EPS) * gain
        gate = g_ref[:, sl].astype(F32)
        o_ref[:, sl] = (y * (gate * jax.nn.sigmoid(gate))).astype(o_ref.dtype)


def _gla(proj, gup, gbias, gain, side_f32):
    b, s, _ = proj.shape
    hk, hv = GLA_HEADS * GLA_DK, GLA_HEADS * GLA_DV
    blk = lambda width, col: pl.BlockSpec((None, s, width), lambda i: (i, 0, col // width))
    side_specs = _side_specs(side_f32, b, lambda i: i)
    outs = pl.pallas_call(
        _gla_kernel,
        grid=(b,),
        in_specs=[blk(2 * hk, COL_QK_G), blk(hv, COL_V_G), blk(hv, COL_G_G), blk(LANES, COL_ALR),
                  pl.BlockSpec((2, LANES, hk), lambda i: (0, 0, 0)),
                  pl.BlockSpec((2, 1, hk), lambda i: (0, 0, 0)),
                  pl.BlockSpec((1, GLA_DV), lambda i: (0, 0))] + side_specs,
        out_specs=[pl.BlockSpec((None, s, hv), lambda i: (i, 0, 0))] + side_specs,
        out_shape=[jax.ShapeDtypeStruct((b, s, hv), BF16)]
                  + [jax.ShapeDtypeStruct(w.shape, BF16) for w in side_f32],
        scratch_shapes=[pltpu.VMEM((2, s, hk), F32),
                        pltpu.VMEM((2, s, hk), BF16),
                        pltpu.VMEM((2, s, hk), BF16),
                        pltpu.VMEM((2, s, hk), F32),
                        pltpu.VMEM((s, hv), F32),
                        pltpu.VMEM((2, hv, hk), F32)],
        compiler_params=_cparams(("parallel",)),
        name="gla",
    )(proj, proj, proj, proj, gup, gbias, gain, *side_f32)
    return outs[0], outs[1:]


J_BELOW = (-MAX_DISTANCE - ATT_KC + 1) // ATT_KC
J_ABOVE = -(-(MAX_DISTANCE + ATT_TQ - 1) // ATT_KC)
N_BAND = J_ABOVE - J_BELOW + 1
assert ATT_KC * J_BELOW + ATT_KC - 1 <= -MAX_DISTANCE
assert ATT_KC * J_ABOVE - (ATT_TQ - 1) >= MAX_DISTANCE


def _band_buckets():
    half = N_BUCKETS // 2
    max_exact = half // 2
    j = (jnp.arange(N_BAND, dtype=jnp.int32) + J_BELOW)[:, None, None]
    rel = (j * ATT_KC + jnp.arange(ATT_KC, dtype=jnp.int32)[None, :, None]
           - jnp.arange(ATT_TQ, dtype=jnp.int32)[None, None, :])
    bucket = jnp.where(rel > 0, half, 0).astype(jnp.int32)
    n = jnp.abs(rel)
    n_large = max_exact + (jnp.log(jnp.maximum(n, max_exact).astype(F32) / max_exact)
                           / math.log(MAX_DISTANCE / max_exact) * (half - max_exact)).astype(jnp.int32)
    n_large = jnp.minimum(n_large, half - 1)
    return bucket + jnp.where(n < max_exact, n, n_large)


def _band_kernel(tab_ref, bucket_ref, o_ref, *, n_heads):
    h = pl.program_id(0)
    for t in range(N_BAND):
        bucket = bucket_ref[t]
        val = jnp.zeros(bucket.shape, F32)
        for bkt in range(N_BUCKETS):
            val = jnp.where(bucket == bkt, tab_ref[bkt * n_heads + h], val)
        o_ref[t] = val * LOG2E


def _bias_band(table):
    n_heads = table.shape[1]
    bucket = jnp.bitwise_and(_band_buckets(), N_BUCKETS - 1)
    return pl.pallas_call(
        functools.partial(_band_kernel, n_heads=n_heads),
        grid=(n_heads,),
        in_specs=[pl.BlockSpec(memory_space=pltpu.SMEM),
                  pl.BlockSpec((N_BAND, ATT_KC, ATT_TQ), lambda h: (0, 0, 0))],
        out_specs=pl.BlockSpec((None, N_BAND, ATT_KC, ATT_TQ), lambda h: (h, 0, 0, 0)),
        out_shape=jax.ShapeDtypeStruct((n_heads, N_BAND, ATT_KC, ATT_TQ), F32),
        compiler_params=_cparams(("parallel",)),
        name="bias_band",
    )(table.reshape(-1), bucket)


def _diff_kernel(lam_ref, q_ref, k_ref, vt_ref, band_ref, gain_ref, o_ref, *, out_scale):
    seq = q_ref.shape[0]
    n_chunks = seq // ATT_KC
    lane = lax.broadcasted_iota(jnp.int32, (ATT_TQ, 2 * DIFF_DK), 1)
    scale = DIFF_DK ** -0.5 * LOG2E
    items = [(r, mp) for r in range(seq // ATT_TQ) for mp in range(2)]
    first_map = {}

    def scores(item):
        r, mp = item
        q = (q_ref[r * ATT_TQ:(r + 1) * ATT_TQ, :].astype(F32) * scale).astype(BF16)
        q = jnp.where((lane >= DIFF_DK) == (mp == 1), q, jnp.zeros_like(q))
        tiles = [band_ref[min(max(c - (ATT_TQ // ATT_KC) * r - J_BELOW, 0), N_BAND - 1)] for c in range(n_chunks)]
        return _dot_nt(k_ref[...], q) + jnp.concatenate(tiles, axis=0)

    def finish(item, s_t):
        r, mp = item
        o_t = _softmax_pv_t(s_t, vt_ref[...])
        if mp == 0:
            first_map[r] = o_t
            return
        o = first_map.pop(r) - lam_ref[0] * o_t
        y = o * lax.rsqrt(jnp.mean(o * o, axis=0, keepdims=True) + RMS_EPS) * gain_ref[...]
        o_ref[r] = (y * out_scale).astype(o_ref.dtype)

    _pipelined(items, scores, finish)


def _diff_attention(proj, vt, band, lam, gain_col, lam_init):
    b, s, _ = proj.shape
    h = DIFF_HEADS
    w = 2 * DIFF_DK
    assert w == LANES
    return pl.pallas_call(
        functools.partial(_diff_kernel, out_scale=1.0 - lam_init),
        grid=(h, b),
        in_specs=[pl.BlockSpec(memory_space=pltpu.SMEM),
                  pl.BlockSpec((None, s, w), lambda hh, bb: (bb, 0, COL_Q_D // w + hh)),
                  pl.BlockSpec((None, s, w), lambda hh, bb: (bb, 0, COL_K_D // w + hh)),
                  pl.BlockSpec((None, DIFF_DV, s), lambda hh, bb: (bb, hh, 0)),
                  pl.BlockSpec((None, N_BAND, ATT_KC, ATT_TQ), lambda hh, bb: (hh, 0, 0, 0)),
                  pl.BlockSpec((DIFF_DV, 1), lambda hh, bb: (0, 0))],
        out_specs=pl.BlockSpec((None, s // ATT_TQ, DIFF_DV, ATT_TQ), lambda hh, bb: (bb, 0, hh, 0)),
        out_shape=jax.ShapeDtypeStruct((b, s // ATT_TQ, h * DIFF_DV, ATT_TQ), BF16),
        compiler_params=_cparams(("parallel", "parallel")),
        name="diff_attn",
    )(lam, proj, proj, vt, band, gain_col)


def _residual_ln(x_ref, o_ref, g_ref, b_ref, row_pairs, col_pairs):
    for j in range(x_ref.shape[0] // ATT_TQ):
        rows = slice(j * ATT_TQ, (j + 1) * ATT_TQ)
        h = ALPHA * x_ref[rows, :]
        for a_ref, w_ref in row_pairs:
            h = h + _dot(a_ref[rows, :], w_ref[...])
        for a_ref, w_ref in col_pairs:
            h = h + _dot_tn(a_ref[j], w_ref[...])
        o_ref[rows, :] = _layer_norm(h, g_ref[...], b_ref[...])


def _tile_specs(n, d, seq, tm):
    per_seq = seq // tm
    rows = lambda width: pl.BlockSpec((tm, width), lambda i: (i, 0))
    cols = lambda k: pl.BlockSpec((None, tm // ATT_TQ, k, ATT_TQ), lambda i: (i // per_seq, i % per_seq, 0, 0))
    full = lambda a: pl.BlockSpec(a.shape, lambda i: (0,) * a.ndim)
    return rows, cols, full


def _outproj_route_kernel(at_ref, w_ref, x_ref, g_ref, b_ref, rw_ref, o_ref, info_ref, info_t_ref, cnt_ref,
                          carry_ref):
    i = pl.program_id(0)
    tm = x_ref.shape[0]

    @pl.when(i == 0)
    def _():
        carry_ref[...] = jnp.zeros_like(carry_ref)

    _residual_ln(x_ref, o_ref, g_ref, b_ref, [], [(at_ref, w_ref)])
    xn = o_ref[...]

    x_hi = xn.astype(BF16)
    x_lo = (xn - x_hi.astype(F32)).astype(BF16)
    logits = _dot(x_hi, rw_ref[0]) + (_dot(x_hi, rw_ref[1]) + _dot(x_lo, rw_ref[0]))
    lane = lax.broadcasted_iota(jnp.int32, logits.shape, 1).astype(F32)
    neg = jnp.float32(-jnp.inf)
    logits = jnp.where(lane < N_EXPERTS, logits, neg)
    v1 = jnp.max(logits, axis=-1, keepdims=True)
    e1 = jnp.min(jnp.where(logits == v1, lane, float(LANES)), axis=-1, keepdims=True)
    rest = jnp.where(lane == e1, neg, logits)
    v2 = jnp.max(rest, axis=-1, keepdims=True)
    e2 = jnp.min(jnp.where(rest == v2, lane, float(LANES)), axis=-1, keepdims=True)
    t = jnp.exp(v2 - v1)
    w1 = 1.0 / (1.0 + t)
    w2 = t / (1.0 + t)

    onehot = jnp.where((lane == e1) | (lane == e2), 1.0, 0.0)
    row = lax.broadcasted_iota(jnp.int32, (tm, tm), 0)
    col = lax.broadcasted_iota(jnp.int32, (tm, tm), 1)
    before = jnp.where(row > col, 1.0, 0.0).astype(BF16)
    prior = carry_ref[...] + _dot(before, onehot.astype(BF16))
    r1 = jnp.sum(jnp.where(lane == e1, prior, 0.0), axis=-1, keepdims=True)
    r2 = jnp.sum(jnp.where(lane == e2, prior, 0.0), axis=-1, keepdims=True)
    carry_ref[...] += jnp.sum(onehot, axis=0, keepdims=True)
    cnt_ref[...] = jnp.broadcast_to(carry_ref[...], cnt_ref.shape)

    info = jnp.zeros(logits.shape, F32)
    for idx, val in enumerate((e1, e2, w1, w2, r1, r2)):
        info = jnp.where(lane == idx, val, info)
    info_ref[...] = info
    info_t_ref[...] = info.T[0:8, :]


def _outproj_route(at, w, x, g, b, router_w2, seq, tm):
    n, d = x.shape
    rows, cols, full = _tile_specs(n, d, seq, tm)
    return pl.pallas_call(
        _outproj_route_kernel,
        grid=(n // tm,),
        in_specs=[cols(at.shape[2]), full(w), rows(d), full(g), full(b), full(router_w2)],
        out_specs=[rows(d), rows(LANES),
                   pl.BlockSpec((8, tm), lambda i: (0, i)),
                   pl.BlockSpec((8, LANES), lambda i: (0, 0))],
        out_shape=[jax.ShapeDtypeStruct((n, d), F32),
                   jax.ShapeDtypeStruct((n, LANES), F32),
                   jax.ShapeDtypeStruct((8, n), F32),
                   jax.ShapeDtypeStruct((8, LANES), F32)],
        scratch_shapes=[pltpu.VMEM((1, LANES), F32)],
        compiler_params=_cparams(("arbitrary",)),
        name="outproj_route",
    )(at, w, x, g, b, router_w2)


def _swiglu_acc(xb, wg_ref, wu_ref, wd_ref, acc_ref):
    n_chunks = wg_ref.shape[-1] // FF_CHUNK
    for c in range(n_chunks):
        cols = slice(c * FF_CHUNK, (c + 1) * FF_CHUNK)
        gate = _dot(xb, wg_ref[:, cols])
        up = _dot(xb, wu_ref[:, cols])
        hidden = (gate * jax.nn.sigmoid(gate) * up).astype(BF16)
        part = _dot(hidden, wd_ref[cols, :])
        if c == 0:
            acc_ref[...] = part
        else:
            acc_ref[...] += part


def _mix_ffn_kernel(a_ref, at_ref, wa_ref, wt_ref, x_ref, g1_ref, b1_ref, wg_ref, wu_ref, wd_ref,
                    g2_ref, b2_ref, o_ref, h_ref):
    _residual_ln(x_ref, h_ref, g1_ref, b1_ref, [(a_ref, wa_ref)], [(at_ref, wt_ref)])
    h = h_ref[...]
    _swiglu_acc(h.astype(BF16), wg_ref, wu_ref, wd_ref, o_ref)
    o_ref[...] = _layer_norm(ALPHA * h + o_ref[...], g2_ref[...], b2_ref[...])


def _mix_ffn(a, at, wa, wt, x, g1, b1, wg, wu, wd, g2, b2, seq, tm):
    n, d = x.shape
    rows, cols, _ = _tile_specs(n, d, seq, tm)
    once = lambda arr: pl.BlockSpec(arr.shape, lambda i: (0,) * arr.ndim, pipeline_mode=pl.Buffered(1))
    return pl.pallas_call(
        _mix_ffn_kernel,
        grid=(n // tm,),
        in_specs=[rows(a.shape[1]), cols(at.shape[2]), once(wa), once(wt), rows(d), once(g1), once(b1),
                  once(wg), once(wu), once(wd), once(g2), once(b2)],
        out_specs=rows(d),
        out_shape=jax.ShapeDtypeStruct((n, d), F32),
        scratch_shapes=[pltpu.VMEM((tm, d), F32)],
        compiler_params=_cparams(("parallel",)),
        name="mix_ffn",
    )(a, at, wa, wt, x, g1, b1, wg, wu, wd, g2, b2)


def _moe_kernel(te_ref, nu_ref, x_ref, wg_ref, wu_ref, wd_ref, o_ref):
    t = pl.program_id(0)

    @pl.when(t < nu_ref[0])
    def _():
        _swiglu_acc(x_ref[...].astype(BF16), wg_ref, wu_ref, wd_ref, o_ref)

    @pl.when(t >= nu_ref[0])
    def _():
        o_ref[...] = jnp.zeros_like(o_ref)


def _moe_ffn(x_sorted, tile_expert, n_used, wg, wu, wd):
    p, d = x_sorted.shape
    f = wg.shape[2]
    tm = MOE_TILE
    grid_spec = pltpu.PrefetchScalarGridSpec(
        num_scalar_prefetch=2,
        grid=(p // tm,),
        in_specs=[pl.BlockSpec((tm, d), lambda t, te, nu: (jnp.minimum(t, nu[0] - 1), 0)),
                  pl.BlockSpec((None, d, f), lambda t, te, nu: (te[t], 0, 0)),
                  pl.BlockSpec((None, d, f), lambda t, te, nu: (te[t], 0, 0)),
                  pl.BlockSpec((None, f, d), lambda t, te, nu: (te[t], 0, 0))],
        out_specs=pl.BlockSpec((tm, d), lambda t, te, nu: (t, 0)),
    )
    return pl.pallas_call(
        _moe_kernel,
        grid_spec=grid_spec,
        out_shape=jax.ShapeDtypeStruct((p, d), F32),
        compiler_params=_cparams(("arbitrary",)),
        name="moe_ffn",
    )(tile_expert, n_used, x_sorted, wg, wu, wd)


def _scatter_kernel(pos_ref, pad_ref, x_ref, o_hbm, zero_ref, sem, zsem):
    i = pl.program_id(0)
    tm = x_ref.shape[0]
    n_tiles = o_hbm.shape[0] // MOE_TILE

    @pl.when(i == 0)
    def _():
        zero_ref[...] = jnp.zeros_like(zero_ref)
        zero_row = zero_ref.at[pl.ds(0, 1)]
        for e in range(N_EXPERTS):
            start, count = pad_ref[e], pad_ref[N_EXPERTS + e]

            def fill(r, carry):
                pltpu.make_async_copy(zero_row, o_hbm.at[pl.ds(start + r, 1)], zsem).start()
                return carry

            lax.fori_loop(0, count, fill, 0)

            def drain(r, carry):
                pltpu.make_async_copy(zero_row, o_hbm.at[pl.ds(start + r, 1)], zsem).wait()
                return carry

            lax.fori_loop(0, count, drain, 0)

        def fill_tile(t, carry):
            dst = o_hbm.at[pl.ds(pl.multiple_of(t * MOE_TILE, MOE_TILE), MOE_TILE)]
            cp = pltpu.make_async_copy(zero_ref, dst, zsem)
            cp.start()
            cp.wait()
            return carry

        lax.fori_loop(pad_ref[2 * N_EXPERTS], n_tiles, fill_tile, 0)

    def issue(r, carry):
        for kk in range(TOP_K):
            dst = pos_ref[kk * tm + r]
            pltpu.make_async_copy(x_ref.at[pl.ds(r, 1)], o_hbm.at[pl.ds(dst, 1)], sem).start(priority=kk)
        return carry

    lax.fori_loop(0, tm, issue, 0, unroll=DMA_UNROLL)

    for kk in range(TOP_K):
        pltpu.make_async_copy(x_ref, o_hbm.at[pl.ds(0, tm)], sem).wait()


def _scatter_rows(x, pos, pad_info, n_slots, tm):
    n, d = x.shape
    return pl.pallas_call(
        _scatter_kernel,
        grid=(n // tm,),
        in_specs=[pl.BlockSpec((TOP_K * tm,), lambda i: (i,), memory_space=pltpu.SMEM),
                  pl.BlockSpec(memory_space=pltpu.SMEM),
                  pl.BlockSpec((tm, d), lambda i: (i, 0))],
        out_specs=pl.BlockSpec(memory_space=pl.ANY),
        out_shape=jax.ShapeDtypeStruct((n_slots, d), F32),
        scratch_shapes=[pltpu.VMEM((MOE_TILE, d), F32),
                        pltpu.SemaphoreType.DMA(()),
                        pltpu.SemaphoreType.DMA(())],
        compiler_params=_cparams(("arbitrary",)),
        name="scatter_rows",
    )(pos, pad_info, x)


def _combine_kernel(pos_ref, next_ref, x_ref, info_ref, g_ref, b_ref, y_hbm, o_ref, buf_ref, sem):
    i = pl.program_id(0)
    tm = x_ref.shape[0]
    slot = i % 2

    def gather(idx_ref, s):
        def issue(r, carry):
            for kk in range(TOP_K):
                src = idx_ref[kk * tm + r]
                pltpu.make_async_copy(y_hbm.at[pl.ds(src, 1)], buf_ref.at[s, kk, pl.ds(r, 1)],
                                      sem.at[s]).start(priority=kk)
            return carry
        lax.fori_loop(0, tm, issue, 0, unroll=DMA_UNROLL)

    @pl.when(i == 0)
    def _():
        gather(pos_ref, slot)

    @pl.when(i + 1 < pl.num_programs(0))
    def _():
        gather(next_ref, 1 - slot)

    for kk in range(TOP_K):
        pltpu.make_async_copy(y_hbm.at[pl.ds(0, tm)], buf_ref.at[slot, kk], sem.at[slot]).wait()

    info = info_ref[...]
    w1 = info[:, 2:3]
    w2 = info[:, 3:4]
    f = w1 * buf_ref[slot, 0] + w2 * buf_ref[slot, 1]
    o_ref[...] = _layer_norm(ALPHA * x_ref[...] + f, g_ref[...], b_ref[...])


def _combine_ln(x, info, pos, y_sorted, g, b, tm):
    n, d = x.shape
    last = n // tm - 1
    return pl.pallas_call(
        _combine_kernel,
        grid=(n // tm,),
        in_specs=[pl.BlockSpec((TOP_K * tm,), lambda i: (i,), memory_space=pltpu.SMEM),
                  pl.BlockSpec((TOP_K * tm,), lambda i: (jnp.minimum(i + 1, last),), memory_space=pltpu.SMEM),
                  pl.BlockSpec((tm, d), lambda i: (i, 0)),
                  pl.BlockSpec((tm, LANES), lambda i: (i, 0)),
                  pl.BlockSpec((1, d), lambda i: (0, 0)),
                  pl.BlockSpec((1, d), lambda i: (0, 0)),
                  pl.BlockSpec(memory_space=pl.ANY)],
        out_specs=pl.BlockSpec((tm, d), lambda i: (i, 0)),
        out_shape=jax.ShapeDtypeStruct((n, d), F32),
        scratch_shapes=[pltpu.VMEM((2, TOP_K, tm, d), F32),
                        pltpu.SemaphoreType.DMA((2,))],
        compiler_params=_cparams(("arbitrary",)),
        name="combine_ln",
    )(pos, pos, x, info, g, b, y_sorted)


def _mla_proj_kernel(x_ref, win_ref, qg_ref, kvg_ref, wq1_ref, wq2_ref, wk_ref, wvt_ref,
                     cq_ref, sq_ref, ck_ref, sk_ref, q_ref, k_ref, vt_ref):
    c = _dot(x_ref[...].astype(BF16), win_ref[...])
    cq = c[:, :MLA_Q_LORA]
    ckv = c[:, MLA_Q_LORA:MLA_Q_LORA + MLA_KV_LORA]
    off = MLA_Q_LORA + MLA_KV_LORA
    k_rope = c[:, off:off + LANES] * ck_ref[...] + c[:, off + LANES:off + 2 * LANES] * sk_ref[...]
    cq = (cq * lax.rsqrt(jnp.mean(cq * cq, axis=-1, keepdims=True) + RMS_EPS) * qg_ref[...]).astype(BF16)
    ckv = (ckv * lax.rsqrt(jnp.mean(ckv * ckv, axis=-1, keepdims=True) + RMS_EPS) * kvg_ref[...]).astype(BF16)
    q1 = _dot(cq, wq1_ref[...])
    q2 = _dot(cq, wq2_ref[...])
    k1 = _dot(ckv, wk_ref[...])
    vt_ref[...] = _dot_nt(wvt_ref[...], ckv).astype(vt_ref.dtype)
    cos_q, sin_q = cq_ref[...], sq_ref[...]
    for h in range(MLA_HEADS):
        sl = slice(h * LANES, (h + 1) * LANES)
        q_ref[:, sl] = (q1[:, sl] * cos_q + q2[:, sl] * sin_q).astype(q_ref.dtype)
        k_ref[:, sl] = (k1[:, sl] + k_rope).astype(k_ref.dtype)


def _mla_proj(x, win, qg, kvg, wq1, wq2, wk, wvt, tabs, seq, tm):
    n, d = x.shape
    per_seq = seq // tm
    full = lambda a: pl.BlockSpec(a.shape, lambda i: (0,) * a.ndim)
    tab = pl.BlockSpec((tm, LANES), lambda i: (i % per_seq, 0))
    hw = MLA_HEADS * LANES
    dvt = wvt.shape[0]
    return pl.pallas_call(
        _mla_proj_kernel,
        grid=(n // tm,),
        in_specs=[pl.BlockSpec((tm, d), lambda i: (i, 0)), full(win), full(qg), full(kvg),
                  full(wq1), full(wq2), full(wk), full(wvt), tab, tab, tab, tab],
        out_specs=[pl.BlockSpec((tm, hw), lambda i: (i, 0)),
                   pl.BlockSpec((tm, hw), lambda i: (i, 0)),
                   pl.BlockSpec((None, dvt, tm), lambda i: (i // per_seq, 0, i % per_seq))],
        out_shape=[jax.ShapeDtypeStruct((n, hw), BF16),
                   jax.ShapeDtypeStruct((n, hw), BF16),
                   jax.ShapeDtypeStruct((n // seq, dvt, seq), BF16)],
        compiler_params=_cparams(("parallel",)),
        name="mla_proj",
    )(x, win, qg, kvg, wq1, wq2, wk, wvt, *tabs)


def _softmax_pv_t(s_t, vt):
    p = jnp.exp2(s_t - _col_reduce(s_t, jnp.max))
    l = _col_reduce(p, jnp.sum)
    return _dot(vt, p.astype(BF16)) * (1.0 / l)


def _col_reduce(x, op):
    rows, n = x.shape
    slab = COL_REDUCE_SLAB if rows % COL_REDUCE_SLAB == 0 else rows
    return op(op(x.reshape(rows // slab, slab, n), axis=0), axis=0, keepdims=True)


def _pipelined(items, scores, finish):
    s_next = scores(items[0])
    for idx, item in enumerate(items):
        s = s_next
        if idx + 1 < len(items):
            s_next = scores(items[idx + 1])
        finish(item, s)


def _mla_attn_kernel(q_ref, k_ref, vt_ref, *refs):
    n_side = (len(refs) - 1) // 2
    o_ref = refs[n_side]
    _side_cast(refs[:n_side], refs[n_side + 1:])
    n_sub = q_ref.shape[0] // ATT_TQ
    items = [(r, hh) for r in range(n_sub) for hh in range(2)]
    cols = lambda hh: slice(hh * LANES, (hh + 1) * LANES)

    def scores(item):
        r, hh = item
        return _dot_nt(k_ref[:, cols(hh)], q_ref[r * ATT_TQ:(r + 1) * ATT_TQ, cols(hh)])

    def finish(item, s_t):
        r, hh = item
        vrows = slice(hh * MLA_DV, (hh + 1) * MLA_DV)
        o_ref[r, vrows, :] = _softmax_pv_t(s_t, vt_ref[vrows, :]).astype(o_ref.dtype)

    _pipelined(items, scores, finish)


def _mla_attention(q, k, vt, tq, side_f32):
    b, s, _ = q.shape
    pairs = MLA_HEADS // 2
    pair_dv = 2 * MLA_DV
    n_q = s // tq
    side_specs = _side_specs(side_f32, b * pairs * n_q, lambda bb, p, i: (bb * pairs + p) * n_q + i)
    outs = pl.pallas_call(
        _mla_attn_kernel,
        grid=(b, pairs, n_q),
        in_specs=[pl.BlockSpec((None, tq, 2 * LANES), lambda bb, p, i: (bb, i, p)),
                  pl.BlockSpec((None, s, 2 * LANES), lambda bb, p, i: (bb, 0, p)),
                  pl.BlockSpec((None, pair_dv, s), lambda bb, p, i: (bb, p, 0))] + side_specs,
        out_specs=[pl.BlockSpec((None, tq // ATT_TQ, pair_dv, ATT_TQ), lambda bb, p, i: (bb, i, p, 0))] + side_specs,
        out_shape=[jax.ShapeDtypeStruct((b, s // ATT_TQ, MLA_HEADS * MLA_DV, ATT_TQ), BF16)]
                  + [jax.ShapeDtypeStruct(w.shape, BF16) for w in side_f32],
        compiler_params=_cparams(("parallel", "parallel", "parallel")),
        name="mla_attn",
    )(q, k, vt, *side_f32)
    return outs[0], outs[1:]


def _even_in_weight(w):
    hk, hv = GLA_HEADS * GLA_DK, GLA_HEADS * GLA_DV
    widths = (hk, hk, hv, hv, 2 * GLA_RANK, DIFF_HEADS * 2 * DIFF_DK, DIFF_HEADS * 2 * DIFF_DK,
              DIFF_HEADS * DIFF_DV)
    offs = [0]
    for wd_ in widths:
        offs.append(offs[-1] + wd_)
    piece = lambda j: w[:, offs[j]:offs[j + 1]]
    pad = jnp.zeros((w.shape[0], EVEN_COLS - COL_ALR - 2 * GLA_RANK), w.dtype)
    main = jnp.concatenate([piece(0), piece(1), piece(2), piece(3), piece(5), piece(6), piece(4), pad], axis=1)
    return main.astype(BF16), piece(7).T.astype(BF16)


def _rot_half_cols(w):
    half = MLA_ROPE // 2
    shp = w.shape
    g = w.reshape(shp[0], -1, MLA_ROPE)
    return jnp.concatenate([-g[..., half:], g[..., :half]], axis=-1).reshape(shp)


def _mla_weights(w_in, w_uq, w_ukv):
    d = w_in.shape[0]
    dq = MLA_NOPE + MLA_ROPE
    z = lambda rows, cols: jnp.zeros((rows, cols), F32)
    w_kr = w_in[:, MLA_Q_LORA + MLA_KV_LORA:]
    kr_blk = lambda m: jnp.concatenate([z(d, MLA_NOPE), m, z(d, LANES - dq)], axis=1)
    win = jnp.concatenate([w_in[:, :MLA_Q_LORA + MLA_KV_LORA], kr_blk(w_kr), kr_blk(_rot_half_cols(w_kr))],
                          axis=1).astype(BF16)
    uq = w_uq.reshape(MLA_Q_LORA, MLA_HEADS, dq)
    pad_q = jnp.zeros((MLA_Q_LORA, MLA_HEADS, LANES - dq), F32)
    wq1 = jnp.concatenate([uq, pad_q], axis=-1).reshape(MLA_Q_LORA, -1).astype(BF16)
    rot = _rot_half_cols(uq[..., MLA_NOPE:].reshape(MLA_Q_LORA, -1)).reshape(MLA_Q_LORA, MLA_HEADS, MLA_ROPE)
    wq2 = jnp.concatenate([jnp.zeros((MLA_Q_LORA, MLA_HEADS, MLA_NOPE), F32), rot, pad_q],
                          axis=-1).reshape(MLA_Q_LORA, -1).astype(BF16)
    ukv = w_ukv.reshape(MLA_KV_LORA, MLA_HEADS, MLA_NOPE + MLA_DV)
    wk = jnp.concatenate([ukv[..., :MLA_NOPE], jnp.zeros((MLA_KV_LORA, MLA_HEADS, LANES - MLA_NOPE), F32)],
                         axis=-1).reshape(MLA_KV_LORA, -1).astype(BF16)
    wvt = ukv[..., MLA_NOPE:].reshape(MLA_KV_LORA, -1).T.astype(BF16)
    return win, wq1, wq2, wk, wvt


def _rope_tables(seq):
    half = MLA_ROPE // 2
    inv = ROPE_THETA ** (-jnp.arange(half, dtype=F32) / half)
    ang = jnp.arange(seq, dtype=F32)[:, None] * inv[None, :]
    cos = jnp.concatenate([jnp.cos(ang), jnp.cos(ang)], axis=1)
    sin = jnp.concatenate([jnp.sin(ang), jnp.sin(ang)], axis=1)
    dq = MLA_NOPE + MLA_ROPE
    scale = dq ** -0.5 * LOG2E
    lay = lambda a, fill: jnp.concatenate([jnp.full((seq, MLA_NOPE), fill, F32), a,
                                           jnp.zeros((seq, LANES - dq), F32)], axis=1)
    return lay(cos, 1.0) * scale, lay(sin, 0.0) * scale, lay(cos, 0.0), lay(sin, 0.0)


def _routing_plan(info_t, counts, n_tokens):
    tm = MOE_TILE
    cnt = counts[0, :N_EXPERTS].astype(jnp.int32)
    padded = ((cnt + tm - 1) // tm) * tm
    ends = jnp.cumsum(padded)
    starts = ends - padded
    e = info_t[0:TOP_K].astype(jnp.int32)
    rank = info_t[4:4 + TOP_K].astype(jnp.int32)
    pos = rank + sum(jnp.where(e == k, starts[k], 0) for k in range(N_EXPERTS))
    n_tiles = (n_tokens * TOP_K) // tm + N_EXPERTS
    tile_start = jnp.arange(n_tiles, dtype=jnp.int32) * tm
    tile_expert = jnp.minimum(jnp.sum(tile_start[:, None] >= ends[None, :], axis=1), N_EXPERTS - 1)
    n_used = (ends[-1] // tm).reshape(1)
    pad_info = jnp.concatenate([starts + cnt, padded - cnt, n_used])
    return pos, tile_expert.astype(jnp.int32), n_used.astype(jnp.int32), pad_info.astype(jnp.int32), n_tiles * tm


def kernel(x, rel_bias_table, even_w_in, gla_gate_up, gla_gate_bias, gla_norm_gain, diff_lambda, diff_norm_gain, even_w_out, ffn_w_gate, ffn_w_up, ffn_w_down, odd_w_in, mla_q_norm_gain, mla_kv_norm_gain, mla_w_uq, mla_w_ukv, odd_w_out, router_w, moe_w_gate, moe_w_up, moe_w_down, ln_gain, ln_bias):
    b, s, d = x.shape
    n = b * s
    tm = min(512, s)
    assert s % tm == 0 and tm % ATT_TQ == 0 and s % GLA_GROUP == 0
    x2 = x.reshape(n, d)
    row = lambda v: v.reshape(1, -1)

    w_main, w_vdt = _even_in_weight(even_w_in[0])
    proj, vd_t = _in_proj(x2, w_main, w_vdt, s, tm)
    proj = proj.reshape(b, s, EVEN_COLS)
    gup = jnp.zeros((2, LANES, GLA_HEADS * GLA_DK), F32)
    for dd in range(2):
        gup = gup.at[dd, dd * GLA_RANK:(dd + 1) * GLA_RANK].set(gla_gate_up[0, dd])
    o_gla, (ffn_wg, ffn_wu, ffn_wd, w_out) = _gla(
        proj, gup.astype(BF16), gla_gate_bias[0][:, None, :], row(gla_norm_gain[0]),
        [ffn_w_gate[0], ffn_w_up[0], ffn_w_down[0], even_w_out[0]])
    lam_init = 0.8 - 0.6 * math.exp(-0.3 * 0)
    lf = diff_lambda[0]
    lam = (jnp.exp(jnp.sum(lf[0] * lf[1])) - jnp.exp(jnp.sum(lf[2] * lf[3])) + lam_init).reshape(1)
    o_diff_t = _diff_attention(proj, vd_t, _bias_band(rel_bias_table), lam,
                               diff_norm_gain[0].reshape(-1, 1), lam_init)
    hv = GLA_HEADS * GLA_DV
    x2 = _mix_ffn(o_gla.reshape(n, hv), o_diff_t, w_out[:hv], w_out[hv:], x2,
                  row(ln_gain[0, 0]), row(ln_bias[0, 0]), ffn_wg, ffn_wu, ffn_wd,
                  row(ln_gain[0, 1]), row(ln_bias[0, 1]), s, tm)

    win, wq1, wq2, wk, wvt = _mla_weights(odd_w_in[0], mla_w_uq[0], mla_w_ukv[0])
    q, k, v_t = _mla_proj(x2, win, row(mla_q_norm_gain[0]), row(mla_kv_norm_gain[0]), wq1, wq2, wk, wvt,
                          _rope_tables(s), s, tm)
    moe_w = [moe_w_gate[0], moe_w_up[0], moe_w_down[0]]
    o_t, moe_wb = _mla_attention(q.reshape(b, s, -1), k.reshape(b, s, -1), v_t, min(MLA_TQ, s),
                                 [w.reshape(-1, w.shape[-1]) for w in moe_w])
    wg_b, wu_b, wd_b = [wb.reshape(w.shape) for wb, w in zip(moe_wb, moe_w)]
    rw = jnp.concatenate([router_w[0], jnp.zeros((d, LANES - N_EXPERTS), F32)], axis=1)
    rw_hi = rw.astype(BF16)
    rw2 = jnp.stack([rw_hi, (rw - rw_hi.astype(F32)).astype(BF16)])
    xn, info, info_t, counts = _outproj_route(o_t, odd_w_out[0].astype(BF16), x2,
                                              row(ln_gain[1, 0]), row(ln_bias[1, 0]), rw2, s, tm)
    pos, tile_expert, n_used, pad_info, n_slots = _routing_plan(info_t, counts, n)
    pos_t = pos.reshape(TOP_K, n // tm, tm).transpose(1, 0, 2).reshape(-1)
    x_sorted = _scatter_rows(xn, pos_t, pad_info, n_slots, tm)
    y_sorted = _moe_ffn(x_sorted, tile_expert, n_used, wg_b, wu_b, wd_b)
    out = _combine_ln(xn, info, pos_t, y_sorted, row(ln_gain[1, 1]), row(ln_bias[1, 1]), tm)
    return out.reshape(b, s, d)
```

```python
import functools
import math

import jax
import jax.numpy as jnp
from jax import lax
from jax.experimental import pallas as pl
from jax.experimental.pallas import tpu as pltpu

F32 = jnp.float32
BF16 = jnp.bfloat16

LANES = 128
V7X_VMEM_BYTES = 64 * 1024 * 1024
VMEM_LIMIT = V7X_VMEM_BYTES - 8 * 1024 * 1024

D_MODEL = 1024
GLA_HEADS, GLA_DK, GLA_DV = 4, 64, 128
GLA_RANK, GLA_TEMP, GLA_CHUNK = 16, 16.0, 64
DIFF_HEADS, DIFF_DK, DIFF_DV = 4, 64, 128
N_BUCKETS, MAX_DISTANCE = 32, 128
MLA_HEADS, MLA_Q_LORA, MLA_KV_LORA = 16, 256, 128
MLA_NOPE, MLA_ROPE, MLA_DV = 64, 32, 64
ROPE_THETA = 10000.0
D_FF, N_EXPERTS, TOP_K = 2816, 8, 2
DEPTH = 2
ALPHA = (2 * DEPTH) ** 0.25
LN_EPS, RMS_EPS = 1e-5, 1e-6

COL_QK_G, COL_V_G, COL_G_G = 0, 512, 1024
COL_Q_D, COL_K_D, COL_ALR = 1536, 2048, 2560
EVEN_COLS = 2688

ATT_KC = 128
ATT_TQ = 512
MLA_TQ = 2048
COL_REDUCE_SLAB = 128
LN_ROWS = 256
LOG2E = math.log2(math.e)

GLA_GROUP = 256
FF_CHUNK = 256
MOE_TILE = 512
DMA_UNROLL = 8


def _cparams(sem):
    return pltpu.CompilerParams(dimension_semantics=sem, vmem_limit_bytes=VMEM_LIMIT)


def _layer_norm(y, g, b):
    mu = jnp.mean(y, axis=-1, keepdims=True)
    d = y - mu
    var = jnp.mean(d * d, axis=-1, keepdims=True)
    return d * lax.rsqrt(var + LN_EPS) * g + b


def _split3(x):
    h1 = x.astype(BF16)
    r1 = x - h1.astype(F32)
    h2 = r1.astype(BF16)
    h3 = (r1 - h2.astype(F32)).astype(BF16)
    return h1, h2, h3


def _dot(a, b):
    return jnp.dot(a, b, preferred_element_type=F32)


def _dot_nt(a, b):
    return lax.dot_general(a, b, (((1,), (1,)), ((), ())), preferred_element_type=F32)


def _dot_tn(a, b):
    return lax.dot_general(a, b, (((0,), (0,)), ((), ())), preferred_element_type=F32)


def _in_proj_kernel(x_ref, w_ref, wvt_ref, o_ref, vt_ref):
    xb = x_ref[...].astype(BF16)
    o_ref[...] = _dot(xb, w_ref[...]).astype(o_ref.dtype)
    vt_ref[...] = _dot_nt(wvt_ref[...], xb).astype(vt_ref.dtype)


def _in_proj(x, w, wvt, seq, tm):
    n, k = x.shape
    m = w.shape[1]
    dvt = wvt.shape[0]
    per_seq = seq // tm
    return pl.pallas_call(
        _in_proj_kernel,
        grid=(n // tm,),
        in_specs=[pl.BlockSpec((tm, k), lambda i: (i, 0)),
                  pl.BlockSpec((k, m), lambda i: (0, 0)),
                  pl.BlockSpec((dvt, k), lambda i: (0, 0))],
        out_specs=[pl.BlockSpec((tm, m), lambda i: (i, 0)),
                   pl.BlockSpec((None, dvt, tm), lambda i: (i // per_seq, 0, i % per_seq))],
        out_shape=[jax.ShapeDtypeStruct((n, m), BF16),
                   jax.ShapeDtypeStruct((n // seq, dvt, seq), BF16)],
        compiler_params=_cparams(("parallel",)),
        name="in_proj",
    )(x, w, wvt)


def _side_cast(in_refs, out_refs):
    for w_ref, wb_ref in zip(in_refs, out_refs):
        wb_ref[...] = w_ref[...].astype(wb_ref.dtype)


def _side_specs(side_f32, n_steps, step_of):
    specs = []
    for w in side_f32:
        assert w.shape[0] % (n_steps * 16) == 0, w.shape
        specs.append(pl.BlockSpec((w.shape[0] // n_steps, w.shape[1]), lambda *ids: (step_of(*ids), 0)))
    return specs


def _gla_kernel(qk_ref, v_ref, g_ref, alr_ref, gup_ref, gbias_ref, gain_ref, *refs):
    n_side = (len(refs) - 7) // 2
    _side_cast(refs[:n_side], refs[n_side + 1:2 * n_side + 1])
    o_ref = refs[n_side]
    la_ref, qdec_ref, kdec_ref, decay_ref, acc_ref, st_ref = refs[2 * n_side + 1:]
    seq = qk_ref.shape[0]
    hk = GLA_HEADS * GLA_DK
    hv = GLA_HEADS * GLA_DV
    grp = GLA_GROUP
    c = GLA_CHUNK

    alr = alr_ref[...]
    for d in range(2):
        logits = _dot(alr, gup_ref[d]) + gbias_ref[d]
        log_sig = jnp.minimum(logits, 0.0) - jnp.log(1.0 + jnp.exp(-jnp.abs(logits)))
        la_ref[d] = log_sig / GLA_TEMP

    row = lax.broadcasted_iota(jnp.int32, (grp, grp), 0)
    col = lax.broadcasted_iota(jnp.int32, (grp, grp), 1)
    same = (row // c) == (col // c)
    tri = [same & (row >= col), same & (row <= col)]
    cum_inc = [jnp.where(t, 1.0, 0.0).astype(BF16) for t in tri]
    cum_rem = [jnp.where(same & (row < col), 1.0, 0.0).astype(BF16),
               jnp.where(same & (row > col), 1.0, 0.0).astype(BF16)]
    lane_k = lax.broadcasted_iota(jnp.int32, (grp, hk), 1) // GLA_DK
    scale = GLA_DK ** -0.5

    def group_body(r, carry):
        rows = pl.ds(pl.multiple_of(r * grp, grp), grp)
        q = qk_ref[rows, 0:hk].astype(F32)
        k = qk_ref[rows, hk:2 * hk].astype(F32)
        v = v_ref[rows, :]
        o_heads = [jnp.zeros((grp, GLA_DV), F32) for _ in range(GLA_HEADS)]
        for d in range(2):
            parts = _split3(la_ref[d, rows, :])
            bcum = sum(_dot(cum_inc[d], p) for p in parts)
            brem = sum(_dot(cum_rem[d], p) for p in parts)
            q_dec = q * jnp.exp(bcum) * scale
            k_inv = (k * jnp.exp(-bcum)).astype(BF16)
            qdec_ref[d, rows, :] = q_dec.astype(BF16)
            kdec_ref[d, rows, :] = (k * jnp.exp(brem)).astype(BF16)
            decay_ref[d, rows, :] = jnp.exp(bcum + brem)
            for h in range(GLA_HEADS):
                q_h = jnp.where(lane_k == h, q_dec, 0.0).astype(BF16)
                attn = jnp.where(tri[d], _dot_nt(q_h, k_inv), 0.0)
                o_heads[h] = o_heads[h] + _dot(attn.astype(BF16), v[:, h * GLA_DV:(h + 1) * GLA_DV])
        for h in range(GLA_HEADS):
            acc_ref[rows, h * GLA_DV:(h + 1) * GLA_DV] = o_heads[h]
        return carry

    lax.fori_loop(0, seq // grp, group_body, 0)

    srow = lax.broadcasted_iota(jnp.int32, (hv, hk), 0) // GLA_DV
    scol = lax.broadcasted_iota(jnp.int32, (hv, hk), 1) // GLA_DK
    head_mask = srow == scol
    n_chunks = seq // c
    st_ref[...] = jnp.zeros_like(st_ref)

    def chunk_body(i, carry):
        for d in range(2):
            ci = i if d == 0 else n_chunks - 1 - i
            rows = pl.ds(pl.multiple_of(ci * c, c), c)
            state = st_ref[d]
            acc_ref[rows, :] += _dot_nt(qdec_ref[d, rows, :], state.astype(BF16))
            kv = _dot_tn(v_ref[rows, :], kdec_ref[d, rows, :])
            decay = decay_ref[d, pl.ds(pl.multiple_of(ci * c, c), 1), :]
            st_ref[d] = state * decay + jnp.where(head_mask, kv, 0.0)
        return carry

    lax.fori_loop(0, n_chunks, chunk_body, 0, unroll=2)

    gain = gain_ref[...]
    for h in range(GLA_HEADS):
        sl = slice(h * GLA_DV, (h + 1) * GLA_DV)
        o = acc_ref[:, sl]
        y = o * lax.rsqrt(jnp.mean(o * o, axis=-1, keepdims=True) + RMS_EPS) * gain
        gate = g_ref[:, sl].astype(F32)
        o_ref[:, sl] = (y * (gate * jax.nn.sigmoid(gate))).astype(o_ref.dtype)


def _gla(proj, gup, gbias, gain, side_f32):
    b, s, _ = proj.shape
    hk, hv = GLA_HEADS * GLA_DK, GLA_HEADS * GLA_DV
    blk = lambda width, col: pl.BlockSpec((None, s, width), lambda i: (i, 0, col // width))
    side_specs = _side_specs(side_f32, b, lambda i: i)
    outs = pl.pallas_call(
        _gla_kernel,
        grid=(b,),
        in_specs=[blk(2 * hk, COL_QK_G), blk(hv, COL_V_G), blk(hv, COL_G_G), blk(LANES, COL_ALR),
                  pl.BlockSpec((2, LANES, hk), lambda i: (0, 0, 0)),
                  pl.BlockSpec((2, 1, hk), lambda i: (0, 0, 0)),
                  pl.BlockSpec((1, GLA_DV), lambda i: (0, 0))] + side_specs,
        out_specs=[pl.BlockSpec((None, s, hv), lambda i: (i, 0, 0))] + side_specs,
        out_shape=[jax.ShapeDtypeStruct((b, s, hv), BF16)]
                  + [jax.ShapeDtypeStruct(w.shape, BF16) for w in side_f32],
        scratch_shapes=[pltpu.VMEM((2, s, hk), F32),
                        pltpu.VMEM((2, s, hk), BF16),
                        pltpu.VMEM((2, s, hk), BF16),
                        pltpu.VMEM((2, s, hk), F32),
                        pltpu.VMEM((s, hv), F32),
                        pltpu.VMEM((2, hv, hk), F32)],
        compiler_params=_cparams(("parallel",)),
        name="gla",
    )(proj, proj, proj, proj, gup, gbias, gain, *side_f32)
    return outs[0], outs[1:]


J_BELOW = (-MAX_DISTANCE - ATT_KC + 1) // ATT_KC
J_ABOVE = -(-(MAX_DISTANCE + ATT_TQ - 1) // ATT_KC)
N_BAND = J_ABOVE - J_BELOW + 1
assert ATT_KC * J_BELOW + ATT_KC - 1 <= -MAX_DISTANCE
assert ATT_KC * J_ABOVE - (ATT_TQ - 1) >= MAX_DISTANCE


def _band_buckets():
    half = N_BUCKETS // 2
    max_exact = half // 2
    j = (jnp.arange(N_BAND, dtype=jnp.int32) + J_BELOW)[:, None, None]
    rel = (j * ATT_KC + jnp.arange(ATT_KC, dtype=jnp.int32)[None, :, None]
           - jnp.arange(ATT_TQ, dtype=jnp.int32)[None, None, :])
    bucket = jnp.where(rel > 0, half, 0).astype(jnp.int32)
    n = jnp.abs(rel)
    n_large = max_exact + (jnp.log(jnp.maximum(n, max_exact).astype(F32) / max_exact)
                           / math.log(MAX_DISTANCE / max_exact) * (half - max_exact)).astype(jnp.int32)
    n_large = jnp.minimum(n_large, half - 1)
    return bucket + jnp.where(n < max_exact, n, n_large)


def _band_kernel(tab_ref, bucket_ref, o_ref, *, n_heads):
    h = pl.program_id(0)
    for t in range(N_BAND):
        bucket = bucket_ref[t]
        val = jnp.zeros(bucket.shape, F32)
        for bkt in range(N_BUCKETS):
            val = jnp.where(bucket == bkt, tab_ref[bkt * n_heads + h], val)
        o_ref[t] = val * LOG2E


def _bias_band(table):
    n_heads = table.shape[1]
    bucket = jnp.bitwise_and(_band_buckets(), N_BUCKETS - 1)
    return pl.pallas_call(
        functools.partial(_band_kernel, n_heads=n_heads),
        grid=(n_heads,),
        in_specs=[pl.BlockSpec(memory_space=pltpu.SMEM),
                  pl.BlockSpec((N_BAND, ATT_KC, ATT_TQ), lambda h: (0, 0, 0))],
        out_specs=pl.BlockSpec((None, N_BAND, ATT_KC, ATT_TQ), lambda h: (h, 0, 0, 0)),
        out_shape=jax.ShapeDtypeStruct((n_heads, N_BAND, ATT_KC, ATT_TQ), F32),
        compiler_params=_cparams(("parallel",)),
        name="bias_band",
    )(table.reshape(-1), bucket)


def _diff_kernel(lam_ref, q_ref, k_ref, vt_ref, band_ref, gain_ref, o_ref, *, out_scale):
    seq = q_ref.shape[0]
    n_chunks = seq // ATT_KC
    lane = lax.broadcasted_iota(jnp.int32, (ATT_TQ, 2 * DIFF_DK), 1)
    scale = DIFF_DK ** -0.5 * LOG2E
    items = [(r, mp) for r in range(seq // ATT_TQ) for mp in range(2)]
    first_map = {}

    def scores(item):
        r, mp = item
        q = (q_ref[r * ATT_TQ:(r + 1) * ATT_TQ, :].astype(F32) * scale).astype(BF16)
        q = jnp.where((lane >= DIFF_DK) == (mp == 1), q, jnp.zeros_like(q))
        tiles = [band_ref[min(max(c - (ATT_TQ // ATT_KC) * r - J_BELOW, 0), N_BAND - 1)] for c in range(n_chunks)]
        return _dot_nt(k_ref[...], q) + jnp.concatenate(tiles, axis=0)

    def finish(item, s_t):
        r, mp = item
        o_t = _softmax_pv_t(s_t, vt_ref[...])
        if mp == 0:
            first_map[r] = o_t
            return
        o = first_map.pop(r) - lam_ref[0] * o_t
        y = o * lax.rsqrt(jnp.mean(o * o, axis=0, keepdims=True) + RMS_EPS) * gain_ref[...]
        o_ref[r] = (y * out_scale).astype(o_ref.dtype)

    _pipelined(items, scores, finish)


def _diff_attention(proj, vt, band, lam, gain_col, lam_init):
    b, s, _ = proj.shape
    h = DIFF_HEADS
    w = 2 * DIFF_DK
    assert w == LANES
    return pl.pallas_call(
        functools.partial(_diff_kernel, out_scale=1.0 - lam_init),
        grid=(h, b),
        in_specs=[pl.BlockSpec(memory_space=pltpu.SMEM),
                  pl.BlockSpec((None, s, w), lambda hh, bb: (bb, 0, COL_Q_D // w + hh)),
                  pl.BlockSpec((None, s, w), lambda hh, bb: (bb, 0, COL_K_D // w + hh)),
                  pl.BlockSpec((None, DIFF_DV, s), lambda hh, bb: (bb, hh, 0)),
                  pl.BlockSpec((None, N_BAND, ATT_KC, ATT_TQ), lambda hh, bb: (hh, 0, 0, 0)),
                  pl.BlockSpec((DIFF_DV, 1), lambda hh, bb: (0, 0))],
        out_specs=pl.BlockSpec((None, s // ATT_TQ, DIFF_DV, ATT_TQ), lambda hh, bb: (bb, 0, hh, 0)),
        out_shape=jax.ShapeDtypeStruct((b, s // ATT_TQ, h * DIFF_DV, ATT_TQ), BF16),
        compiler_params=_cparams(("parallel", "parallel")),
        name="diff_attn",
    )(lam, proj, proj, vt, band, gain_col)


def _residual_ln(x_ref, o_ref, g_ref, b_ref, row_pairs, col_pairs):
    for j in range(x_ref.shape[0] // LN_ROWS):
        rows = slice(j * LN_ROWS, (j + 1) * LN_ROWS)
        tile, lane0 = divmod(j * LN_ROWS, ATT_TQ)
        h = ALPHA * x_ref[rows, :]
        for a_ref, w_ref in row_pairs:
            h = h + _dot(a_ref[rows, :], w_ref[...])
        for a_ref, w_ref in col_pairs:
            h = h + _dot_tn(a_ref[tile, :, lane0:lane0 + LN_ROWS], w_ref[...])
        o_ref[rows, :] = _layer_norm(h, g_ref[...], b_ref[...])


def _tile_specs(n, d, seq, tm):
    per_seq = seq // tm
    rows = lambda width: pl.BlockSpec((tm, width), lambda i: (i, 0))
    cols = lambda k: pl.BlockSpec((None, tm // ATT_TQ, k, ATT_TQ), lambda i: (i // per_seq, i % per_seq, 0, 0))
    full = lambda a: pl.BlockSpec(a.shape, lambda i: (0,) * a.ndim)
    return rows, cols, full


def _outproj_route_kernel(at_ref, w_ref, x_ref, g_ref, b_ref, rw_ref, o_ref, info_ref, info_t_ref, cnt_ref,
                          carry_ref):
    i = pl.program_id(0)
    tm = x_ref.shape[0]

    @pl.when(i == 0)
    def _():
        carry_ref[...] = jnp.zeros_like(carry_ref)

    _residual_ln(x_ref, o_ref, g_ref, b_ref, [], [(at_ref, w_ref)])
    xn = o_ref[...]

    x_hi = xn.astype(BF16)
    x_lo = (xn - x_hi.astype(F32)).astype(BF16)
    logits = _dot(x_hi, rw_ref[0]) + (_dot(x_hi, rw_ref[1]) + _dot(x_lo, rw_ref[0]))
    lane = lax.broadcasted_iota(jnp.int32, logits.shape, 1).astype(F32)
    neg = jnp.float32(-jnp.inf)
    logits = jnp.where(lane < N_EXPERTS, logits, neg)
    v1 = jnp.max(logits, axis=-1, keepdims=True)
    e1 = jnp.min(jnp.where(logits == v1, lane, float(LANES)), axis=-1, keepdims=True)
    rest = jnp.where(lane == e1, neg, logits)
    v2 = jnp.max(rest, axis=-1, keepdims=True)
    e2 = jnp.min(jnp.where(rest == v2, lane, float(LANES)), axis=-1, keepdims=True)
    t = jnp.exp(v2 - v1)
    w1 = 1.0 / (1.0 + t)
    w2 = t / (1.0 + t)

    onehot = jnp.where((lane == e1) | (lane == e2), 1.0, 0.0)
    row = lax.broadcasted_iota(jnp.int32, (tm, tm), 0)
    col = lax.broadcasted_iota(jnp.int32, (tm, tm), 1)
    before = jnp.where(row > col, 1.0, 0.0).astype(BF16)
    prior = carry_ref[...] + _dot(before, onehot.astype(BF16))
    r1 = jnp.sum(jnp.where(lane == e1, prior, 0.0), axis=-1, keepdims=True)
    r2 = jnp.sum(jnp.where(lane == e2, prior, 0.0), axis=-1, keepdims=True)
    carry_ref[...] += jnp.sum(onehot, axis=0, keepdims=True)
    cnt_ref[...] = jnp.broadcast_to(carry_ref[...], cnt_ref.shape)

    info = jnp.zeros(logits.shape, F32)
    for idx, val in enumerate((e1, e2, w1, w2, r1, r2)):
        info = jnp.where(lane == idx, val, info)
    info_ref[...] = info
    info_t_ref[...] = info.T[0:8, :]


def _outproj_route(at, w, x, g, b, router_w2, seq, tm):
    n, d = x.shape
    rows, cols, full = _tile_specs(n, d, seq, tm)
    return pl.pallas_call(
        _outproj_route_kernel,
        grid=(n // tm,),
        in_specs=[cols(at.shape[2]), full(w), rows(d), full(g), full(b), full(router_w2)],
        out_specs=[rows(d), rows(LANES),
                   pl.BlockSpec((8, tm), lambda i: (0, i)),
                   pl.BlockSpec((8, LANES), lambda i: (0, 0))],
        out_shape=[jax.ShapeDtypeStruct((n, d), F32),
                   jax.ShapeDtypeStruct((n, LANES), F32),
                   jax.ShapeDtypeStruct((8, n), F32),
                   jax.ShapeDtypeStruct((8, LANES), F32)],
        scratch_shapes=[pltpu.VMEM((1, LANES), F32)],
        compiler_params=_cparams(("arbitrary",)),
        name="outproj_route",
    )(at, w, x, g, b, router_w2)


def _swiglu_acc(xb, wg_ref, wu_ref, wd_ref, acc_ref):
    n_chunks = wg_ref.shape[-1] // FF_CHUNK
    for c in range(n_chunks):
        cols = slice(c * FF_CHUNK, (c + 1) * FF_CHUNK)
        gate = _dot(xb, wg_ref[:, cols])
        up = _dot(xb, wu_ref[:, cols])
        hidden = (gate * jax.nn.sigmoid(gate) * up).astype(BF16)
        part = _dot(hidden, wd_ref[cols, :])
        if c == 0:
            acc_ref[...] = part
        else:
            acc_ref[...] += part


def _mix_ffn_kernel(a_ref, at_ref, wa_ref, wt_ref, x_ref, g1_ref, b1_ref, wg_ref, wu_ref, wd_ref,
                    g2_ref, b2_ref, o_ref, h_ref):
    _residual_ln(x_ref, h_ref, g1_ref, b1_ref, [(a_ref, wa_ref)], [(at_ref, wt_ref)])
    h = h_ref[...]
    _swiglu_acc(h.astype(BF16), wg_ref, wu_ref, wd_ref, o_ref)
    o_ref[...] = _layer_norm(ALPHA * h + o_ref[...], g2_ref[...], b2_ref[...])


def _mix_ffn(a, at, wa, wt, x, g1, b1, wg, wu, wd, g2, b2, seq, tm):
    n, d = x.shape
    rows, cols, _ = _tile_specs(n, d, seq, tm)
    once = lambda arr: pl.BlockSpec(arr.shape, lambda i: (0,) * arr.ndim, pipeline_mode=pl.Buffered(1))
    return pl.pallas_call(
        _mix_ffn_kernel,
        grid=(n // tm,),
        in_specs=[rows(a.shape[1]), cols(at.shape[2]), once(wa), once(wt), rows(d), once(g1), once(b1),
                  once(wg), once(wu), once(wd), once(g2), once(b2)],
        out_specs=rows(d),
        out_shape=jax.ShapeDtypeStruct((n, d), F32),
        scratch_shapes=[pltpu.VMEM((tm, d), F32)],
        compiler_params=_cparams(("parallel",)),
        name="mix_ffn",
    )(a, at, wa, wt, x, g1, b1, wg, wu, wd, g2, b2)


def _moe_kernel(te_ref, nu_ref, x_ref, wg_ref, wu_ref, wd_ref, o_ref):
    t = pl.program_id(0)

    @pl.when(t < nu_ref[0])
    def _():
        _swiglu_acc(x_ref[...].astype(BF16), wg_ref, wu_ref, wd_ref, o_ref)

    @pl.when(t >= nu_ref[0])
    def _():
        o_ref[...] = jnp.zeros_like(o_ref)


def _moe_ffn(x_sorted, tile_expert, n_used, wg, wu, wd):
    p, d = x_sorted.shape
    f = wg.shape[2]
    tm = MOE_TILE
    grid_spec = pltpu.PrefetchScalarGridSpec(
        num_scalar_prefetch=2,
        grid=(p // tm,),
        in_specs=[pl.BlockSpec((tm, d), lambda t, te, nu: (jnp.minimum(t, nu[0] - 1), 0)),
                  pl.BlockSpec((None, d, f), lambda t, te, nu: (te[t], 0, 0)),
                  pl.BlockSpec((None, d, f), lambda t, te, nu: (te[t], 0, 0)),
                  pl.BlockSpec((None, f, d), lambda t, te, nu: (te[t], 0, 0))],
        out_specs=pl.BlockSpec((tm, d), lambda t, te, nu: (t, 0)),
    )
    return pl.pallas_call(
        _moe_kernel,
        grid_spec=grid_spec,
        out_shape=jax.ShapeDtypeStruct((p, d), F32),
        compiler_params=_cparams(("arbitrary",)),
        name="moe_ffn",
    )(tile_expert, n_used, x_sorted, wg, wu, wd)


def _scatter_kernel(pos_ref, pad_ref, x_ref, o_hbm, zero_ref, sem, zsem):
    i = pl.program_id(0)
    tm = x_ref.shape[0]
    n_tiles = o_hbm.shape[0] // MOE_TILE

    @pl.when(i == 0)
    def _():
        zero_ref[...] = jnp.zeros_like(zero_ref)
        zero_row = zero_ref.at[pl.ds(0, 1)]
        for e in range(N_EXPERTS):
            start, count = pad_ref[e], pad_ref[N_EXPERTS + e]

            def fill(r, carry):
                pltpu.make_async_copy(zero_row, o_hbm.at[pl.ds(start + r, 1)], zsem).start()
                return carry

            lax.fori_loop(0, count, fill, 0)

            def drain(r, carry):
                pltpu.make_async_copy(zero_row, o_hbm.at[pl.ds(start + r, 1)], zsem).wait()
                return carry

            lax.fori_loop(0, count, drain, 0)

        def fill_tile(t, carry):
            dst = o_hbm.at[pl.ds(pl.multiple_of(t * MOE_TILE, MOE_TILE), MOE_TILE)]
            cp = pltpu.make_async_copy(zero_ref, dst, zsem)
            cp.start()
            cp.wait()
            return carry

        lax.fori_loop(pad_ref[2 * N_EXPERTS], n_tiles, fill_tile, 0)

    def issue(r, carry):
        for kk in range(TOP_K):
            dst = pos_ref[kk * tm + r]
            pltpu.make_async_copy(x_ref.at[pl.ds(r, 1)], o_hbm.at[pl.ds(dst, 1)], sem).start(priority=kk)
        return carry

    lax.fori_loop(0, tm, issue, 0, unroll=DMA_UNROLL)

    for kk in range(TOP_K):
        pltpu.make_async_copy(x_ref, o_hbm.at[pl.ds(0, tm)], sem).wait()


def _scatter_rows(x, pos, pad_info, n_slots, tm):
    n, d = x.shape
    return pl.pallas_call(
        _scatter_kernel,
        grid=(n // tm,),
        in_specs=[pl.BlockSpec((TOP_K * tm,), lambda i: (i,), memory_space=pltpu.SMEM),
                  pl.BlockSpec(memory_space=pltpu.SMEM),
                  pl.BlockSpec((tm, d), lambda i: (i, 0))],
        out_specs=pl.BlockSpec(memory_space=pl.ANY),
        out_shape=jax.ShapeDtypeStruct((n_slots, d), F32),
        scratch_shapes=[pltpu.VMEM((MOE_TILE, d), F32),
                        pltpu.SemaphoreType.DMA(()),
                        pltpu.SemaphoreType.DMA(())],
        compiler_params=_cparams(("arbitrary",)),
        name="scatter_rows",
    )(pos, pad_info, x)


def _combine_kernel(pos_ref, next_ref, x_ref, info_ref, g_ref, b_ref, y_hbm, o_ref, buf_ref, sem):
    i = pl.program_id(0)
    tm = x_ref.shape[0]
    slot = i % 2

    def gather(idx_ref, s):
        def issue(r, carry):
            for kk in range(TOP_K):
                src = idx_ref[kk * tm + r]
                pltpu.make_async_copy(y_hbm.at[pl.ds(src, 1)], buf_ref.at[s, kk, pl.ds(r, 1)],
                                      sem.at[s]).start(priority=kk)
            return carry
        lax.fori_loop(0, tm, issue, 0, unroll=DMA_UNROLL)

    @pl.when(i == 0)
    def _():
        gather(pos_ref, slot)

    @pl.when(i + 1 < pl.num_programs(0))
    def _():
        gather(next_ref, 1 - slot)

    for kk in range(TOP_K):
        pltpu.make_async_copy(y_hbm.at[pl.ds(0, tm)], buf_ref.at[slot, kk], sem.at[slot]).wait()

    info = info_ref[...]
    w1 = info[:, 2:3]
    w2 = info[:, 3:4]
    f = w1 * buf_ref[slot, 0] + w2 * buf_ref[slot, 1]
    o_ref[...] = _layer_norm(ALPHA * x_ref[...] + f, g_ref[...], b_ref[...])


def _combine_ln(x, info, pos, y_sorted, g, b, tm):
    n, d = x.shape
    last = n // tm - 1
    return pl.pallas_call(
        _combine_kernel,
        grid=(n // tm,),
        in_specs=[pl.BlockSpec((TOP_K * tm,), lambda i: (i,), memory_space=pltpu.SMEM),
                  pl.BlockSpec((TOP_K * tm,), lambda i: (jnp.minimum(i + 1, last),), memory_space=pltpu.SMEM),
                  pl.BlockSpec((tm, d), lambda i: (i, 0)),
                  pl.BlockSpec((tm, LANES), lambda i: (i, 0)),
                  pl.BlockSpec((1, d), lambda i: (0, 0)),
                  pl.BlockSpec((1, d), lambda i: (0, 0)),
                  pl.BlockSpec(memory_space=pl.ANY)],
        out_specs=pl.BlockSpec((tm, d), lambda i: (i, 0)),
        out_shape=jax.ShapeDtypeStruct((n, d), F32),
        scratch_shapes=[pltpu.VMEM((2, TOP_K, tm, d), F32),
                        pltpu.SemaphoreType.DMA((2,))],
        compiler_params=_cparams(("arbitrary",)),
        name="combine_ln",
    )(pos, pos, x, info, g, b, y_sorted)


def _mla_proj_kernel(x_ref, win_ref, qg_ref, kvg_ref, wq1_ref, wq2_ref, wk_ref, wvt_ref,
                     cq_ref, sq_ref, ck_ref, sk_ref, q_ref, k_ref, vt_ref):
    c = _dot(x_ref[...].astype(BF16), win_ref[...])
    cq = c[:, :MLA_Q_LORA]
    ckv = c[:, MLA_Q_LORA:MLA_Q_LORA + MLA_KV_LORA]
    off = MLA_Q_LORA + MLA_KV_LORA
    k_rope = c[:, off:off + LANES] * ck_ref[...] + c[:, off + LANES:off + 2 * LANES] * sk_ref[...]
    cq = (cq * lax.rsqrt(jnp.mean(cq * cq, axis=-1, keepdims=True) + RMS_EPS) * qg_ref[...]).astype(BF16)
    ckv = (ckv * lax.rsqrt(jnp.mean(ckv * ckv, axis=-1, keepdims=True) + RMS_EPS) * kvg_ref[...]).astype(BF16)
    q1 = _dot(cq, wq1_ref[...])
    q2 = _dot(cq, wq2_ref[...])
    k1 = _dot(ckv, wk_ref[...])
    vt_ref[...] = _dot_nt(wvt_ref[...], ckv).astype(vt_ref.dtype)
    cos_q, sin_q = cq_ref[...], sq_ref[...]
    for h in range(MLA_HEADS):
        sl = slice(h * LANES, (h + 1) * LANES)
        q_ref[:, sl] = (q1[:, sl] * cos_q + q2[:, sl] * sin_q).astype(q_ref.dtype)
        k_ref[:, sl] = (k1[:, sl] + k_rope).astype(k_ref.dtype)


def _mla_proj(x, win, qg, kvg, wq1, wq2, wk, wvt, tabs, seq, tm):
    n, d = x.shape
    per_seq = seq // tm
    full = lambda a: pl.BlockSpec(a.shape, lambda i: (0,) * a.ndim)
    tab = pl.BlockSpec((tm, LANES), lambda i: (i % per_seq, 0))
    hw = MLA_HEADS * LANES
    dvt = wvt.shape[0]
    return pl.pallas_call(
        _mla_proj_kernel,
        grid=(n // tm,),
        in_specs=[pl.BlockSpec((tm, d), lambda i: (i, 0)), full(win), full(qg), full(kvg),
                  full(wq1), full(wq2), full(wk), full(wvt), tab, tab, tab, tab],
        out_specs=[pl.BlockSpec((tm, hw), lambda i: (i, 0)),
                   pl.BlockSpec((tm, hw), lambda i: (i, 0)),
                   pl.BlockSpec((None, dvt, tm), lambda i: (i // per_seq, 0, i % per_seq))],
        out_shape=[jax.ShapeDtypeStruct((n, hw), BF16),
                   jax.ShapeDtypeStruct((n, hw), BF16),
                   jax.ShapeDtypeStruct((n // seq, dvt, seq), BF16)],
        compiler_params=_cparams(("parallel",)),
        name="mla_proj",
    )(x, win, qg, kvg, wq1, wq2, wk, wvt, *tabs)


def _softmax_pv_t(s_t, vt):
    p = jnp.exp2(s_t - _col_reduce(s_t, jnp.max))
    l = _col_reduce(p, jnp.sum)
    return _dot(vt, p.astype(BF16)) * (1.0 / l)


def _col_reduce(x, op):
    rows, n = x.shape
    slab = COL_REDUCE_SLAB if rows % COL_REDUCE_SLAB == 0 else rows
    return op(op(x.reshape(rows // slab, slab, n), axis=0), axis=0, keepdims=True)


def _pipelined(items, scores, finish):
    s_next = scores(items[0])
    for idx, item in enumerate(items):
        s = s_next
        if idx + 1 < len(items):
            s_next = scores(items[idx + 1])
        finish(item, s)


def _mla_attn_kernel(q_ref, k_ref, vt_ref, *refs):
    n_side = (len(refs) - 1) // 2
    o_ref = refs[n_side]
    _side_cast(refs[:n_side], refs[n_side + 1:])
    n_sub = q_ref.shape[0] // ATT_TQ
    items = [(r, hh) for r in range(n_sub) for hh in range(2)]
    cols = lambda hh: slice(hh * LANES, (hh + 1) * LANES)

    def scores(item):
        r, hh = item
        return _dot_nt(k_ref[:, cols(hh)], q_ref[r * ATT_TQ:(r + 1) * ATT_TQ, cols(hh)])

    def finish(item, s_t):
        r, hh = item
        vrows = slice(hh * MLA_DV, (hh + 1) * MLA_DV)
        o_ref[r, vrows, :] = _softmax_pv_t(s_t, vt_ref[vrows, :]).astype(o_ref.dtype)

    _pipelined(items, scores, finish)


def _mla_attention(q, k, vt, tq, side_f32):
    b, s, _ = q.shape
    pairs = MLA_HEADS // 2
    pair_dv = 2 * MLA_DV
    n_q = s // tq
    side_specs = _side_specs(side_f32, b * pairs * n_q, lambda bb, p, i: (bb * pairs + p) * n_q + i)
    outs = pl.pallas_call(
        _mla_attn_kernel,
        grid=(b, pairs, n_q),
        in_specs=[pl.BlockSpec((None, tq, 2 * LANES), lambda bb, p, i: (bb, i, p)),
                  pl.BlockSpec((None, s, 2 * LANES), lambda bb, p, i: (bb, 0, p)),
                  pl.BlockSpec((None, pair_dv, s), lambda bb, p, i: (bb, p, 0))] + side_specs,
        out_specs=[pl.BlockSpec((None, tq // ATT_TQ, pair_dv, ATT_TQ), lambda bb, p, i: (bb, i, p, 0))] + side_specs,
        out_shape=[jax.ShapeDtypeStruct((b, s // ATT_TQ, MLA_HEADS * MLA_DV, ATT_TQ), BF16)]
                  + [jax.ShapeDtypeStruct(w.shape, BF16) for w in side_f32],
        compiler_params=_cparams(("parallel", "parallel", "parallel")),
        name="mla_attn",
    )(q, k, vt, *side_f32)
    return outs[0], outs[1:]


def _even_in_weight(w):
    hk, hv = GLA_HEADS * GLA_DK, GLA_HEADS * GLA_DV
    widths = (hk, hk, hv, hv, 2 * GLA_RANK, DIFF_HEADS * 2 * DIFF_DK, DIFF_HEADS * 2 * DIFF_DK,
              DIFF_HEADS * DIFF_DV)
    offs = [0]
    for wd_ in widths:
        offs.append(offs[-1] + wd_)
    piece = lambda j: w[:, offs[j]:offs[j + 1]]
    pad = jnp.zeros((w.shape[0], EVEN_COLS - COL_ALR - 2 * GLA_RANK), w.dtype)
    main = jnp.concatenate([piece(0), piece(1), piece(2), piece(3), piece(5), piece(6), piece(4), pad], axis=1)
    return main.astype(BF16), piece(7).T.astype(BF16)


def _rot_half_cols(w):
    half = MLA_ROPE // 2
    shp = w.shape
    g = w.reshape(shp[0], -1, MLA_ROPE)
    return jnp.concatenate([-g[..., half:], g[..., :half]], axis=-1).reshape(shp)


def _mla_weights(w_in, w_uq, w_ukv):
    d = w_in.shape[0]
    dq = MLA_NOPE + MLA_ROPE
    z = lambda rows, cols: jnp.zeros((rows, cols), F32)
    w_kr = w_in[:, MLA_Q_LORA + MLA_KV_LORA:]
    kr_blk = lambda m: jnp.concatenate([z(d, MLA_NOPE), m, z(d, LANES - dq)], axis=1)
    win = jnp.concatenate([w_in[:, :MLA_Q_LORA + MLA_KV_LORA], kr_blk(w_kr), kr_blk(_rot_half_cols(w_kr))],
                          axis=1).astype(BF16)
    uq = w_uq.reshape(MLA_Q_LORA, MLA_HEADS, dq)
    pad_q = jnp.zeros((MLA_Q_LORA, MLA_HEADS, LANES - dq), F32)
    wq1 = jnp.concatenate([uq, pad_q], axis=-1).reshape(MLA_Q_LORA, -1).astype(BF16)
    rot = _rot_half_cols(uq[..., MLA_NOPE:].reshape(MLA_Q_LORA, -1)).reshape(MLA_Q_LORA, MLA_HEADS, MLA_ROPE)
    wq2 = jnp.concatenate([jnp.zeros((MLA_Q_LORA, MLA_HEADS, MLA_NOPE), F32), rot, pad_q],
                          axis=-1).reshape(MLA_Q_LORA, -1).astype(BF16)
    ukv = w_ukv.reshape(MLA_KV_LORA, MLA_HEADS, MLA_NOPE + MLA_DV)
    wk = jnp.concatenate([ukv[..., :MLA_NOPE], jnp.zeros((MLA_KV_LORA, MLA_HEADS, LANES - MLA_NOPE), F32)],
                         axis=-1).reshape(MLA_KV_LORA, -1).astype(BF16)
    wvt = ukv[..., MLA_NOPE:].reshape(MLA_KV_LORA, -1).T.astype(BF16)
    return win, wq1, wq2, wk, wvt


def _rope_tables(seq):
    half = MLA_ROPE // 2
    inv = ROPE_THETA ** (-jnp.arange(half, dtype=F32) / half)
    ang = jnp.arange(seq, dtype=F32)[:, None] * inv[None, :]
    cos = jnp.concatenate([jnp.cos(ang), jnp.cos(ang)], axis=1)
    sin = jnp.concatenate([jnp.sin(ang), jnp.sin(ang)], axis=1)
    dq = MLA_NOPE + MLA_ROPE
    scale = dq ** -0.5 * LOG2E
    lay = lambda a, fill: jnp.concatenate([jnp.full((seq, MLA_NOPE), fill, F32), a,
                                           jnp.zeros((seq, LANES - dq), F32)], axis=1)
    return lay(cos, 1.0) * scale, lay(sin, 0.0) * scale, lay(cos, 0.0), lay(sin, 0.0)


def _routing_plan(info_t, counts, n_tokens):
    tm = MOE_TILE
    cnt = counts[0, :N_EXPERTS].astype(jnp.int32)
    padded = ((cnt + tm - 1) // tm) * tm
    ends = jnp.cumsum(padded)
    starts = ends - padded
    e = info_t[0:TOP_K].astype(jnp.int32)
    rank = info_t[4:4 + TOP_K].astype(jnp.int32)
    pos = rank + sum(jnp.where(e == k, starts[k], 0) for k in range(N_EXPERTS))
    n_tiles = (n_tokens * TOP_K) // tm + N_EXPERTS
    tile_start = jnp.arange(n_tiles, dtype=jnp.int32) * tm
    tile_expert = jnp.minimum(jnp.sum(tile_start[:, None] >= ends[None, :], axis=1), N_EXPERTS - 1)
    n_used = (ends[-1] // tm).reshape(1)
    pad_info = jnp.concatenate([starts + cnt, padded - cnt, n_used])
    return pos, tile_expert.astype(jnp.int32), n_used.astype(jnp.int32), pad_info.astype(jnp.int32), n_tiles * tm


def kernel(x, rel_bias_table, even_w_in, gla_gate_up, gla_gate_bias, gla_norm_gain, diff_lambda, diff_norm_gain, even_w_out, ffn_w_gate, ffn_w_up, ffn_w_down, odd_w_in, mla_q_norm_gain, mla_kv_norm_gain, mla_w_uq, mla_w_ukv, odd_w_out, router_w, moe_w_gate, moe_w_up, moe_w_down, ln_gain, ln_bias):
    b, s, d = x.shape
    n = b * s
    tm = min(512, s)
    assert s % tm == 0 and tm % ATT_TQ == 0 and s % GLA_GROUP == 0
    x2 = x.reshape(n, d)
    row = lambda v: v.reshape(1, -1)

    w_main, w_vdt = _even_in_weight(even_w_in[0])
    proj, vd_t = _in_proj(x2, w_main, w_vdt, s, tm)
    proj = proj.reshape(b, s, EVEN_COLS)
    gup = jnp.zeros((2, LANES, GLA_HEADS * GLA_DK), F32)
    for dd in range(2):
        gup = gup.at[dd, dd * GLA_RANK:(dd + 1) * GLA_RANK].set(gla_gate_up[0, dd])
    o_gla, (ffn_wg, ffn_wu, ffn_wd, w_out) = _gla(
        proj, gup.astype(BF16), gla_gate_bias[0][:, None, :], row(gla_norm_gain[0]),
        [ffn_w_gate[0], ffn_w_up[0], ffn_w_down[0], even_w_out[0]])
    lam_init = 0.8 - 0.6 * math.exp(-0.3 * 0)
    lf = diff_lambda[0]
    lam = (jnp.exp(jnp.sum(lf[0] * lf[1])) - jnp.exp(jnp.sum(lf[2] * lf[3])) + lam_init).reshape(1)
    o_diff_t = _diff_attention(proj, vd_t, _bias_band(rel_bias_table), lam,
                               diff_norm_gain[0].reshape(-1, 1), lam_init)
    hv = GLA_HEADS * GLA_DV
    x2 = _mix_ffn(o_gla.reshape(n, hv), o_diff_t, w_out[:hv], w_out[hv:], x2,
                  row(ln_gain[0, 0]), row(ln_bias[0, 0]), ffn_wg, ffn_wu, ffn_wd,
                  row(ln_gain[0, 1]), row(ln_bias[0, 1]), s, tm)

    win, wq1, wq2, wk, wvt = _mla_weights(odd_w_in[0], mla_w_uq[0], mla_w_ukv[0])
    q, k, v_t = _mla_proj(x2, win, row(mla_q_norm_gain[0]), row(mla_kv_norm_gain[0]), wq1, wq2, wk, wvt,
                          _rope_tables(s), s, tm)
    moe_w = [moe_w_gate[0], moe_w_up[0], moe_w_down[0]]
    o_t, moe_wb = _mla_attention(q.reshape(b, s, -1), k.reshape(b, s, -1), v_t, min(MLA_TQ, s),
                                 [w.reshape(-1, w.shape[-1]) for w in moe_w])
    wg_b, wu_b, wd_b = [wb.reshape(w.shape) for wb, w in zip(moe_wb, moe_w)]
    rw = jnp.concatenate([router_w[0], jnp.zeros((d, LANES - N_EXPERTS), F32)], axis=1)
    rw_hi = rw.astype(BF16)
    rw2 = jnp.stack([rw_hi, (rw - rw_hi.astype(F32)).astype(BF16)])
    xn, info, info_t, counts = _outproj_route(o_t, odd_w_out[0].astype(BF16), x2,
                                              row(ln_gain[1, 0]), row(ln_bias[1, 0]), rw2, s, tm)
    pos, tile_expert, n_used, pad_info, n_slots = _routing_plan(info_t, counts, n)
    pos_t = pos.reshape(TOP_K, n // tm, tm).transpose(1, 0, 2).reshape(-1)
    x_sorted = _scatter_rows(xn, pos_t, pad_info, n_slots, tm)
    y_sorted = _moe_ffn(x_sorted, tile_expert, n_used, wg_b, wu_b, wd_b)
    out = _combine_ln(xn, info, pos_t, y_sorted, row(ln_gain[1, 1]), row(ln_bias[1, 1]), tm)
    return out.reshape(b, s, d)
```

```python
import functools
import math

import jax
import jax.numpy as jnp
from jax import lax
from jax.experimental import pallas as pl
from jax.experimental.pallas import tpu as pltpu

F32 = jnp.float32
BF16 = jnp.bfloat16

LANES = 128
V7X_VMEM_BYTES = 64 * 1024 * 1024
VMEM_LIMIT = V7X_VMEM_BYTES - 8 * 1024 * 1024

GLA_HEADS, GLA_DK, GLA_DV = 4, 64, 128
GLA_RANK, GLA_TEMP, GLA_CHUNK = 16, 16.0, 64
DIFF_HEADS, DIFF_DK, DIFF_DV = 4, 64, 128
N_BUCKETS, MAX_DISTANCE = 32, 128
MLA_HEADS, MLA_Q_LORA, MLA_KV_LORA = 16, 256, 128
MLA_NOPE, MLA_ROPE, MLA_DV = 64, 32, 64
ROPE_THETA = 10000.0
N_EXPERTS, TOP_K = 8, 2
DEPTH = 2
ALPHA = (2 * DEPTH) ** 0.25
LN_EPS, RMS_EPS = 1e-5, 1e-6

COL_QK_G, COL_V_G, COL_G_G = 0, 512, 1024
COL_Q_D, COL_K_D, COL_ALR = 1536, 2048, 2560
EVEN_COLS = 2688

ATT_KC = 128
ATT_TQ = 512
MLA_TQ = 2048
COL_REDUCE_SLAB = 128
LN_ROWS = 256
LOG2E = math.log2(math.e)

GLA_GROUP = 256
FF_CHUNK = 256
MOE_TILE = 512
DMA_UNROLL = 8


def _cparams(sem):
    return pltpu.CompilerParams(dimension_semantics=sem, vmem_limit_bytes=VMEM_LIMIT)


def _layer_norm(y, g, b):
    mu = jnp.mean(y, axis=-1, keepdims=True)
    d = y - mu
    var = jnp.mean(d * d, axis=-1, keepdims=True)
    return d * lax.rsqrt(var + LN_EPS) * g + b


def _split3(x):
    h1 = x.astype(BF16)
    r1 = x - h1.astype(F32)
    h2 = r1.astype(BF16)
    h3 = (r1 - h2.astype(F32)).astype(BF16)
    return h1, h2, h3


def _dot(a, b):
    return jnp.dot(a, b, preferred_element_type=F32)


def _dot_nt(a, b):
    return lax.dot_general(a, b, (((1,), (1,)), ((), ())), preferred_element_type=F32)


def _dot_tn(a, b):
    return lax.dot_general(a, b, (((0,), (0,)), ((), ())), preferred_element_type=F32)


def _in_proj_kernel(x_ref, w_ref, wvt_ref, o_ref, vt_ref):
    xb = x_ref[...].astype(BF16)
    o_ref[...] = _dot(xb, w_ref[...]).astype(o_ref.dtype)
    vt_ref[...] = _dot_nt(wvt_ref[...], xb).astype(vt_ref.dtype)


def _in_proj(x, w, wvt, seq, tm):
    n, k = x.shape
    m = w.shape[1]
    dvt = wvt.shape[0]
    per_seq = seq // tm
    return pl.pallas_call(
        _in_proj_kernel,
        grid=(n // tm,),
        in_specs=[pl.BlockSpec((tm, k), lambda i: (i, 0)),
                  pl.BlockSpec((k, m), lambda i: (0, 0)),
                  pl.BlockSpec((dvt, k), lambda i: (0, 0))],
        out_specs=[pl.BlockSpec((tm, m), lambda i: (i, 0)),
                   pl.BlockSpec((None, dvt, tm), lambda i: (i // per_seq, 0, i % per_seq))],
        out_shape=[jax.ShapeDtypeStruct((n, m), BF16),
                   jax.ShapeDtypeStruct((n // seq, dvt, seq), BF16)],
        compiler_params=_cparams(("parallel",)),
        name="in_proj",
    )(x, w, wvt)


def _side_cast(in_refs, out_refs):
    for w_ref, wb_ref in zip(in_refs, out_refs):
        wb_ref[...] = w_ref[...].astype(wb_ref.dtype)


def _side_specs(side_f32, n_steps, step_of):
    specs = []
    for w in side_f32:
        assert w.shape[0] % (n_steps * 16) == 0, w.shape
        specs.append(pl.BlockSpec((w.shape[0] // n_steps, w.shape[1]), lambda *ids: (step_of(*ids), 0)))
    return specs


def _gla_kernel(qk_ref, v_ref, g_ref, alr_ref, gup_ref, gbias_ref, gain_ref, *refs):
    n_side = (len(refs) - 7) // 2
    _side_cast(refs[:n_side], refs[n_side + 1:2 * n_side + 1])
    o_ref = refs[n_side]
    la_ref, qdec_ref, kdec_ref, decay_ref, acc_ref, st_ref = refs[2 * n_side + 1:]
    seq = qk_ref.shape[0]
    hk = GLA_HEADS * GLA_DK
    hv = GLA_HEADS * GLA_DV
    grp = GLA_GROUP
    c = GLA_CHUNK

    alr = alr_ref[...]
    for d in range(2):
        logits = _dot(alr, gup_ref[d]) + gbias_ref[d]
        log_sig = jnp.minimum(logits, 0.0) - jnp.log(1.0 + jnp.exp(-jnp.abs(logits)))
        la_ref[d] = log_sig / GLA_TEMP

    row = lax.broadcasted_iota(jnp.int32, (grp, grp), 0)
    col = lax.broadcasted_iota(jnp.int32, (grp, grp), 1)
    same = (row // c) == (col // c)
    tri = [same & (row >= col), same & (row <= col)]
    cum_inc = [jnp.where(t, 1.0, 0.0).astype(BF16) for t in tri]
    cum_rem = [jnp.where(same & (row < col), 1.0, 0.0).astype(BF16),
               jnp.where(same & (row > col), 1.0, 0.0).astype(BF16)]
    lane_k = lax.broadcasted_iota(jnp.int32, (grp, hk), 1) // GLA_DK
    scale = GLA_DK ** -0.5

    def group_body(r, carry):
        rows = pl.ds(pl.multiple_of(r * grp, grp), grp)
        q = qk_ref[rows, 0:hk].astype(F32)
        k = qk_ref[rows, hk:2 * hk].astype(F32)
        v = v_ref[rows, :]
        o_heads = [jnp.zeros((grp, GLA_DV), F32) for _ in range(GLA_HEADS)]
        for d in range(2):
            parts = _split3(la_ref[d, rows, :])
            bcum = sum(_dot(cum_inc[d], p) for p in parts)
            brem = sum(_dot(cum_rem[d], p) for p in parts)
            q_dec = q * jnp.exp(bcum) * scale
            k_inv = (k * jnp.exp(-bcum)).astype(BF16)
            qdec_ref[d, rows, :] = q_dec.astype(BF16)
            kdec_ref[d, rows, :] = (k * jnp.exp(brem)).astype(BF16)
            decay_ref[d, rows, :] = jnp.exp(bcum + brem)
            for h in range(GLA_HEADS):
                q_h = jnp.where(lane_k == h, q_dec, 0.0).astype(BF16)
                attn = jnp.where(tri[d], _dot_nt(q_h, k_inv), 0.0)
                o_heads[h] = o_heads[h] + _dot(attn.astype(BF16), v[:, h * GLA_DV:(h + 1) * GLA_DV])
        for h in range(GLA_HEADS):
            acc_ref[rows, h * GLA_DV:(h + 1) * GLA_DV] = o_heads[h]
        return carry

    lax.fori_loop(0, seq // grp, group_body, 0)

    srow = lax.broadcasted_iota(jnp.int32, (hv, hk), 0) // GLA_DV
    scol = lax.broadcasted_iota(jnp.int32, (hv, hk), 1) // GLA_DK
    head_mask = srow == scol
    n_chunks = seq // c
    st_ref[...] = jnp.zeros_like(st_ref)

    def chunk_body(i, carry):
        for d in range(2):
            ci = i if d == 0 else n_chunks - 1 - i
            rows = pl.ds(pl.multiple_of(ci * c, c), c)
            state = st_ref[d]
            acc_ref[rows, :] += _dot_nt(qdec_ref[d, rows, :], state.astype(BF16))
            kv = _dot_tn(v_ref[rows, :], kdec_ref[d, rows, :])
            decay = decay_ref[d, pl.ds(pl.multiple_of(ci * c, c), 1), :]
            st_ref[d] = state * decay + jnp.where(head_mask, kv, 0.0)
        return carry

    lax.fori_loop(0, n_chunks, chunk_body, 0, unroll=2)

    gain = gain_ref[...]
    for h in range(GLA_HEADS):
        sl = slice(h * GLA_DV, (h + 1) * GLA_DV)
        o = acc_ref[:, sl]
        y = o * lax.rsqrt(jnp.mean(o * o, axis=-1, keepdims=True) + RMS_EPS) * gain
        gate = g_ref[:, sl].astype(F32)
        o_ref[:, sl] = (y * (gate * jax.nn.sigmoid(gate))).astype(o_ref.dtype)


def _gla(proj, gup, gbias, gain, side_f32):
    b, s, _ = proj.shape
    hk, hv = GLA_HEADS * GLA_DK, GLA_HEADS * GLA_DV
    blk = lambda width, col: pl.BlockSpec((None, s, width), lambda i: (i, 0, col // width))
    side_specs = _side_specs(side_f32, b, lambda i: i)
    outs = pl.pallas_call(
        _gla_kernel,
        grid=(b,),
        in_specs=[blk(2 * hk, COL_QK_G), blk(hv, COL_V_G), blk(hv, COL_G_G), blk(LANES, COL_ALR),
                  pl.BlockSpec((2, LANES, hk), lambda i: (0, 0, 0)),
                  pl.BlockSpec((2, 1, hk), lambda i: (0, 0, 0)),
                  pl.BlockSpec((1, GLA_DV), lambda i: (0, 0))] + side_specs,
        out_specs=[pl.BlockSpec((None, s, hv), lambda i: (i, 0, 0))] + side_specs,
        out_shape=[jax.ShapeDtypeStruct((b, s, hv), BF16)]
                  + [jax.ShapeDtypeStruct(w.shape, BF16) for w in side_f32],
        scratch_shapes=[pltpu.VMEM((2, s, hk), F32),
                        pltpu.VMEM((2, s, hk), BF16),
                        pltpu.VMEM((2, s, hk), BF16),
                        pltpu.VMEM((2, s, hk), F32),
                        pltpu.VMEM((s, hv), F32),
                        pltpu.VMEM((2, hv, hk), F32)],
        compiler_params=_cparams(("parallel",)),
        name="gla",
    )(proj, proj, proj, proj, gup, gbias, gain, *side_f32)
    return outs[0], outs[1:]


J_BELOW = (-MAX_DISTANCE - ATT_KC + 1) // ATT_KC
J_ABOVE = -(-(MAX_DISTANCE + ATT_TQ - 1) // ATT_KC)
N_BAND = J_ABOVE - J_BELOW + 1
assert ATT_KC * J_BELOW + ATT_KC - 1 <= -MAX_DISTANCE
assert ATT_KC * J_ABOVE - (ATT_TQ - 1) >= MAX_DISTANCE


def _band_buckets():
    half = N_BUCKETS // 2
    max_exact = half // 2
    j = (jnp.arange(N_BAND, dtype=jnp.int32) + J_BELOW)[:, None, None]
    rel = (j * ATT_KC + jnp.arange(ATT_KC, dtype=jnp.int32)[None, :, None]
           - jnp.arange(ATT_TQ, dtype=jnp.int32)[None, None, :])
    bucket = jnp.where(rel > 0, half, 0).astype(jnp.int32)
    n = jnp.abs(rel)
    n_large = max_exact + (jnp.log(jnp.maximum(n, max_exact).astype(F32) / max_exact)
                           / math.log(MAX_DISTANCE / max_exact) * (half - max_exact)).astype(jnp.int32)
    n_large = jnp.minimum(n_large, half - 1)
    return bucket + jnp.where(n < max_exact, n, n_large)


def _band_kernel(tab_ref, bucket_ref, o_ref, *, n_heads):
    h = pl.program_id(0)
    for t in range(N_BAND):
        bucket = bucket_ref[t]
        val = jnp.zeros(bucket.shape, F32)
        for bkt in range(N_BUCKETS):
            val = jnp.where(bucket == bkt, tab_ref[bkt * n_heads + h], val)
        o_ref[t] = val * LOG2E


def _bias_band(table):
    n_heads = table.shape[1]
    bucket = jnp.bitwise_and(_band_buckets(), N_BUCKETS - 1)
    return pl.pallas_call(
        functools.partial(_band_kernel, n_heads=n_heads),
        grid=(n_heads,),
        in_specs=[pl.BlockSpec(memory_space=pltpu.SMEM),
                  pl.BlockSpec((N_BAND, ATT_KC, ATT_TQ), lambda h: (0, 0, 0))],
        out_specs=pl.BlockSpec((None, N_BAND, ATT_KC, ATT_TQ), lambda h: (h, 0, 0, 0)),
        out_shape=jax.ShapeDtypeStruct((n_heads, N_BAND, ATT_KC, ATT_TQ), F32),
        compiler_params=_cparams(("parallel",)),
        name="bias_band",
    )(table.reshape(-1), bucket)


def _diff_kernel(lam_ref, q_ref, k_ref, vt_ref, band_ref, gain_ref, o_ref, *, out_scale):
    seq = q_ref.shape[0]
    n_chunks = seq // ATT_KC
    lane = lax.broadcasted_iota(jnp.int32, (ATT_TQ, 2 * DIFF_DK), 1)
    scale = DIFF_DK ** -0.5 * LOG2E
    items = [(r, mp) for r in range(seq // ATT_TQ) for mp in range(2)]
    first_map = {}

    def scores(item):
        r, mp = item
        q = (q_ref[r * ATT_TQ:(r + 1) * ATT_TQ, :].astype(F32) * scale).astype(BF16)
        q = jnp.where((lane >= DIFF_DK) == (mp == 1), q, jnp.zeros_like(q))
        tiles = [band_ref[min(max(c - (ATT_TQ // ATT_KC) * r - J_BELOW, 0), N_BAND - 1)] for c in range(n_chunks)]
        return _dot_nt(k_ref[...], q) + jnp.concatenate(tiles, axis=0)

    def finish(item, s_t):
        r, mp = item
        o_t = _softmax_pv_t(s_t, vt_ref[...])
        if mp == 0:
            first_map[r] = o_t
            return
        o = first_map.pop(r) - lam_ref[0] * o_t
        y = o * lax.rsqrt(jnp.mean(o * o, axis=0, keepdims=True) + RMS_EPS) * gain_ref[...]
        o_ref[r] = (y * out_scale).astype(o_ref.dtype)

    _pipelined(items, scores, finish)


def _diff_attention(proj, vt, band, lam, gain_col, lam_init):
    b, s, _ = proj.shape
    h = DIFF_HEADS
    w = 2 * DIFF_DK
    assert w == LANES
    return pl.pallas_call(
        functools.partial(_diff_kernel, out_scale=1.0 - lam_init),
        grid=(h, b),
        in_specs=[pl.BlockSpec(memory_space=pltpu.SMEM),
                  pl.BlockSpec((None, s, w), lambda hh, bb: (bb, 0, COL_Q_D // w + hh)),
                  pl.BlockSpec((None, s, w), lambda hh, bb: (bb, 0, COL_K_D // w + hh)),
                  pl.BlockSpec((None, DIFF_DV, s), lambda hh, bb: (bb, hh, 0)),
                  pl.BlockSpec((None, N_BAND, ATT_KC, ATT_TQ), lambda hh, bb: (hh, 0, 0, 0)),
                  pl.BlockSpec((DIFF_DV, 1), lambda hh, bb: (0, 0))],
        out_specs=pl.BlockSpec((None, s // ATT_TQ, DIFF_DV, ATT_TQ), lambda hh, bb: (bb, 0, hh, 0)),
        out_shape=jax.ShapeDtypeStruct((b, s // ATT_TQ, h * DIFF_DV, ATT_TQ), BF16),
        compiler_params=_cparams(("parallel", "parallel")),
        name="diff_attn",
    )(lam, proj, proj, vt, band, gain_col)


def _residual_ln(x_ref, o_ref, g_ref, b_ref, row_pairs, col_pairs):
    for j in range(x_ref.shape[0] // LN_ROWS):
        rows = slice(j * LN_ROWS, (j + 1) * LN_ROWS)
        tile, lane0 = divmod(j * LN_ROWS, ATT_TQ)
        h = ALPHA * x_ref[rows, :]
        for a_ref, w_ref in row_pairs:
            h = h + _dot(a_ref[rows, :], w_ref[...])
        for a_ref, w_ref in col_pairs:
            h = h + _dot_tn(a_ref[tile, :, lane0:lane0 + LN_ROWS], w_ref[...])
        o_ref[rows, :] = _layer_norm(h, g_ref[...], b_ref[...])


def _tile_specs(n, d, seq, tm):
    per_seq = seq // tm
    rows = lambda width: pl.BlockSpec((tm, width), lambda i: (i, 0))
    cols = lambda k: pl.BlockSpec((None, tm // ATT_TQ, k, ATT_TQ), lambda i: (i // per_seq, i % per_seq, 0, 0))
    full = lambda a: pl.BlockSpec(a.shape, lambda i: (0,) * a.ndim)
    return rows, cols, full


def _outproj_route_kernel(at_ref, w_ref, x_ref, g_ref, b_ref, rw_ref, o_ref, info_ref, info_t_ref, cnt_ref,
                          carry_ref):
    i = pl.program_id(0)
    tm = x_ref.shape[0]

    @pl.when(i == 0)
    def _():
        carry_ref[...] = jnp.zeros_like(carry_ref)

    _residual_ln(x_ref, o_ref, g_ref, b_ref, [], [(at_ref, w_ref)])
    xn = o_ref[...]

    x_hi = xn.astype(BF16)
    x_lo = (xn - x_hi.astype(F32)).astype(BF16)
    logits = _dot(x_hi, rw_ref[0]) + (_dot(x_hi, rw_ref[1]) + _dot(x_lo, rw_ref[0]))
    lane = lax.broadcasted_iota(jnp.int32, logits.shape, 1).astype(F32)
    neg = jnp.float32(-jnp.inf)
    logits = jnp.where(lane < N_EXPERTS, logits, neg)
    v1 = jnp.max(logits, axis=-1, keepdims=True)
    e1 = jnp.min(jnp.where(logits == v1, lane, float(LANES)), axis=-1, keepdims=True)
    rest = jnp.where(lane == e1, neg, logits)
    v2 = jnp.max(rest, axis=-1, keepdims=True)
    e2 = jnp.min(jnp.where(rest == v2, lane, float(LANES)), axis=-1, keepdims=True)
    t = jnp.exp(v2 - v1)
    w1 = 1.0 / (1.0 + t)
    w2 = t / (1.0 + t)

    onehot = jnp.where((lane == e1) | (lane == e2), 1.0, 0.0)
    row = lax.broadcasted_iota(jnp.int32, (tm, tm), 0)
    col = lax.broadcasted_iota(jnp.int32, (tm, tm), 1)
    before = jnp.where(row > col, 1.0, 0.0).astype(BF16)
    prior = carry_ref[...] + _dot(before, onehot.astype(BF16))
    r1 = jnp.sum(jnp.where(lane == e1, prior, 0.0), axis=-1, keepdims=True)
    r2 = jnp.sum(jnp.where(lane == e2, prior, 0.0), axis=-1, keepdims=True)
    carry_ref[...] += jnp.sum(onehot, axis=0, keepdims=True)
    cnt_ref[...] = jnp.broadcast_to(carry_ref[...], cnt_ref.shape)

    info = jnp.zeros(logits.shape, F32)
    for idx, val in enumerate((e1, e2, w1, w2, r1, r2)):
        info = jnp.where(lane == idx, val, info)
    info_ref[...] = info
    info_t_ref[...] = info.T[0:8, :]


def _outproj_route(at, w, x, g, b, router_w2, seq, tm):
    n, d = x.shape
    rows, cols, full = _tile_specs(n, d, seq, tm)
    return pl.pallas_call(
        _outproj_route_kernel,
        grid=(n // tm,),
        in_specs=[cols(at.shape[2]), full(w), rows(d), full(g), full(b), full(router_w2)],
        out_specs=[rows(d), rows(LANES),
                   pl.BlockSpec((8, tm), lambda i: (0, i)),
                   pl.BlockSpec((8, LANES), lambda i: (0, 0))],
        out_shape=[jax.ShapeDtypeStruct((n, d), F32),
                   jax.ShapeDtypeStruct((n, LANES), F32),
                   jax.ShapeDtypeStruct((8, n), F32),
                   jax.ShapeDtypeStruct((8, LANES), F32)],
        scratch_shapes=[pltpu.VMEM((1, LANES), F32)],
        compiler_params=_cparams(("arbitrary",)),
        name="outproj_route",
    )(at, w, x, g, b, router_w2)


def _swiglu_acc(xb, wg_ref, wu_ref, wd_ref, acc_ref):
    n_chunks = wg_ref.shape[-1] // FF_CHUNK
    for c in range(n_chunks):
        cols = slice(c * FF_CHUNK, (c + 1) * FF_CHUNK)
        gate = _dot(xb, wg_ref[:, cols])
        up = _dot(xb, wu_ref[:, cols])
        hidden = (gate * jax.nn.sigmoid(gate) * up).astype(BF16)
        part = _dot(hidden, wd_ref[cols, :])
        if c == 0:
            acc_ref[...] = part
        else:
            acc_ref[...] += part


def _mix_ffn_kernel(a_ref, at_ref, wa_ref, wt_ref, x_ref, g1_ref, b1_ref, wg_ref, wu_ref, wd_ref,
                    g2_ref, b2_ref, o_ref, h_ref):
    _residual_ln(x_ref, h_ref, g1_ref, b1_ref, [(a_ref, wa_ref)], [(at_ref, wt_ref)])
    h = h_ref[...]
    _swiglu_acc(h.astype(BF16), wg_ref, wu_ref, wd_ref, o_ref)
    o_ref[...] = _layer_norm(ALPHA * h + o_ref[...], g2_ref[...], b2_ref[...])


def _mix_ffn(a, at, wa, wt, x, g1, b1, wg, wu, wd, g2, b2, seq, tm):
    n, d = x.shape
    rows, cols, _ = _tile_specs(n, d, seq, tm)
    once = lambda arr: pl.BlockSpec(arr.shape, lambda i: (0,) * arr.ndim, pipeline_mode=pl.Buffered(1))
    return pl.pallas_call(
        _mix_ffn_kernel,
        grid=(n // tm,),
        in_specs=[rows(a.shape[1]), cols(at.shape[2]), once(wa), once(wt), rows(d), once(g1), once(b1),
                  once(wg), once(wu), once(wd), once(g2), once(b2)],
        out_specs=rows(d),
        out_shape=jax.ShapeDtypeStruct((n, d), F32),
        scratch_shapes=[pltpu.VMEM((tm, d), F32)],
        compiler_params=_cparams(("parallel",)),
        name="mix_ffn",
    )(a, at, wa, wt, x, g1, b1, wg, wu, wd, g2, b2)


def _moe_kernel(te_ref, nu_ref, x_ref, wg_ref, wu_ref, wd_ref, o_ref):
    t = pl.program_id(0)

    @pl.when(t < nu_ref[0])
    def _():
        _swiglu_acc(x_ref[...].astype(BF16), wg_ref, wu_ref, wd_ref, o_ref)

    @pl.when(t >= nu_ref[0])
    def _():
        o_ref[...] = jnp.zeros_like(o_ref)


def _moe_ffn(x_sorted, tile_expert, n_used, wg, wu, wd):
    p, d = x_sorted.shape
    f = wg.shape[2]
    tm = MOE_TILE
    grid_spec = pltpu.PrefetchScalarGridSpec(
        num_scalar_prefetch=2,
        grid=(p // tm,),
        in_specs=[pl.BlockSpec((tm, d), lambda t, te, nu: (jnp.minimum(t, nu[0] - 1), 0)),
                  pl.BlockSpec((None, d, f), lambda t, te, nu: (te[t], 0, 0)),
                  pl.BlockSpec((None, d, f), lambda t, te, nu: (te[t], 0, 0)),
                  pl.BlockSpec((None, f, d), lambda t, te, nu: (te[t], 0, 0))],
        out_specs=pl.BlockSpec((tm, d), lambda t, te, nu: (t, 0)),
    )
    return pl.pallas_call(
        _moe_kernel,
        grid_spec=grid_spec,
        out_shape=jax.ShapeDtypeStruct((p, d), F32),
        compiler_params=_cparams(("arbitrary",)),
        name="moe_ffn",
    )(tile_expert, n_used, x_sorted, wg, wu, wd)


def _scatter_kernel(pos_ref, pad_ref, x_ref, o_hbm, zero_ref, sem, zsem):
    i = pl.program_id(0)
    tm = x_ref.shape[0]
    n_tiles = o_hbm.shape[0] // MOE_TILE

    @pl.when(i == 0)
    def _():
        zero_ref[...] = jnp.zeros_like(zero_ref)
        zero_row = zero_ref.at[pl.ds(0, 1)]
        for e in range(N_EXPERTS):
            start, count = pad_ref[e], pad_ref[N_EXPERTS + e]

            def fill(r, carry):
                pltpu.make_async_copy(zero_row, o_hbm.at[pl.ds(start + r, 1)], zsem).start()
                return carry

            lax.fori_loop(0, count, fill, 0)

            def drain(r, carry):
                pltpu.make_async_copy(zero_row, o_hbm.at[pl.ds(start + r, 1)], zsem).wait()
                return carry

            lax.fori_loop(0, count, drain, 0)

        def fill_tile(t, carry):
            dst = o_hbm.at[pl.ds(pl.multiple_of(t * MOE_TILE, MOE_TILE), MOE_TILE)]
            cp = pltpu.make_async_copy(zero_ref, dst, zsem)
            cp.start()
            cp.wait()
            return carry

        lax.fori_loop(pad_ref[2 * N_EXPERTS], n_tiles, fill_tile, 0)

    def issue(r, carry):
        for kk in range(TOP_K):
            dst = pos_ref[kk * tm + r]
            pltpu.make_async_copy(x_ref.at[pl.ds(r, 1)], o_hbm.at[pl.ds(dst, 1)], sem).start()
        return carry

    lax.fori_loop(0, tm, issue, 0, unroll=DMA_UNROLL)

    for kk in range(TOP_K):
        pltpu.make_async_copy(x_ref, o_hbm.at[pl.ds(0, tm)], sem).wait()


def _scatter_rows(x, pos, pad_info, n_slots, tm):
    n, d = x.shape
    return pl.pallas_call(
        _scatter_kernel,
        grid=(n // tm,),
        in_specs=[pl.BlockSpec((TOP_K * tm,), lambda i: (i,), memory_space=pltpu.SMEM),
                  pl.BlockSpec(memory_space=pltpu.SMEM),
                  pl.BlockSpec((tm, d), lambda i: (i, 0))],
        out_specs=pl.BlockSpec(memory_space=pl.ANY),
        out_shape=jax.ShapeDtypeStruct((n_slots, d), F32),
        scratch_shapes=[pltpu.VMEM((MOE_TILE, d), F32),
                        pltpu.SemaphoreType.DMA(()),
                        pltpu.SemaphoreType.DMA(())],
        compiler_params=_cparams(("arbitrary",)),
        name="scatter_rows",
    )(pos, pad_info, x)


def _combine_kernel(pos_ref, next_ref, x_ref, info_ref, g_ref, b_ref, y_hbm, o_ref, buf_ref, sem):
    i = pl.program_id(0)
    tm = x_ref.shape[0]
    slot = i % 2

    def gather(idx_ref, s):
        def issue(r, carry):
            for kk in range(TOP_K):
                src = idx_ref[kk * tm + r]
                pltpu.make_async_copy(y_hbm.at[pl.ds(src, 1)], buf_ref.at[s, kk, pl.ds(r, 1)], sem.at[s]).start()
            return carry
        lax.fori_loop(0, tm, issue, 0, unroll=DMA_UNROLL)

    @pl.when(i == 0)
    def _():
        gather(pos_ref, slot)

    @pl.when(i + 1 < pl.num_programs(0))
    def _():
        gather(next_ref, 1 - slot)

    for kk in range(TOP_K):
        pltpu.make_async_copy(y_hbm.at[pl.ds(0, tm)], buf_ref.at[slot, kk], sem.at[slot]).wait()

    info = info_ref[...]
    w1 = info[:, 2:3]
    w2 = info[:, 3:4]
    f = w1 * buf_ref[slot, 0] + w2 * buf_ref[slot, 1]
    o_ref[...] = _layer_norm(ALPHA * x_ref[...] + f, g_ref[...], b_ref[...])


def _combine_ln(x, info, pos, y_sorted, g, b, tm):
    n, d = x.shape
    last = n // tm - 1
    return pl.pallas_call(
        _combine_kernel,
        grid=(n // tm,),
        in_specs=[pl.BlockSpec((TOP_K * tm,), lambda i: (i,), memory_space=pltpu.SMEM),
                  pl.BlockSpec((TOP_K * tm,), lambda i: (jnp.minimum(i + 1, last),), memory_space=pltpu.SMEM),
                  pl.BlockSpec((tm, d), lambda i: (i, 0)),
                  pl.BlockSpec((tm, LANES), lambda i: (i, 0)),
                  pl.BlockSpec((1, d), lambda i: (0, 0)),
                  pl.BlockSpec((1, d), lambda i: (0, 0)),
                  pl.BlockSpec(memory_space=pl.ANY)],
        out_specs=pl.BlockSpec((tm, d), lambda i: (i, 0)),
        out_shape=jax.ShapeDtypeStruct((n, d), F32),
        scratch_shapes=[pltpu.VMEM((2, TOP_K, tm, d), F32),
                        pltpu.SemaphoreType.DMA((2,))],
        compiler_params=_cparams(("arbitrary",)),
        name="combine_ln",
    )(pos, pos, x, info, g, b, y_sorted)


def _mla_proj_kernel(x_ref, win_ref, qg_ref, kvg_ref, wq1_ref, wq2_ref, wk_ref, wvt_ref,
                     cq_ref, sq_ref, ck_ref, sk_ref, q_ref, k_ref, vt_ref):
    c = _dot(x_ref[...].astype(BF16), win_ref[...])
    cq = c[:, :MLA_Q_LORA]
    ckv = c[:, MLA_Q_LORA:MLA_Q_LORA + MLA_KV_LORA]
    off = MLA_Q_LORA + MLA_KV_LORA
    k_rope = c[:, off:off + LANES] * ck_ref[...] + c[:, off + LANES:off + 2 * LANES] * sk_ref[...]
    cq = (cq * lax.rsqrt(jnp.mean(cq * cq, axis=-1, keepdims=True) + RMS_EPS) * qg_ref[...]).astype(BF16)
    ckv = (ckv * lax.rsqrt(jnp.mean(ckv * ckv, axis=-1, keepdims=True) + RMS_EPS) * kvg_ref[...]).astype(BF16)
    q1 = _dot(cq, wq1_ref[...])
    q2 = _dot(cq, wq2_ref[...])
    k1 = _dot(ckv, wk_ref[...])
    vt_ref[...] = _dot_nt(wvt_ref[...], ckv).astype(vt_ref.dtype)
    cos_q, sin_q = cq_ref[...], sq_ref[...]
    for h in range(MLA_HEADS):
        sl = slice(h * LANES, (h + 1) * LANES)
        q_ref[:, sl] = (q1[:, sl] * cos_q + q2[:, sl] * sin_q).astype(q_ref.dtype)
        k_ref[:, sl] = (k1[:, sl] + k_rope).astype(k_ref.dtype)


def _mla_proj(x, win, qg, kvg, wq1, wq2, wk, wvt, tabs, seq, tm):
    n, d = x.shape
    per_seq = seq // tm
    full = lambda a: pl.BlockSpec(a.shape, lambda i: (0,) * a.ndim)
    tab = pl.BlockSpec((tm, LANES), lambda i: (i % per_seq, 0))
    hw = MLA_HEADS * LANES
    dvt = wvt.shape[0]
    return pl.pallas_call(
        _mla_proj_kernel,
        grid=(n // tm,),
        in_specs=[pl.BlockSpec((tm, d), lambda i: (i, 0)), full(win), full(qg), full(kvg),
                  full(wq1), full(wq2), full(wk), full(wvt), tab, tab, tab, tab],
        out_specs=[pl.BlockSpec((tm, hw), lambda i: (i, 0)),
                   pl.BlockSpec((tm, hw), lambda i: (i, 0)),
                   pl.BlockSpec((None, dvt, tm), lambda i: (i // per_seq, 0, i % per_seq))],
        out_shape=[jax.ShapeDtypeStruct((n, hw), BF16),
                   jax.ShapeDtypeStruct((n, hw), BF16),
                   jax.ShapeDtypeStruct((n // seq, dvt, seq), BF16)],
        compiler_params=_cparams(("parallel",)),
        name="mla_proj",
    )(x, win, qg, kvg, wq1, wq2, wk, wvt, *tabs)


def _softmax_pv_t(s_t, vt):
    p = jnp.exp2(s_t - _col_reduce(s_t, jnp.max))
    l = _col_reduce(p, jnp.sum)
    return _dot(vt, p.astype(BF16)) * (1.0 / l)


def _col_reduce(x, op):
    rows, n = x.shape
    slab = COL_REDUCE_SLAB if rows % COL_REDUCE_SLAB == 0 else rows
    return op(op(x.reshape(rows // slab, slab, n), axis=0), axis=0, keepdims=True)


def _pipelined(items, scores, finish):
    s_next = scores(items[0])
    for idx, item in enumerate(items):
        s = s_next
        if idx + 1 < len(items):
            s_next = scores(items[idx + 1])
        finish(item, s)


def _mla_attn_kernel(q_ref, k_ref, vt_ref, *refs):
    n_side = (len(refs) - 1) // 2
    o_ref = refs[n_side]
    _side_cast(refs[:n_side], refs[n_side + 1:])
    n_sub = q_ref.shape[0] // ATT_TQ
    items = [(r, hh) for r in range(n_sub) for hh in range(2)]
    cols = lambda hh: slice(hh * LANES, (hh + 1) * LANES)

    def scores(item):
        r, hh = item
        return _dot_nt(k_ref[:, cols(hh)], q_ref[r * ATT_TQ:(r + 1) * ATT_TQ, cols(hh)])

    def finish(item, s_t):
        r, hh = item
        vrows = slice(hh * MLA_DV, (hh + 1) * MLA_DV)
        o_ref[r, vrows, :] = _softmax_pv_t(s_t, vt_ref[vrows, :]).astype(o_ref.dtype)

    _pipelined(items, scores, finish)


def _mla_attention(q, k, vt, tq, side_f32):
    b, s, _ = q.shape
    pairs = MLA_HEADS // 2
    pair_dv = 2 * MLA_DV
    n_q = s // tq
    side_specs = _side_specs(side_f32, b * pairs * n_q, lambda bb, p, i: (bb * pairs + p) * n_q + i)
    outs = pl.pallas_call(
        _mla_attn_kernel,
        grid=(b, pairs, n_q),
        in_specs=[pl.BlockSpec((None, tq, 2 * LANES), lambda bb, p, i: (bb, i, p)),
                  pl.BlockSpec((None, s, 2 * LANES), lambda bb, p, i: (bb, 0, p)),
                  pl.BlockSpec((None, pair_dv, s), lambda bb, p, i: (bb, p, 0))] + side_specs,
        out_specs=[pl.BlockSpec((None, tq // ATT_TQ, pair_dv, ATT_TQ), lambda bb, p, i: (bb, i, p, 0))] + side_specs,
        out_shape=[jax.ShapeDtypeStruct((b, s // ATT_TQ, MLA_HEADS * MLA_DV, ATT_TQ), BF16)]
                  + [jax.ShapeDtypeStruct(w.shape, BF16) for w in side_f32],
        compiler_params=_cparams(("parallel", "parallel", "parallel")),
        name="mla_attn",
    )(q, k, vt, *side_f32)
    return outs[0], outs[1:]


def _even_in_weight(w):
    hk, hv = GLA_HEADS * GLA_DK, GLA_HEADS * GLA_DV
    widths = (hk, hk, hv, hv, 2 * GLA_RANK, DIFF_HEADS * 2 * DIFF_DK, DIFF_HEADS * 2 * DIFF_DK,
              DIFF_HEADS * DIFF_DV)
    offs = [0]
    for wd_ in widths:
        offs.append(offs[-1] + wd_)
    piece = lambda j: w[:, offs[j]:offs[j + 1]]
    pad = jnp.zeros((w.shape[0], EVEN_COLS - COL_ALR - 2 * GLA_RANK), w.dtype)
    main = jnp.concatenate([piece(0), piece(1), piece(2), piece(3), piece(5), piece(6), piece(4), pad], axis=1)
    return main.astype(BF16), piece(7).T.astype(BF16)


def _rot_half_cols(w):
    half = MLA_ROPE // 2
    shp = w.shape
    g = w.reshape(shp[0], -1, MLA_ROPE)
    return jnp.concatenate([-g[..., half:], g[..., :half]], axis=-1).reshape(shp)


def _mla_weights(w_in, w_uq, w_ukv):
    d = w_in.shape[0]
    dq = MLA_NOPE + MLA_ROPE
    z = lambda rows, cols: jnp.zeros((rows, cols), F32)
    w_kr = w_in[:, MLA_Q_LORA + MLA_KV_LORA:]
    kr_blk = lambda m: jnp.concatenate([z(d, MLA_NOPE), m, z(d, LANES - dq)], axis=1)
    win = jnp.concatenate([w_in[:, :MLA_Q_LORA + MLA_KV_LORA], kr_blk(w_kr), kr_blk(_rot_half_cols(w_kr))],
                          axis=1).astype(BF16)
    uq = w_uq.reshape(MLA_Q_LORA, MLA_HEADS, dq)
    pad_q = jnp.zeros((MLA_Q_LORA, MLA_HEADS, LANES - dq), F32)
    wq1 = jnp.concatenate([uq, pad_q], axis=-1).reshape(MLA_Q_LORA, -1).astype(BF16)
    rot = _rot_half_cols(uq[..., MLA_NOPE:].reshape(MLA_Q_LORA, -1)).reshape(MLA_Q_LORA, MLA_HEADS, MLA_ROPE)
    wq2 = jnp.concatenate([jnp.zeros((MLA_Q_LORA, MLA_HEADS, MLA_NOPE), F32), rot, pad_q],
                          axis=-1).reshape(MLA_Q_LORA, -1).astype(BF16)
    ukv = w_ukv.reshape(MLA_KV_LORA, MLA_HEADS, MLA_NOPE + MLA_DV)
    wk = jnp.concatenate([ukv[..., :MLA_NOPE], jnp.zeros((MLA_KV_LORA, MLA_HEADS, LANES - MLA_NOPE), F32)],
                         axis=-1).reshape(MLA_KV_LORA, -1).astype(BF16)
    wvt = ukv[..., MLA_NOPE:].reshape(MLA_KV_LORA, -1).T.astype(BF16)
    return win, wq1, wq2, wk, wvt


def _rope_tables(seq):
    half = MLA_ROPE // 2
    inv = ROPE_THETA ** (-jnp.arange(half, dtype=F32) / half)
    ang = jnp.arange(seq, dtype=F32)[:, None] * inv[None, :]
    cos = jnp.concatenate([jnp.cos(ang), jnp.cos(ang)], axis=1)
    sin = jnp.concatenate([jnp.sin(ang), jnp.sin(ang)], axis=1)
    dq = MLA_NOPE + MLA_ROPE
    scale = dq ** -0.5 * LOG2E
    lay = lambda a, fill: jnp.concatenate([jnp.full((seq, MLA_NOPE), fill, F32), a,
                                           jnp.zeros((seq, LANES - dq), F32)], axis=1)
    return lay(cos, 1.0) * scale, lay(sin, 0.0) * scale, lay(cos, 0.0), lay(sin, 0.0)


def _routing_plan(info_t, counts, n_tokens):
    tm = MOE_TILE
    cnt = counts[0, :N_EXPERTS].astype(jnp.int32)
    padded = ((cnt + tm - 1) // tm) * tm
    ends = jnp.cumsum(padded)
    starts = ends - padded
    e = info_t[0:TOP_K].astype(jnp.int32)
    rank = info_t[4:4 + TOP_K].astype(jnp.int32)
    pos = rank + sum(jnp.where(e == k, starts[k], 0) for k in range(N_EXPERTS))
    n_tiles = (n_tokens * TOP_K) // tm + N_EXPERTS
    tile_start = jnp.arange(n_tiles, dtype=jnp.int32) * tm
    tile_expert = jnp.minimum(jnp.sum(tile_start[:, None] >= ends[None, :], axis=1), N_EXPERTS - 1)
    n_used = (ends[-1] // tm).reshape(1)
    pad_info = jnp.concatenate([starts + cnt, padded - cnt, n_used])
    return pos, tile_expert.astype(jnp.int32), n_used.astype(jnp.int32), pad_info.astype(jnp.int32), n_tiles * tm


def kernel(x, rel_bias_table, even_w_in, gla_gate_up, gla_gate_bias, gla_norm_gain, diff_lambda, diff_norm_gain, even_w_out, ffn_w_gate, ffn_w_up, ffn_w_down, odd_w_in, mla_q_norm_gain, mla_kv_norm_gain, mla_w_uq, mla_w_ukv, odd_w_out, router_w, moe_w_gate, moe_w_up, moe_w_down, ln_gain, ln_bias):
    b, s, d = x.shape
    n = b * s
    tm = min(512, s)
    assert s % tm == 0 and tm % ATT_TQ == 0 and s % GLA_GROUP == 0
    x2 = x.reshape(n, d)
    row = lambda v: v.reshape(1, -1)

    w_main, w_vdt = _even_in_weight(even_w_in[0])
    proj, vd_t = _in_proj(x2, w_main, w_vdt, s, tm)
    proj = proj.reshape(b, s, EVEN_COLS)
    gup = jnp.zeros((2, LANES, GLA_HEADS * GLA_DK), F32)
    for dd in range(2):
        gup = gup.at[dd, dd * GLA_RANK:(dd + 1) * GLA_RANK].set(gla_gate_up[0, dd])
    o_gla, (ffn_wg, ffn_wu, ffn_wd, w_out) = _gla(
        proj, gup.astype(BF16), gla_gate_bias[0][:, None, :], row(gla_norm_gain[0]),
        [ffn_w_gate[0], ffn_w_up[0], ffn_w_down[0], even_w_out[0]])
    lam_init = 0.8 - 0.6 * math.exp(-0.3 * 0)
    lf = diff_lambda[0]
    lam = (jnp.exp(jnp.sum(lf[0] * lf[1])) - jnp.exp(jnp.sum(lf[2] * lf[3])) + lam_init).reshape(1)
    o_diff_t = _diff_attention(proj, vd_t, _bias_band(rel_bias_table), lam,
                               diff_norm_gain[0].reshape(-1, 1), lam_init)
    hv = GLA_HEADS * GLA_DV
    x2 = _mix_ffn(o_gla.reshape(n, hv), o_diff_t, w_out[:hv], w_out[hv:], x2,
                  row(ln_gain[0, 0]), row(ln_bias[0, 0]), ffn_wg, ffn_wu, ffn_wd,
                  row(ln_gain[0, 1]), row(ln_bias[0, 1]), s, tm)

    win, wq1, wq2, wk, wvt = _mla_weights(odd_w_in[0], mla_w_uq[0], mla_w_ukv[0])
    q, k, v_t = _mla_proj(x2, win, row(mla_q_norm_gain[0]), row(mla_kv_norm_gain[0]), wq1, wq2, wk, wvt,
                          _rope_tables(s), s, tm)
    moe_w = [moe_w_gate[0], moe_w_up[0], moe_w_down[0]]
    o_t, moe_wb = _mla_attention(q.reshape(b, s, -1), k.reshape(b, s, -1), v_t, min(MLA_TQ, s),
                                 [w.reshape(-1, w.shape[-1]) for w in moe_w])
    wg_b, wu_b, wd_b = [wb.reshape(w.shape) for wb, w in zip(moe_wb, moe_w)]
    rw = jnp.concatenate([router_w[0], jnp.zeros((d, LANES - N_EXPERTS), F32)], axis=1)
    rw_hi = rw.astype(BF16)
    rw2 = jnp.stack([rw_hi, (rw - rw_hi.astype(F32)).astype(BF16)])
    xn, info, info_t, counts = _outproj_route(o_t, odd_w_out[0].astype(BF16), x2,
                                              row(ln_gain[1, 0]), row(ln_bias[1, 0]), rw2, s, tm)
    pos, tile_expert, n_used, pad_info, n_slots = _routing_plan(info_t, counts, n)
    pos_t = pos.reshape(TOP_K, n // tm, tm).transpose(1, 0, 2).reshape(-1)
    x_sorted = _scatter_rows(xn, pos_t, pad_info, n_slots, tm)
    y_sorted = _moe_ffn(x_sorted, tile_expert, n_used, wg_b, wu_b, wd_b)
    out = _combine_ln(xn, info, pos_t, y_sorted, row(ln_gain[1, 1]), row(ln_bias[1, 1]), tm)
    return out.reshape(b, s, d)
```

```python
import functools
import math

import jax
import jax.numpy as jnp
from jax import lax
from jax.experimental import pallas as pl
from jax.experimental.pallas import tpu as pltpu

F32 = jnp.float32
BF16 = jnp.bfloat16

LANES = 128
V7X_VMEM_BYTES = 64 * 1024 * 1024
VMEM_LIMIT = V7X_VMEM_BYTES - 8 * 1024 * 1024

GLA_HEADS, GLA_DK, GLA_DV = 4, 64, 128
GLA_RANK, GLA_TEMP, GLA_CHUNK = 16, 16.0, 64
DIFF_HEADS, DIFF_DK, DIFF_DV = 4, 64, 128
N_BUCKETS, MAX_DISTANCE = 32, 128
MLA_HEADS, MLA_Q_LORA, MLA_KV_LORA = 16, 256, 128
MLA_NOPE, MLA_ROPE, MLA_DV = 64, 32, 64
ROPE_THETA = 10000.0
N_EXPERTS, TOP_K = 8, 2
DEPTH = 2
ALPHA = (2 * DEPTH) ** 0.25
LN_EPS, RMS_EPS = 1e-5, 1e-6

COL_QK_G, COL_V_G, COL_G_G = 0, 512, 1024
COL_Q_D, COL_K_D, COL_ALR = 1536, 2048, 2560
EVEN_COLS = 2688

ATT_KC = 128
ATT_TQ = 512
MLA_TQ = 2048
COL_REDUCE_SLAB = 128
LN_ROWS = 256
PV_KEYS = 512
LOG2E = math.log2(math.e)

GLA_GROUP = 256
FF_CHUNK = 256
MOE_TILE = 512
DMA_UNROLL = 8


def _cparams(sem):
    return pltpu.CompilerParams(dimension_semantics=sem, vmem_limit_bytes=VMEM_LIMIT)


def _layer_norm(y, g, b):
    mu = jnp.mean(y, axis=-1, keepdims=True)
    d = y - mu
    var = jnp.mean(d * d, axis=-1, keepdims=True)
    return d * lax.rsqrt(var + LN_EPS) * g + b


def _split3(x):
    h1 = x.astype(BF16)
    r1 = x - h1.astype(F32)
    h2 = r1.astype(BF16)
    h3 = (r1 - h2.astype(F32)).astype(BF16)
    return h1, h2, h3


def _dot(a, b):
    return jnp.dot(a, b, preferred_element_type=F32)


def _dot_nt(a, b):
    return lax.dot_general(a, b, (((1,), (1,)), ((), ())), preferred_element_type=F32)


def _dot_tn(a, b):
    return lax.dot_general(a, b, (((0,), (0,)), ((), ())), preferred_element_type=F32)


def _in_proj_kernel(x_ref, w_ref, wvt_ref, o_ref, vt_ref):
    xb = x_ref[...].astype(BF16)
    o_ref[...] = _dot(xb, w_ref[...]).astype(o_ref.dtype)
    vt_ref[...] = _dot_nt(wvt_ref[...], xb).astype(vt_ref.dtype)


def _in_proj(x, w, wvt, seq, tm):
    n, k = x.shape
    m = w.shape[1]
    dvt = wvt.shape[0]
    per_seq = seq // tm
    return pl.pallas_call(
        _in_proj_kernel,
        grid=(n // tm,),
        in_specs=[pl.BlockSpec((tm, k), lambda i: (i, 0)),
                  pl.BlockSpec((k, m), lambda i: (0, 0)),
                  pl.BlockSpec((dvt, k), lambda i: (0, 0))],
        out_specs=[pl.BlockSpec((tm, m), lambda i: (i, 0)),
                   pl.BlockSpec((None, dvt, tm), lambda i: (i // per_seq, 0, i % per_seq))],
        out_shape=[jax.ShapeDtypeStruct((n, m), BF16),
                   jax.ShapeDtypeStruct((n // seq, dvt, seq), BF16)],
        compiler_params=_cparams(("parallel",)),
        name="in_proj",
    )(x, w, wvt)


def _side_cast(in_refs, out_refs):
    for w_ref, wb_ref in zip(in_refs, out_refs):
        wb_ref[...] = w_ref[...].astype(wb_ref.dtype)


def _side_specs(side_f32, n_steps, step_of):
    specs = []
    for w in side_f32:
        assert w.shape[0] % (n_steps * 16) == 0, w.shape
        specs.append(pl.BlockSpec((w.shape[0] // n_steps, w.shape[1]), lambda *ids: (step_of(*ids), 0)))
    return specs


def _gla_kernel(qk_ref, v_ref, g_ref, alr_ref, gup_ref, gbias_ref, gain_ref, *refs):
    n_side = (len(refs) - 7) // 2
    _side_cast(refs[:n_side], refs[n_side + 1:2 * n_side + 1])
    o_ref = refs[n_side]
    la_ref, qdec_ref, kdec_ref, decay_ref, acc_ref, st_ref = refs[2 * n_side + 1:]
    seq = qk_ref.shape[0]
    hk = GLA_HEADS * GLA_DK
    hv = GLA_HEADS * GLA_DV
    grp = GLA_GROUP
    c = GLA_CHUNK

    alr = alr_ref[...]
    for d in range(2):
        logits = _dot(alr, gup_ref[d]) + gbias_ref[d]
        log_sig = jnp.minimum(logits, 0.0) - jnp.log(1.0 + jnp.exp(-jnp.abs(logits)))
        la_ref[d] = log_sig / GLA_TEMP

    row = lax.broadcasted_iota(jnp.int32, (grp, grp), 0)
    col = lax.broadcasted_iota(jnp.int32, (grp, grp), 1)
    same = (row // c) == (col // c)
    tri = [same & (row >= col), same & (row <= col)]
    cum_inc = [jnp.where(t, 1.0, 0.0).astype(BF16) for t in tri]
    cum_rem = [jnp.where(same & (row < col), 1.0, 0.0).astype(BF16),
               jnp.where(same & (row > col), 1.0, 0.0).astype(BF16)]
    lane_k = lax.broadcasted_iota(jnp.int32, (grp, hk), 1) // GLA_DK
    scale = GLA_DK ** -0.5

    def group_body(r, carry):
        rows = pl.ds(pl.multiple_of(r * grp, grp), grp)
        q = qk_ref[rows, 0:hk].astype(F32)
        k = qk_ref[rows, hk:2 * hk].astype(F32)
        v = v_ref[rows, :]
        o_heads = [jnp.zeros((grp, GLA_DV), F32) for _ in range(GLA_HEADS)]
        for d in range(2):
            parts = _split3(la_ref[d, rows, :])
            bcum = sum(_dot(cum_inc[d], p) for p in parts)
            brem = sum(_dot(cum_rem[d], p) for p in parts)
            q_dec = q * jnp.exp(bcum) * scale
            k_inv = (k * jnp.exp(-bcum)).astype(BF16)
            qdec_ref[d, rows, :] = q_dec.astype(BF16)
            kdec_ref[d, rows, :] = (k * jnp.exp(brem)).astype(BF16)
            decay_ref[d, rows, :] = jnp.exp(bcum + brem)
            for h in range(GLA_HEADS):
                q_h = jnp.where(lane_k == h, q_dec, 0.0).astype(BF16)
                attn = jnp.where(tri[d], _dot_nt(q_h, k_inv), 0.0)
                o_heads[h] = o_heads[h] + _dot(attn.astype(BF16), v[:, h * GLA_DV:(h + 1) * GLA_DV])
        for h in range(GLA_HEADS):
            acc_ref[rows, h * GLA_DV:(h + 1) * GLA_DV] = o_heads[h]
        return carry

    lax.fori_loop(0, seq // grp, group_body, 0)

    srow = lax.broadcasted_iota(jnp.int32, (hv, hk), 0) // GLA_DV
    scol = lax.broadcasted_iota(jnp.int32, (hv, hk), 1) // GLA_DK
    head_mask = srow == scol
    n_chunks = seq // c
    st_ref[...] = jnp.zeros_like(st_ref)

    def chunk_body(i, carry):
        for d in range(2):
            ci = i if d == 0 else n_chunks - 1 - i
            rows = pl.ds(pl.multiple_of(ci * c, c), c)
            state = st_ref[d]
            acc_ref[rows, :] += _dot_nt(qdec_ref[d, rows, :], state.astype(BF16))
            kv = _dot_tn(v_ref[rows, :], kdec_ref[d, rows, :])
            decay = decay_ref[d, pl.ds(pl.multiple_of(ci * c, c), 1), :]
            st_ref[d] = state * decay + jnp.where(head_mask, kv, 0.0)
        return carry

    lax.fori_loop(0, n_chunks, chunk_body, 0, unroll=2)

    gain = gain_ref[...]
    for h in range(GLA_HEADS):
        sl = slice(h * GLA_DV, (h + 1) * GLA_DV)
        o = acc_ref[:, sl]
        y = o * lax.rsqrt(jnp.mean(o * o, axis=-1, keepdims=True) + RMS_EPS) * gain
        gate = g_ref[:, sl].astype(F32)
        o_ref[:, sl] = (y * (gate * jax.nn.sigmoid(gate))).astype(o_ref.dtype)


def _gla(proj, gup, gbias, gain, side_f32):
    b, s, _ = proj.shape
    hk, hv = GLA_HEADS * GLA_DK, GLA_HEADS * GLA_DV
    blk = lambda width, col: pl.BlockSpec((None, s, width), lambda i: (i, 0, col // width))
    side_specs = _side_specs(side_f32, b, lambda i: i)
    outs = pl.pallas_call(
        _gla_kernel,
        grid=(b,),
        in_specs=[blk(2 * hk, COL_QK_G), blk(hv, COL_V_G), blk(hv, COL_G_G), blk(LANES, COL_ALR),
                  pl.BlockSpec((2, LANES, hk), lambda i: (0, 0, 0)),
                  pl.BlockSpec((2, 1, hk), lambda i: (0, 0, 0)),
                  pl.BlockSpec((1, GLA_DV), lambda i: (0, 0))] + side_specs,
        out_specs=[pl.BlockSpec((None, s, hv), lambda i: (i, 0, 0))] + side_specs,
        out_shape=[jax.ShapeDtypeStruct((b, s, hv), BF16)]
                  + [jax.ShapeDtypeStruct(w.shape, BF16) for w in side_f32],
        scratch_shapes=[pltpu.VMEM((2, s, hk), F32),
                        pltpu.VMEM((2, s, hk), BF16),
                        pltpu.VMEM((2, s, hk), BF16),
                        pltpu.VMEM((2, s, hk), F32),
                        pltpu.VMEM((s, hv), F32),
                        pltpu.VMEM((2, hv, hk), F32)],
        compiler_params=_cparams(("parallel",)),
        name="gla",
    )(proj, proj, proj, proj, gup, gbias, gain, *side_f32)
    return outs[0], outs[1:]


J_BELOW = (-MAX_DISTANCE - ATT_KC + 1) // ATT_KC
J_ABOVE = -(-(MAX_DISTANCE + ATT_TQ - 1) // ATT_KC)
N_BAND = J_ABOVE - J_BELOW + 1
assert ATT_KC * J_BELOW + ATT_KC - 1 <= -MAX_DISTANCE
assert ATT_KC * J_ABOVE - (ATT_TQ - 1) >= MAX_DISTANCE


def _band_buckets():
    half = N_BUCKETS // 2
    max_exact = half // 2
    j = (jnp.arange(N_BAND, dtype=jnp.int32) + J_BELOW)[:, None, None]
    rel = (j * ATT_KC + jnp.arange(ATT_KC, dtype=jnp.int32)[None, :, None]
           - jnp.arange(ATT_TQ, dtype=jnp.int32)[None, None, :])
    bucket = jnp.where(rel > 0, half, 0).astype(jnp.int32)
    n = jnp.abs(rel)
    n_large = max_exact + (jnp.log(jnp.maximum(n, max_exact).astype(F32) / max_exact)
                           / math.log(MAX_DISTANCE / max_exact) * (half - max_exact)).astype(jnp.int32)
    n_large = jnp.minimum(n_large, half - 1)
    return bucket + jnp.where(n < max_exact, n, n_large)


def _band_kernel(tab_ref, bucket_ref, o_ref, *, n_heads):
    h = pl.program_id(0)
    for t in range(N_BAND):
        bucket = bucket_ref[t]
        val = jnp.zeros(bucket.shape, F32)
        for bkt in range(N_BUCKETS):
            val = jnp.where(bucket == bkt, tab_ref[bkt * n_heads + h], val)
        o_ref[t] = val * LOG2E


def _bias_band(table):
    n_heads = table.shape[1]
    bucket = jnp.bitwise_and(_band_buckets(), N_BUCKETS - 1)
    return pl.pallas_call(
        functools.partial(_band_kernel, n_heads=n_heads),
        grid=(n_heads,),
        in_specs=[pl.BlockSpec(memory_space=pltpu.SMEM),
                  pl.BlockSpec((N_BAND, ATT_KC, ATT_TQ), lambda h: (0, 0, 0))],
        out_specs=pl.BlockSpec((None, N_BAND, ATT_KC, ATT_TQ), lambda h: (h, 0, 0, 0)),
        out_shape=jax.ShapeDtypeStruct((n_heads, N_BAND, ATT_KC, ATT_TQ), F32),
        compiler_params=_cparams(("parallel",)),
        name="bias_band",
    )(table.reshape(-1), bucket)


def _diff_kernel(lam_ref, q_ref, k_ref, vt_ref, band_ref, gain_ref, o_ref, *, out_scale):
    seq = q_ref.shape[0]
    n_chunks = seq // ATT_KC
    lane = lax.broadcasted_iota(jnp.int32, (ATT_TQ, 2 * DIFF_DK), 1)
    scale = DIFF_DK ** -0.5 * LOG2E
    items = [(r, mp) for r in range(seq // ATT_TQ) for mp in range(2)]
    first_map = {}

    def scores(item):
        r, mp = item
        q = (q_ref[r * ATT_TQ:(r + 1) * ATT_TQ, :].astype(F32) * scale).astype(BF16)
        q = jnp.where((lane >= DIFF_DK) == (mp == 1), q, jnp.zeros_like(q))
        tiles = [band_ref[min(max(c - (ATT_TQ // ATT_KC) * r - J_BELOW, 0), N_BAND - 1)] for c in range(n_chunks)]
        return _dot_nt(k_ref[...], q) + jnp.concatenate(tiles, axis=0)

    def finish(item, s_t):
        r, mp = item
        o_t = _softmax_pv_t(s_t, vt_ref[...])
        if mp == 0:
            first_map[r] = o_t
            return
        o = first_map.pop(r) - lam_ref[0] * o_t
        y = o * lax.rsqrt(jnp.mean(o * o, axis=0, keepdims=True) + RMS_EPS) * gain_ref[...]
        o_ref[r] = (y * out_scale).astype(o_ref.dtype)

    _pipelined(items, scores, finish)


def _diff_attention(proj, vt, band, lam, gain_col, lam_init):
    b, s, _ = proj.shape
    h = DIFF_HEADS
    w = 2 * DIFF_DK
    assert w == LANES
    return pl.pallas_call(
        functools.partial(_diff_kernel, out_scale=1.0 - lam_init),
        grid=(h, b),
        in_specs=[pl.BlockSpec(memory_space=pltpu.SMEM),
                  pl.BlockSpec((None, s, w), lambda hh, bb: (bb, 0, COL_Q_D // w + hh)),
                  pl.BlockSpec((None, s, w), lambda hh, bb: (bb, 0, COL_K_D // w + hh)),
                  pl.BlockSpec((None, DIFF_DV, s), lambda hh, bb: (bb, hh, 0)),
                  pl.BlockSpec((None, N_BAND, ATT_KC, ATT_TQ), lambda hh, bb: (hh, 0, 0, 0)),
                  pl.BlockSpec((DIFF_DV, 1), lambda hh, bb: (0, 0))],
        out_specs=pl.BlockSpec((None, s // ATT_TQ, DIFF_DV, ATT_TQ), lambda hh, bb: (bb, 0, hh, 0)),
        out_shape=jax.ShapeDtypeStruct((b, s // ATT_TQ, h * DIFF_DV, ATT_TQ), BF16),
        compiler_params=_cparams(("parallel", "parallel")),
        name="diff_attn",
    )(lam, proj, proj, vt, band, gain_col)


def _residual_ln(x_ref, o_ref, g_ref, b_ref, row_pairs, col_pairs):
    for j in range(x_ref.shape[0] // LN_ROWS):
        rows = slice(j * LN_ROWS, (j + 1) * LN_ROWS)
        tile, lane0 = divmod(j * LN_ROWS, ATT_TQ)
        h = ALPHA * x_ref[rows, :]
        for a_ref, w_ref in row_pairs:
            h = h + _dot(a_ref[rows, :], w_ref[...])
        for a_ref, w_ref in col_pairs:
            h = h + _dot_tn(a_ref[tile, :, lane0:lane0 + LN_ROWS], w_ref[...])
        o_ref[rows, :] = _layer_norm(h, g_ref[...], b_ref[...])


def _tile_specs(n, d, seq, tm):
    per_seq = seq // tm
    rows = lambda width: pl.BlockSpec((tm, width), lambda i: (i, 0))
    cols = lambda k: pl.BlockSpec((None, tm // ATT_TQ, k, ATT_TQ), lambda i: (i // per_seq, i % per_seq, 0, 0))
    full = lambda a: pl.BlockSpec(a.shape, lambda i: (0,) * a.ndim)
    return rows, cols, full


def _outproj_route_kernel(at_ref, w_ref, x_ref, g_ref, b_ref, rw_ref, o_ref, info_ref, info_t_ref, cnt_ref,
                          carry_ref):
    i = pl.program_id(0)
    tm = x_ref.shape[0]

    @pl.when(i == 0)
    def _():
        carry_ref[...] = jnp.zeros_like(carry_ref)

    _residual_ln(x_ref, o_ref, g_ref, b_ref, [], [(at_ref, w_ref)])
    xn = o_ref[...]

    x_hi = xn.astype(BF16)
    x_lo = (xn - x_hi.astype(F32)).astype(BF16)
    logits = _dot(x_hi, rw_ref[0]) + (_dot(x_hi, rw_ref[1]) + _dot(x_lo, rw_ref[0]))
    lane = lax.broadcasted_iota(jnp.int32, logits.shape, 1).astype(F32)
    neg = jnp.float32(-jnp.inf)
    logits = jnp.where(lane < N_EXPERTS, logits, neg)
    v1 = jnp.max(logits, axis=-1, keepdims=True)
    e1 = jnp.min(jnp.where(logits == v1, lane, float(LANES)), axis=-1, keepdims=True)
    rest = jnp.where(lane == e1, neg, logits)
    v2 = jnp.max(rest, axis=-1, keepdims=True)
    e2 = jnp.min(jnp.where(rest == v2, lane, float(LANES)), axis=-1, keepdims=True)
    t = jnp.exp(v2 - v1)
    w1 = 1.0 / (1.0 + t)
    w2 = t / (1.0 + t)

    onehot = jnp.where((lane == e1) | (lane == e2), 1.0, 0.0)
    row = lax.broadcasted_iota(jnp.int32, (tm, tm), 0)
    col = lax.broadcasted_iota(jnp.int32, (tm, tm), 1)
    before = jnp.where(row > col, 1.0, 0.0).astype(BF16)
    prior = carry_ref[...] + _dot(before, onehot.astype(BF16))
    r1 = jnp.sum(jnp.where(lane == e1, prior, 0.0), axis=-1, keepdims=True)
    r2 = jnp.sum(jnp.where(lane == e2, prior, 0.0), axis=-1, keepdims=True)
    carry_ref[...] += jnp.sum(onehot, axis=0, keepdims=True)
    cnt_ref[...] = jnp.broadcast_to(carry_ref[...], cnt_ref.shape)

    info = jnp.zeros(logits.shape, F32)
    for idx, val in enumerate((e1, e2, w1, w2, r1, r2)):
        info = jnp.where(lane == idx, val, info)
    info_ref[...] = info
    info_t_ref[...] = info.T[0:8, :]


def _outproj_route(at, w, x, g, b, router_w2, seq, tm):
    n, d = x.shape
    rows, cols, full = _tile_specs(n, d, seq, tm)
    return pl.pallas_call(
        _outproj_route_kernel,
        grid=(n // tm,),
        in_specs=[cols(at.shape[2]), full(w), rows(d), full(g), full(b), full(router_w2)],
        out_specs=[rows(d), rows(LANES),
                   pl.BlockSpec((8, tm), lambda i: (0, i)),
                   pl.BlockSpec((8, LANES), lambda i: (0, 0))],
        out_shape=[jax.ShapeDtypeStruct((n, d), F32),
                   jax.ShapeDtypeStruct((n, LANES), F32),
                   jax.ShapeDtypeStruct((8, n), F32),
                   jax.ShapeDtypeStruct((8, LANES), F32)],
        scratch_shapes=[pltpu.VMEM((1, LANES), F32)],
        compiler_params=_cparams(("arbitrary",)),
        name="outproj_route",
    )(at, w, x, g, b, router_w2)


def _swiglu_acc(xb, wg_ref, wu_ref, wd_ref, acc_ref):
    n_chunks = wg_ref.shape[-1] // FF_CHUNK
    for c in range(n_chunks):
        cols = slice(c * FF_CHUNK, (c + 1) * FF_CHUNK)
        gate = _dot(xb, wg_ref[:, cols])
        up = _dot(xb, wu_ref[:, cols])
        hidden = (gate * jax.nn.sigmoid(gate) * up).astype(BF16)
        part = _dot(hidden, wd_ref[cols, :])
        if c == 0:
            acc_ref[...] = part
        else:
            acc_ref[...] += part


def _mix_ffn_kernel(a_ref, at_ref, wa_ref, wt_ref, x_ref, g1_ref, b1_ref, wg_ref, wu_ref, wd_ref,
                    g2_ref, b2_ref, o_ref, h_ref):
    _residual_ln(x_ref, h_ref, g1_ref, b1_ref, [(a_ref, wa_ref)], [(at_ref, wt_ref)])
    h = h_ref[...]
    _swiglu_acc(h.astype(BF16), wg_ref, wu_ref, wd_ref, o_ref)
    o_ref[...] = _layer_norm(ALPHA * h + o_ref[...], g2_ref[...], b2_ref[...])


def _mix_ffn(a, at, wa, wt, x, g1, b1, wg, wu, wd, g2, b2, seq, tm):
    n, d = x.shape
    rows, cols, _ = _tile_specs(n, d, seq, tm)
    once = lambda arr: pl.BlockSpec(arr.shape, lambda i: (0,) * arr.ndim, pipeline_mode=pl.Buffered(1))
    return pl.pallas_call(
        _mix_ffn_kernel,
        grid=(n // tm,),
        in_specs=[rows(a.shape[1]), cols(at.shape[2]), once(wa), once(wt), rows(d), once(g1), once(b1),
                  once(wg), once(wu), once(wd), once(g2), once(b2)],
        out_specs=rows(d),
        out_shape=jax.ShapeDtypeStruct((n, d), F32),
        scratch_shapes=[pltpu.VMEM((tm, d), F32)],
        compiler_params=_cparams(("parallel",)),
        name="mix_ffn",
    )(a, at, wa, wt, x, g1, b1, wg, wu, wd, g2, b2)


def _moe_kernel(te_ref, nu_ref, x_ref, wg_ref, wu_ref, wd_ref, o_ref):
    t = pl.program_id(0)

    @pl.when(t < nu_ref[0])
    def _():
        _swiglu_acc(x_ref[...].astype(BF16), wg_ref, wu_ref, wd_ref, o_ref)

    @pl.when(t >= nu_ref[0])
    def _():
        o_ref[...] = jnp.zeros_like(o_ref)


def _moe_ffn(x_sorted, tile_expert, n_used, wg, wu, wd):
    p, d = x_sorted.shape
    f = wg.shape[2]
    tm = MOE_TILE
    grid_spec = pltpu.PrefetchScalarGridSpec(
        num_scalar_prefetch=2,
        grid=(p // tm,),
        in_specs=[pl.BlockSpec((tm, d), lambda t, te, nu: (jnp.minimum(t, nu[0] - 1), 0)),
                  pl.BlockSpec((None, d, f), lambda t, te, nu: (te[t], 0, 0)),
                  pl.BlockSpec((None, d, f), lambda t, te, nu: (te[t], 0, 0)),
                  pl.BlockSpec((None, f, d), lambda t, te, nu: (te[t], 0, 0))],
        out_specs=pl.BlockSpec((tm, d), lambda t, te, nu: (t, 0)),
    )
    return pl.pallas_call(
        _moe_kernel,
        grid_spec=grid_spec,
        out_shape=jax.ShapeDtypeStruct((p, d), F32),
        compiler_params=_cparams(("arbitrary",)),
        name="moe_ffn",
    )(tile_expert, n_used, x_sorted, wg, wu, wd)


def _scatter_kernel(pos_ref, pad_ref, x_ref, o_hbm, zero_ref, sem, zsem):
    i = pl.program_id(0)
    tm = x_ref.shape[0]
    n_tiles = o_hbm.shape[0] // MOE_TILE

    @pl.when(i == 0)
    def _():
        zero_ref[...] = jnp.zeros_like(zero_ref)
        zero_row = zero_ref.at[pl.ds(0, 1)]
        for e in range(N_EXPERTS):
            start, count = pad_ref[e], pad_ref[N_EXPERTS + e]

            def fill(r, carry):
                pltpu.make_async_copy(zero_row, o_hbm.at[pl.ds(start + r, 1)], zsem).start()
                return carry

            lax.fori_loop(0, count, fill, 0)

            def drain(r, carry):
                pltpu.make_async_copy(zero_row, o_hbm.at[pl.ds(start + r, 1)], zsem).wait()
                return carry

            lax.fori_loop(0, count, drain, 0)

        def fill_tile(t, carry):
            dst = o_hbm.at[pl.ds(pl.multiple_of(t * MOE_TILE, MOE_TILE), MOE_TILE)]
            cp = pltpu.make_async_copy(zero_ref, dst, zsem)
            cp.start()
            cp.wait()
            return carry

        lax.fori_loop(pad_ref[2 * N_EXPERTS], n_tiles, fill_tile, 0)

    def issue(r, carry):
        for kk in range(TOP_K):
            dst = pos_ref[kk * tm + r]
            pltpu.make_async_copy(x_ref.at[pl.ds(r, 1)], o_hbm.at[pl.ds(dst, 1)], sem).start()
        return carry

    lax.fori_loop(0, tm, issue, 0, unroll=DMA_UNROLL)

    for kk in range(TOP_K):
        pltpu.make_async_copy(x_ref, o_hbm.at[pl.ds(0, tm)], sem).wait()


def _scatter_rows(x, pos, pad_info, n_slots, tm):
    n, d = x.shape
    return pl.pallas_call(
        _scatter_kernel,
        grid=(n // tm,),
        in_specs=[pl.BlockSpec((TOP_K * tm,), lambda i: (i,), memory_space=pltpu.SMEM),
                  pl.BlockSpec(memory_space=pltpu.SMEM),
                  pl.BlockSpec((tm, d), lambda i: (i, 0))],
        out_specs=pl.BlockSpec(memory_space=pl.ANY),
        out_shape=jax.ShapeDtypeStruct((n_slots, d), F32),
        scratch_shapes=[pltpu.VMEM((MOE_TILE, d), F32),
                        pltpu.SemaphoreType.DMA(()),
                        pltpu.SemaphoreType.DMA(())],
        compiler_params=_cparams(("arbitrary",)),
        name="scatter_rows",
    )(pos, pad_info, x)


def _combine_kernel(pos_ref, next_ref, x_ref, info_ref, g_ref, b_ref, y_hbm, o_ref, buf_ref, sem):
    i = pl.program_id(0)
    tm = x_ref.shape[0]
    slot = i % 2

    def gather(idx_ref, s):
        def issue(r, carry):
            for kk in range(TOP_K):
                src = idx_ref[kk * tm + r]
                pltpu.make_async_copy(y_hbm.at[pl.ds(src, 1)], buf_ref.at[s, kk, pl.ds(r, 1)], sem.at[s]).start()
            return carry
        lax.fori_loop(0, tm, issue, 0, unroll=DMA_UNROLL)

    @pl.when(i == 0)
    def _():
        gather(pos_ref, slot)

    @pl.when(i + 1 < pl.num_programs(0))
    def _():
        gather(next_ref, 1 - slot)

    for kk in range(TOP_K):
        pltpu.make_async_copy(y_hbm.at[pl.ds(0, tm)], buf_ref.at[slot, kk], sem.at[slot]).wait()

    info = info_ref[...]
    w1 = info[:, 2:3]
    w2 = info[:, 3:4]
    f = w1 * buf_ref[slot, 0] + w2 * buf_ref[slot, 1]
    o_ref[...] = _layer_norm(ALPHA * x_ref[...] + f, g_ref[...], b_ref[...])


def _combine_ln(x, info, pos, y_sorted, g, b, tm):
    n, d = x.shape
    last = n // tm - 1
    return pl.pallas_call(
        _combine_kernel,
        grid=(n // tm,),
        in_specs=[pl.BlockSpec((TOP_K * tm,), lambda i: (i,), memory_space=pltpu.SMEM),
                  pl.BlockSpec((TOP_K * tm,), lambda i: (jnp.minimum(i + 1, last),), memory_space=pltpu.SMEM),
                  pl.BlockSpec((tm, d), lambda i: (i, 0)),
                  pl.BlockSpec((tm, LANES), lambda i: (i, 0)),
                  pl.BlockSpec((1, d), lambda i: (0, 0)),
                  pl.BlockSpec((1, d), lambda i: (0, 0)),
                  pl.BlockSpec(memory_space=pl.ANY)],
        out_specs=pl.BlockSpec((tm, d), lambda i: (i, 0)),
        out_shape=jax.ShapeDtypeStruct((n, d), F32),
        scratch_shapes=[pltpu.VMEM((2, TOP_K, tm, d), F32),
                        pltpu.SemaphoreType.DMA((2,))],
        compiler_params=_cparams(("arbitrary",)),
        name="combine_ln",
    )(pos, pos, x, info, g, b, y_sorted)


def _mla_proj_kernel(x_ref, win_ref, qg_ref, kvg_ref, wq1_ref, wq2_ref, wk_ref, wvt_ref,
                     cq_ref, sq_ref, ck_ref, sk_ref, q_ref, k_ref, vt_ref):
    c = _dot(x_ref[...].astype(BF16), win_ref[...])
    cq = c[:, :MLA_Q_LORA]
    ckv = c[:, MLA_Q_LORA:MLA_Q_LORA + MLA_KV_LORA]
    off = MLA_Q_LORA + MLA_KV_LORA
    k_rope = c[:, off:off + LANES] * ck_ref[...] + c[:, off + LANES:off + 2 * LANES] * sk_ref[...]
    cq = (cq * lax.rsqrt(jnp.mean(cq * cq, axis=-1, keepdims=True) + RMS_EPS) * qg_ref[...]).astype(BF16)
    ckv = (ckv * lax.rsqrt(jnp.mean(ckv * ckv, axis=-1, keepdims=True) + RMS_EPS) * kvg_ref[...]).astype(BF16)
    q1 = _dot(cq, wq1_ref[...])
    q2 = _dot(cq, wq2_ref[...])
    k1 = _dot(ckv, wk_ref[...])
    vt_ref[...] = _dot_nt(wvt_ref[...], ckv).astype(vt_ref.dtype)
    cos_q, sin_q = cq_ref[...], sq_ref[...]
    for h in range(MLA_HEADS):
        sl = slice(h * LANES, (h + 1) * LANES)
        q_ref[:, sl] = (q1[:, sl] * cos_q + q2[:, sl] * sin_q).astype(q_ref.dtype)
        k_ref[:, sl] = (k1[:, sl] + k_rope).astype(k_ref.dtype)


def _mla_proj(x, win, qg, kvg, wq1, wq2, wk, wvt, tabs, seq, tm):
    n, d = x.shape
    per_seq = seq // tm
    full = lambda a: pl.BlockSpec(a.shape, lambda i: (0,) * a.ndim)
    tab = pl.BlockSpec((tm, LANES), lambda i: (i % per_seq, 0))
    hw = MLA_HEADS * LANES
    dvt = wvt.shape[0]
    return pl.pallas_call(
        _mla_proj_kernel,
        grid=(n // tm,),
        in_specs=[pl.BlockSpec((tm, d), lambda i: (i, 0)), full(win), full(qg), full(kvg),
                  full(wq1), full(wq2), full(wk), full(wvt), tab, tab, tab, tab],
        out_specs=[pl.BlockSpec((tm, hw), lambda i: (i, 0)),
                   pl.BlockSpec((tm, hw), lambda i: (i, 0)),
                   pl.BlockSpec((None, dvt, tm), lambda i: (i // per_seq, 0, i % per_seq))],
        out_shape=[jax.ShapeDtypeStruct((n, hw), BF16),
                   jax.ShapeDtypeStruct((n, hw), BF16),
                   jax.ShapeDtypeStruct((n // seq, dvt, seq), BF16)],
        compiler_params=_cparams(("parallel",)),
        name="mla_proj",
    )(x, win, qg, kvg, wq1, wq2, wk, wvt, *tabs)


def _softmax_pv_t(s_t, vt):
    m = _col_reduce(s_t, jnp.max)
    n_keys = s_t.shape[0]
    blk = PV_KEYS if n_keys % PV_KEYS == 0 else n_keys
    l = jnp.zeros_like(m)
    acc = jnp.zeros((vt.shape[0], s_t.shape[1]), F32)
    for c in range(n_keys // blk):
        keys = slice(c * blk, (c + 1) * blk)
        p = jnp.exp2(s_t[keys, :] - m)
        l = l + _col_reduce(p, jnp.sum)
        acc = acc + _dot(vt[:, keys], p.astype(BF16))
    return acc * (1.0 / l)


def _col_reduce(x, op):
    rows, n = x.shape
    slab = COL_REDUCE_SLAB if rows % COL_REDUCE_SLAB == 0 else rows
    return op(op(x.reshape(rows // slab, slab, n), axis=0), axis=0, keepdims=True)


def _pipelined(items, scores, finish):
    s_next = scores(items[0])
    for idx, item in enumerate(items):
        s = s_next
        if idx + 1 < len(items):
            s_next = scores(items[idx + 1])
        finish(item, s)


def _mla_attn_kernel(q_ref, k_ref, vt_ref, *refs):
    n_side = (len(refs) - 1) // 2
    o_ref = refs[n_side]
    _side_cast(refs[:n_side], refs[n_side + 1:])
    n_sub = q_ref.shape[0] // ATT_TQ
    items = [(r, hh) for r in range(n_sub) for hh in range(2)]
    cols = lambda hh: slice(hh * LANES, (hh + 1) * LANES)

    def scores(item):
        r, hh = item
        return _dot_nt(k_ref[:, cols(hh)], q_ref[r * ATT_TQ:(r + 1) * ATT_TQ, cols(hh)])

    def finish(item, s_t):
        r, hh = item
        vrows = slice(hh * MLA_DV, (hh + 1) * MLA_DV)
        o_ref[r, vrows, :] = _softmax_pv_t(s_t, vt_ref[vrows, :]).astype(o_ref.dtype)

    _pipelined(items, scores, finish)


def _mla_attention(q, k, vt, tq, side_f32):
    b, s, _ = q.shape
    pairs = MLA_HEADS // 2
    pair_dv = 2 * MLA_DV
    n_q = s // tq
    side_specs = _side_specs(side_f32, b * pairs * n_q, lambda bb, p, i: (bb * pairs + p) * n_q + i)
    outs = pl.pallas_call(
        _mla_attn_kernel,
        grid=(b, pairs, n_q),
        in_specs=[pl.BlockSpec((None, tq, 2 * LANES), lambda bb, p, i: (bb, i, p)),
                  pl.BlockSpec((None, s, 2 * LANES), lambda bb, p, i: (bb, 0, p)),
                  pl.BlockSpec((None, pair_dv, s), lambda bb, p, i: (bb, p, 0))] + side_specs,
        out_specs=[pl.BlockSpec((None, tq // ATT_TQ, pair_dv, ATT_TQ), lambda bb, p, i: (bb, i, p, 0))] + side_specs,
        out_shape=[jax.ShapeDtypeStruct((b, s // ATT_TQ, MLA_HEADS * MLA_DV, ATT_TQ), BF16)]
                  + [jax.ShapeDtypeStruct(w.shape, BF16) for w in side_f32],
        compiler_params=_cparams(("parallel", "parallel", "parallel")),
        name="mla_attn",
    )(q, k, vt, *side_f32)
    return outs[0], outs[1:]


def _even_in_weight(w):
    hk, hv = GLA_HEADS * GLA_DK, GLA_HEADS * GLA_DV
    widths = (hk, hk, hv, hv, 2 * GLA_RANK, DIFF_HEADS * 2 * DIFF_DK, DIFF_HEADS * 2 * DIFF_DK,
              DIFF_HEADS * DIFF_DV)
    offs = [0]
    for wd_ in widths:
        offs.append(offs[-1] + wd_)
    piece = lambda j: w[:, offs[j]:offs[j + 1]]
    pad = jnp.zeros((w.shape[0], EVEN_COLS - COL_ALR - 2 * GLA_RANK), w.dtype)
    main = jnp.concatenate([piece(0), piece(1), piece(2), piece(3), piece(5), piece(6), piece(4), pad], axis=1)
    return main.astype(BF16), piece(7).T.astype(BF16)


def _rot_half_cols(w):
    half = MLA_ROPE // 2
    shp = w.shape
    g = w.reshape(shp[0], -1, MLA_ROPE)
    return jnp.concatenate([-g[..., half:], g[..., :half]], axis=-1).reshape(shp)


def _mla_weights(w_in, w_uq, w_ukv):
    d = w_in.shape[0]
    dq = MLA_NOPE + MLA_ROPE
    z = lambda rows, cols: jnp.zeros((rows, cols), F32)
    w_kr = w_in[:, MLA_Q_LORA + MLA_KV_LORA:]
    kr_blk = lambda m: jnp.concatenate([z(d, MLA_NOPE), m, z(d, LANES - dq)], axis=1)
    win = jnp.concatenate([w_in[:, :MLA_Q_LORA + MLA_KV_LORA], kr_blk(w_kr), kr_blk(_rot_half_cols(w_kr))],
                          axis=1).astype(BF16)
    uq = w_uq.reshape(MLA_Q_LORA, MLA_HEADS, dq)
    pad_q = jnp.zeros((MLA_Q_LORA, MLA_HEADS, LANES - dq), F32)
    wq1 = jnp.concatenate([uq, pad_q], axis=-1).reshape(MLA_Q_LORA, -1).astype(BF16)
    rot = _rot_half_cols(uq[..., MLA_NOPE:].reshape(MLA_Q_LORA, -1)).reshape(MLA_Q_LORA, MLA_HEADS, MLA_ROPE)
    wq2 = jnp.concatenate([jnp.zeros((MLA_Q_LORA, MLA_HEADS, MLA_NOPE), F32), rot, pad_q],
                          axis=-1).reshape(MLA_Q_LORA, -1).astype(BF16)
    ukv = w_ukv.reshape(MLA_KV_LORA, MLA_HEADS, MLA_NOPE + MLA_DV)
    wk = jnp.concatenate([ukv[..., :MLA_NOPE], jnp.zeros((MLA_KV_LORA, MLA_HEADS, LANES - MLA_NOPE), F32)],
                         axis=-1).reshape(MLA_KV_LORA, -1).astype(BF16)
    wvt = ukv[..., MLA_NOPE:].reshape(MLA_KV_LORA, -1).T.astype(BF16)
    return win, wq1, wq2, wk, wvt


def _rope_tables(seq):
    half = MLA_ROPE // 2
    inv = ROPE_THETA ** (-jnp.arange(half, dtype=F32) / half)
    ang = jnp.arange(seq, dtype=F32)[:, None] * inv[None, :]
    cos = jnp.concatenate([jnp.cos(ang), jnp.cos(ang)], axis=1)
    sin = jnp.concatenate([jnp.sin(ang), jnp.sin(ang)], axis=1)
    dq = MLA_NOPE + MLA_ROPE
    scale = dq ** -0.5 * LOG2E
    lay = lambda a, fill: jnp.concatenate([jnp.full((seq, MLA_NOPE), fill, F32), a,
                                           jnp.zeros((seq, LANES - dq), F32)], axis=1)
    return lay(cos, 1.0) * scale, lay(sin, 0.0) * scale, lay(cos, 0.0), lay(sin, 0.0)


def _routing_plan(info_t, counts, n_tokens):
    tm = MOE_TILE
    cnt = counts[0, :N_EXPERTS].astype(jnp.int32)
    padded = ((cnt + tm - 1) // tm) * tm
    ends = jnp.cumsum(padded)
    starts = ends - padded
    e = info_t[0:TOP_K].astype(jnp.int32)
    rank = info_t[4:4 + TOP_K].astype(jnp.int32)
    pos = rank + sum(jnp.where(e == k, starts[k], 0) for k in range(N_EXPERTS))
    n_tiles = (n_tokens * TOP_K) // tm + N_EXPERTS
    tile_start = jnp.arange(n_tiles, dtype=jnp.int32) * tm
    tile_expert = jnp.minimum(jnp.sum(tile_start[:, None] >= ends[None, :], axis=1), N_EXPERTS - 1)
    n_used = (ends[-1] // tm).reshape(1)
    pad_info = jnp.concatenate([starts + cnt, padded - cnt, n_used])
    return pos, tile_expert.astype(jnp.int32), n_used.astype(jnp.int32), pad_info.astype(jnp.int32), n_tiles * tm


def kernel(x, rel_bias_table, even_w_in, gla_gate_up, gla_gate_bias, gla_norm_gain, diff_lambda, diff_norm_gain, even_w_out, ffn_w_gate, ffn_w_up, ffn_w_down, odd_w_in, mla_q_norm_gain, mla_kv_norm_gain, mla_w_uq, mla_w_ukv, odd_w_out, router_w, moe_w_gate, moe_w_up, moe_w_down, ln_gain, ln_bias):
    b, s, d = x.shape
    n = b * s
    tm = min(512, s)
    assert s % tm == 0 and tm % ATT_TQ == 0 and s % GLA_GROUP == 0
    x2 = x.reshape(n, d)
    row = lambda v: v.reshape(1, -1)

    w_main, w_vdt = _even_in_weight(even_w_in[0])
    proj, vd_t = _in_proj(x2, w_main, w_vdt, s, tm)
    proj = proj.reshape(b, s, EVEN_COLS)
    gup = jnp.zeros((2, LANES, GLA_HEADS * GLA_DK), F32)
    for dd in range(2):
        gup = gup.at[dd, dd * GLA_RANK:(dd + 1) * GLA_RANK].set(gla_gate_up[0, dd])
    o_gla, (ffn_wg, ffn_wu, ffn_wd, w_out) = _gla(
        proj, gup.astype(BF16), gla_gate_bias[0][:, None, :], row(gla_norm_gain[0]),
        [ffn_w_gate[0], ffn_w_up[0], ffn_w_down[0], even_w_out[0]])
    lam_init = 0.8 - 0.6 * math.exp(-0.3 * 0)
    lf = diff_lambda[0]
    lam = (jnp.exp(jnp.sum(lf[0] * lf[1])) - jnp.exp(jnp.sum(lf[2] * lf[3])) + lam_init).reshape(1)
    o_diff_t = _diff_attention(proj, vd_t, _bias_band(rel_bias_table), lam,
                               diff_norm_gain[0].reshape(-1, 1), lam_init)
    hv = GLA_HEADS * GLA_DV
    x2 = _mix_ffn(o_gla.reshape(n, hv), o_diff_t, w_out[:hv], w_out[hv:], x2,
                  row(ln_gain[0, 0]), row(ln_bias[0, 0]), ffn_wg, ffn_wu, ffn_wd,
                  row(ln_gain[0, 1]), row(ln_bias[0, 1]), s, tm)

    win, wq1, wq2, wk, wvt = _mla_weights(odd_w_in[0], mla_w_uq[0], mla_w_ukv[0])
    q, k, v_t = _mla_proj(x2, win, row(mla_q_norm_gain[0]), row(mla_kv_norm_gain[0]), wq1, wq2, wk, wvt,
                          _rope_tables(s), s, tm)
    moe_w = [moe_w_gate[0], moe_w_up[0], moe_w_down[0]]
    o_t, moe_wb = _mla_attention(q.reshape(b, s, -1), k.reshape(b, s, -1), v_t, min(MLA_TQ, s),
                                 [w.reshape(-1, w.shape[-1]) for w in moe_w])
    wg_b, wu_b, wd_b = [wb.reshape(w.shape) for wb, w in zip(moe_wb, moe_w)]
    rw = jnp.concatenate([router_w[0], jnp.zeros((d, LANES - N_EXPERTS), F32)], axis=1)
    rw_hi = rw.astype(BF16)
    rw2 = jnp.stack([rw_hi, (rw - rw_hi.astype(F32)).astype(BF16)])
    xn, info, info_t, counts = _outproj_route(o_t, odd_w_out[0].astype(BF16), x2,
                                              row(ln_gain[1, 0]), row(ln_bias[1, 0]), rw2, s, tm)
    pos, tile_expert, n_used, pad_info, n_slots = _routing_plan(info_t, counts, n)
    pos_t = pos.reshape(TOP_K, n // tm, tm).transpose(1, 0, 2).reshape(-1)
    x_sorted = _scatter_rows(xn, pos_t, pad_info, n_slots, tm)
    y_sorted = _moe_ffn(x_sorted, tile_expert, n_used, wg_b, wu_b, wd_b)
    out = _combine_ln(xn, info, pos_t, y_sorted, row(ln_gain[1, 1]), row(ln_bias[1, 1]), tm)
    return out.reshape(b, s, d)
```
